```python
import jax, jax.numpy as jnp
from jax import lax
import numpy as np

D_MODEL = 2048
BATCH = 16
SEQ = 2048
DEPTH = 2

N_EVEN = (DEPTH + 1) // 2
N_ODD = DEPTH // 2

LRU_WIDTH = D_MODEL // 2
LRU_BLOCKS = 8
LRU_BW = LRU_WIDTH // LRU_BLOCKS
LRU_CONV = 4
LRU_C = 8.0

N_HEADS = 8
N_KV_HEADS = 2
HEAD_DIM = 128
ATT_WIDTH = N_HEADS * HEAD_DIM
IDX_HEADS = 8
IDX_DIM = 64
INDEX_TOPK = 256
Q_BLOCK = 64

ROPE_THETA = 500000.0
ROT_DIM_ATTN = HEAD_DIM // 4
ROT_DIM_IDX = IDX_DIM // 4
MAX_POS_OFFSET = 4096

HYB_SIZES = (LRU_WIDTH, LRU_WIDTH, ATT_WIDTH, N_KV_HEADS * HEAD_DIM, N_KV_HEADS * HEAD_DIM,
             IDX_HEADS * IDX_DIM, IDX_DIM, IDX_HEADS)
HYB_IN = sum(HYB_SIZES)
HYB_OUT = LRU_WIDTH + ATT_WIDTH

CFM_WIDTH = D_MODEL
CFM_CONV = 31

N_EXPERTS = 64
TOP_K = 8
N_GROUPS = 8
TOPK_GROUPS = 4
EXPERT_FF = 512
SHARED_FF = 512
ROUTED_SCALE = 2.5
MOE_BLOCK = 128

kernel_name = 'hybrid_rglru_dsa_conformer_moe'

F32 = jnp.float32


def rms_norm(x, g, eps=1e-6):
    xf = x.astype(F32)
    y = xf * lax.rsqrt(jnp.mean(xf * xf, axis=-1, keepdims=True) + eps)
    return (y * g.astype(F32)).astype(x.dtype)


def layer_norm(x, g, b, eps=1e-5):
    xf = x.astype(F32)
    mu = jnp.mean(xf, axis=-1, keepdims=True)
    xc = xf - mu
    var = jnp.mean(xc * xc, axis=-1, keepdims=True)
    return (xc * lax.rsqrt(var + eps) * g.astype(F32) + b.astype(F32)).astype(x.dtype)


def modulate(h, shift, scale):
    return h * (1.0 + scale[:, None, :]) + shift[:, None, :]


def causal_depthwise_conv(x, w, b):
    width = w.shape[0]
    y = lax.conv_general_dilated(x, w[:, None, :].astype(x.dtype), window_strides=(1,),
                                 padding=[(width - 1, 0)],
                                 dimension_numbers=('NWC', 'WIO', 'NWC'),
                                 feature_group_count=x.shape[-1])
    return y + b


def rope_tables(positions, rot_dim):
    inv = jnp.power(jnp.float32(ROPE_THETA), -jnp.arange(0, rot_dim, 2, dtype=F32) / rot_dim)
    ang = positions.astype(F32)[..., None] * inv
    return jnp.cos(ang), jnp.sin(ang)


def apply_partial_rope(x, cos, sin):
    rot = 2 * cos.shape[-1]
    xr, xp = x[..., :rot].astype(F32), x[..., rot:]
    x1, x2 = jnp.split(xr, 2, axis=-1)
    c = cos[:, :, None, :]
    s = sin[:, :, None, :]
    out = jnp.concatenate([x1 * c - x2 * s, x2 * c + x1 * s], axis=-1).astype(x.dtype)
    return jnp.concatenate([out, xp], axis=-1)


def swiglu(x, wg, wu, wd):
    return (jax.nn.silu(x @ wg) * (x @ wu)) @ wd


def rg_lru(xc, wa, ba, wx, bx, lam):
    b_, s_, cw = xc.shape
    xb = xc.reshape(b_, s_, LRU_BLOCKS, LRU_BW)
    gate_a = jax.nn.sigmoid(jnp.einsum('bsnd,nde->bsne', xb, wa).reshape(b_, s_, cw) + ba).astype(F32)
    gate_x = jax.nn.sigmoid(jnp.einsum('bsnd,nde->bsne', xb, wx).reshape(b_, s_, cw) + bx).astype(F32)
    log_a = -LRU_C * gate_a * jax.nn.softplus(-lam.astype(F32))
    a = jnp.exp(log_a)
    mult = jnp.sqrt(-jnp.expm1(2.0 * log_a))
    mult = jnp.where(jnp.arange(s_)[None, :, None] == 0, 1.0, mult)
    u = gate_x * xc.astype(F32) * mult

    def combine(left, right):
        a_l, b_l = left
        a_r, b_r = right
        return a_l * a_r, a_r * b_l + b_r

    _, h = lax.associative_scan(combine, (a, u), axis=1)
    return h.astype(xc.dtype)


def dsa_attention(q, k, v, iq, ik, iw, n_sel):
    b_, s_, h_, dh = q.shape
    g_ = k.shape[2]
    r_ = h_ // g_
    n_blocks = s_ // Q_BLOCK
    key_pos = jnp.arange(s_)
    ikf = ik.astype(F32)
    w_scale = (IDX_HEADS ** -0.5) * (IDX_DIM ** -0.5)

    def one_block(bi):
        start = bi * Q_BLOCK
        qb = lax.dynamic_slice_in_dim(q, start, Q_BLOCK, axis=1)
        iqb = lax.dynamic_slice_in_dim(iq, start, Q_BLOCK, axis=1).astype(F32)
        iwb = lax.dynamic_slice_in_dim(iw, start, Q_BLOCK, axis=1).astype(F32) * w_scale
        t = start + jnp.arange(Q_BLOCK)
        dots = jax.nn.relu(jnp.einsum('bqhd,bsd->bqhs', iqb, ikf))
        score = jnp.einsum('bqhs,bqh->bqs', dots, iwb)
        causal = key_pos[None, :] <= t[:, None]
        score = jnp.where(causal[None], score, -jnp.inf)
        _, sel = lax.top_k(score, n_sel)
        valid = sel <= t[None, :, None]
        kg = jax.vmap(lambda kk, ii: kk[ii])(k, sel)
        vg = jax.vmap(lambda vv, ii: vv[ii])(v, sel)
        qg = qb.reshape(b_, Q_BLOCK, g_, r_, dh)
        logits = jnp.einsum('bqgrd,bqkgd->bqgrk', qg, kg, preferred_element_type=F32) * (dh ** -0.5)
        logits = jnp.where(valid[:, :, None, None, :], logits, -jnp.inf)
        p = jax.nn.softmax(logits, axis=-1)
        o = jnp.einsum('bqgrk,bqkgd->bqgrd', p.astype(v.dtype), vg)
        return o.reshape(b_, Q_BLOCK, h_ * dh)

    out = lax.map(one_block, jnp.arange(n_blocks))
    return out.transpose(1, 0, 2, 3).reshape(b_, s_, h_ * dh)


def hybrid_mixer(h, cos_a, sin_a, cos_i, sin_i, n_sel, w_in, w_out, conv_w, conv_b,
                 wa, ba, wx, bx, lam):
    b_, s_, _ = h.shape
    proj = h @ w_in
    points = []
    acc = 0
    for size in HYB_SIZES[:-1]:
        acc += size
        points.append(acc)
    lx, lg, q, k, v, iq, ik, iw = jnp.split(proj, points, axis=-1)
    xc = causal_depthwise_conv(lx, conv_w, conv_b)
    y_lru = rg_lru(xc, wa, ba, wx, bx, lam) * jax.nn.gelu(lg)
    q = apply_partial_rope(q.reshape(b_, s_, N_HEADS, HEAD_DIM), cos_a, sin_a)
    k = apply_partial_rope(k.reshape(b_, s_, N_KV_HEADS, HEAD_DIM), cos_a, sin_a)
    v = v.reshape(b_, s_, N_KV_HEADS, HEAD_DIM)
    iq = apply_partial_rope(iq.reshape(b_, s_, IDX_HEADS, IDX_DIM), cos_i, sin_i)
    ik = apply_partial_rope(ik[:, :, None, :], cos_i, sin_i)[:, :, 0, :]
    y_att = dsa_attention(q, k, v, iq, ik, iw, n_sel)
    return jnp.concatenate([y_lru, y_att], axis=-1) @ w_out


def conformer_conv(h, w1, b1, dw, dwb, ln_g, ln_b, w2, b2):
    u = h @ w1 + b1
    a, g = jnp.split(u, 2, axis=-1)
    u = a * jax.nn.sigmoid(g)
    u = causal_depthwise_conv(u, dw, dwb)
    u = jax.nn.silu(layer_norm(u, ln_g, ln_b))
    return u @ w2 + b2


def routed_experts(hf, eidx, gw, wg, wu, wd):
    t_, d_ = hf.shape
    e_ = wg.shape[0]
    n_assign = t_ * TOP_K
    flat_e = eidx.reshape(-1)
    flat_tok = jnp.repeat(jnp.arange(t_, dtype=jnp.int32), TOP_K)
    flat_w = gw.reshape(-1)
    order = jnp.argsort(flat_e)
    e_s, tok_s, w_s = flat_e[order], flat_tok[order], flat_w[order]
    counts = jnp.bincount(flat_e, length=e_)
    padded = (counts + MOE_BLOCK - 1) // MOE_BLOCK * MOE_BLOCK
    start = jnp.cumsum(counts) - counts
    ends = jnp.cumsum(padded)
    pstart = ends - padded
    dest = pstart[e_s] + (jnp.arange(n_assign) - start[e_s])
    n_blocks = -(-n_assign // MOE_BLOCK) + e_
    n_rows = n_blocks * MOE_BLOCK
    row_tok = jnp.full((n_rows,), t_, jnp.int32).at[dest].set(tok_s)
    row_w = jnp.zeros((n_rows,), F32).at[dest].set(w_s)
    block_exp = jnp.minimum(jnp.searchsorted(ends, jnp.arange(n_blocks) * MOE_BLOCK, side='right'), e_ - 1)
    h_pad = jnp.concatenate([hf, jnp.zeros((1, d_), hf.dtype)], axis=0)

    def step(acc, blk):
        rows, rw, e = blk
        y = swiglu(h_pad[rows], wg[e], wu[e], wd[e])
        return acc.at[rows].add(y.astype(F32) * rw[:, None]), None

    acc, _ = lax.scan(step, jnp.zeros((t_ + 1, d_), F32),
                      (row_tok.reshape(n_blocks, MOE_BLOCK), row_w.reshape(n_blocks, MOE_BLOCK), block_exp))
    return acc[:t_].astype(hf.dtype)


def moe_ffn(h, w_router, e_bias, wg, wu, wd, swg, swu, swd):
    b_, s_, d_ = h.shape
    t_ = b_ * s_
    hf = h.reshape(t_, d_)
    scores = jax.nn.sigmoid(jnp.dot(hf.astype(F32), w_router.astype(F32)))
    choice = scores + e_bias.astype(F32)
    grp = choice.reshape(t_, N_GROUPS, N_EXPERTS // N_GROUPS)
    grp_score = lax.top_k(grp, 2)[0].sum(-1)
    _, gidx = lax.top_k(grp_score, TOPK_GROUPS)
    gmask = jnp.any(jnp.arange(N_GROUPS)[None, None, :] == gidx[..., None], axis=1)
    emask = jnp.repeat(gmask, N_EXPERTS // N_GROUPS, axis=1)
    _, eidx = lax.top_k(jnp.where(emask, choice, -jnp.inf), TOP_K)
    gw = jnp.take_along_axis(scores, eidx, axis=1)
    gw = gw / jnp.sum(gw, axis=-1, keepdims=True) * ROUTED_SCALE
    routed = routed_experts(hf, eidx, gw, wg, wu, wd)
    shared = swiglu(hf, swg, swu, swd)
    return (routed + shared).reshape(b_, s_, d_)


def setup_inputs(seed: int = 0) -> dict:
    key = jax.random.key(seed)
    ks = iter(jax.random.split(key, 48))
    D = D_MODEL

    def nrm(shape, scale):
        return jax.random.normal(next(ks), shape, F32) * scale

    x = nrm((BATCH, SEQ, D), 1.0)
    c = nrm((BATCH, D), 1.0)
    offs = jax.random.randint(next(ks), (BATCH, 1), 0, MAX_POS_OFFSET, dtype=jnp.int32)
    positions = jnp.arange(SEQ, dtype=jnp.int32)[None, :] + offs
    mod_w = nrm((DEPTH, D, 6 * D), 0.5 * D ** -0.5)
    mod_b = nrm((DEPTH, 6 * D), 0.02)
    norm_mix = 1.0 + nrm((DEPTH, D), 0.02)
    norm_ffn = 1.0 + nrm((DEPTH, D), 0.02)
    hyb_w_in = nrm((N_EVEN, D, HYB_IN), D ** -0.5)
    hyb_w_out = nrm((N_EVEN, HYB_OUT, D), HYB_OUT ** -0.5)
    lru_conv_w = nrm((N_EVEN, LRU_CONV, LRU_WIDTH), LRU_CONV ** -0.5)
    lru_conv_b = nrm((N_EVEN, LRU_WIDTH), 0.01)
    lru_wa = nrm((N_EVEN, LRU_BLOCKS, LRU_BW, LRU_BW), LRU_BW ** -0.5)
    lru_ba = nrm((N_EVEN, LRU_WIDTH), 0.01)
    lru_wx = nrm((N_EVEN, LRU_BLOCKS, LRU_BW, LRU_BW), LRU_BW ** -0.5)
    lru_bx = nrm((N_EVEN, LRU_WIDTH), 0.01)
    u = jax.random.uniform(next(ks), (N_EVEN, LRU_WIDTH), F32, minval=0.9, maxval=0.999)
    s = u ** (1.0 / LRU_C)
    lru_lambda = jnp.log(s) - jnp.log1p(-s)
    cfm_w1 = nrm((N_ODD, D, 2 * CFM_WIDTH), D ** -0.5)
    cfm_b1 = nrm((N_ODD, 2 * CFM_WIDTH), 0.01)
    cfm_dw = nrm((N_ODD, CFM_CONV, CFM_WIDTH), CFM_CONV ** -0.5)
    cfm_dwb = nrm((N_ODD, CFM_WIDTH), 0.01)
    cfm_ln_g = 1.0 + nrm((N_ODD, CFM_WIDTH), 0.02)
    cfm_ln_b = nrm((N_ODD, CFM_WIDTH), 0.01)
    cfm_w2 = nrm((N_ODD, CFM_WIDTH, D), CFM_WIDTH ** -0.5)
    cfm_b2 = nrm((N_ODD, D), 0.01)
    moe_router = nrm((DEPTH, D, N_EXPERTS), D ** -0.5)
    moe_bias = nrm((DEPTH, N_EXPERTS), 0.01)
    moe_wg = nrm((DEPTH, N_EXPERTS, D, EXPERT_FF), D ** -0.5)
    moe_wu = nrm((DEPTH, N_EXPERTS, D, EXPERT_FF), D ** -0.5)
    moe_wd = nrm((DEPTH, N_EXPERTS, EXPERT_FF, D), EXPERT_FF ** -0.5)
    sh_wg = nrm((DEPTH, D, SHARED_FF), D ** -0.5)
    sh_wu = nrm((DEPTH, D, SHARED_FF), D ** -0.5)
    sh_wd = nrm((DEPTH, SHARED_FF, D), SHARED_FF ** -0.5)
    final_norm = 1.0 + nrm((D,), 0.02)
    return {'x': x, 'c': c, 'positions': positions, 'mod_w': mod_w, 'mod_b': mod_b,
            'norm_mix': norm_mix, 'norm_ffn': norm_ffn, 'hyb_w_in': hyb_w_in, 'hyb_w_out': hyb_w_out,
            'lru_conv_w': lru_conv_w, 'lru_conv_b': lru_conv_b, 'lru_wa': lru_wa, 'lru_ba': lru_ba,
            'lru_wx': lru_wx, 'lru_bx': lru_bx, 'lru_lambda': lru_lambda,
            'cfm_w1': cfm_w1, 'cfm_b1': cfm_b1, 'cfm_dw': cfm_dw, 'cfm_dwb': cfm_dwb,
            'cfm_ln_g': cfm_ln_g, 'cfm_ln_b': cfm_ln_b, 'cfm_w2': cfm_w2, 'cfm_b2': cfm_b2,
            'moe_router': moe_router, 'moe_bias': moe_bias, 'moe_wg': moe_wg, 'moe_wu': moe_wu,
            'moe_wd': moe_wd, 'sh_wg': sh_wg, 'sh_wu': sh_wu, 'sh_wd': sh_wd, 'final_norm': final_norm}


def reference(x, c, positions, mod_w, mod_b, norm_mix, norm_ffn, hyb_w_in, hyb_w_out,
              lru_conv_w, lru_conv_b, lru_wa, lru_ba, lru_wx, lru_bx, lru_lambda,
              cfm_w1, cfm_b1, cfm_dw, cfm_dwb, cfm_ln_g, cfm_ln_b, cfm_w2, cfm_b2,
              moe_router, moe_bias, moe_wg, moe_wu, moe_wd, sh_wg, sh_wu, sh_wd, final_norm):
    seq = x.shape[1]
    n_sel = min(INDEX_TOPK, seq // 4)
    cos_a, sin_a = rope_tables(positions, ROT_DIM_ATTN)
    cos_i, sin_i = rope_tables(positions, ROT_DIM_IDX)
    cond = jax.nn.silu(c)
    for l in range(DEPTH):
        mod = cond @ mod_w[l] + mod_b[l]
        sh_m, sc_m, g_m, sh_f, sc_f, g_f = jnp.split(mod, 6, axis=-1)
        h = modulate(rms_norm(x, norm_mix[l]), sh_m, sc_m)
        j = l // 2
        if l % 2 == 0:
            y = hybrid_mixer(h, cos_a, sin_a, cos_i, sin_i, n_sel, hyb_w_in[j], hyb_w_out[j],
                             lru_conv_w[j], lru_conv_b[j], lru_wa[j], lru_ba[j], lru_wx[j],
                             lru_bx[j], lru_lambda[j])
        else:
            y = conformer_conv(h, cfm_w1[j], cfm_b1[j], cfm_dw[j], cfm_dwb[j], cfm_ln_g[j],
                               cfm_ln_b[j], cfm_w2[j], cfm_b2[j])
        x = x + g_m[:, None, :] * y
        h = modulate(rms_norm(x, norm_ffn[l]), sh_f, sc_f)
        x = x + g_f[:, None, :] * moe_ffn(h, moe_router[l], moe_bias[l], moe_wg[l], moe_wu[l],
                                          moe_wd[l], sh_wg[l], sh_wu[l], sh_wd[l])
    return rms_norm(x, final_norm)
```

```python
import functools

import jax
import jax.numpy as jnp
from jax import lax
from jax.experimental import pallas as pl
from jax.experimental.pallas import tpu as pltpu

F32 = jnp.float32
BF16 = jnp.bfloat16

LRU_WIDTH = 1024
LRU_BW = 128
LRU_CONV = 4
LRU_C = 8.0
N_HEADS = 8
N_KV_HEADS = 2
HEAD_DIM = 128
ATT_WIDTH = N_HEADS * HEAD_DIM
KV_WIDTH = N_KV_HEADS * HEAD_DIM
IDX_HEADS = 8
IDX_DIM = 64
INDEX_TOPK = 256
ROPE_THETA = 500000.0
ROT_ATTN = HEAD_DIM // 4
ROT_IDX = IDX_DIM // 4
CFM_CONV = 31
N_EXPERTS = 64
TOP_K = 8
N_GROUPS = 8
TOPK_GROUPS = 4
ROUTED_SCALE = 2.5

MAIN_COLS = 2 * LRU_WIDTH + ATT_WIDTH + 2 * KV_WIDTH
IDX_COLS = IDX_HEADS * IDX_DIM + IDX_DIM + IDX_HEADS
IDX_PAD = 640

MOE_ROWS = 512
LANES = 128
VMEM_LIMIT = 56 * 1024 * 1024
INT_MIN = -2147483648
NEG_BIG = -1e30

SH_M, SC_M, G_M, SH_F, SC_F, G_F = range(6)


def _cp(sem):
    return pltpu.CompilerParams(dimension_semantics=sem, vmem_limit_bytes=VMEM_LIMIT)


def _norm_mod(x, g, mod_ref, sh_row, sc_row):
    ms = jnp.mean(x * x, axis=-1, keepdims=True)
    y = x * lax.rsqrt(ms + 1e-6) * g
    return y * (1.0 + mod_ref[0, sc_row:sc_row + 1, :]) + mod_ref[0, sh_row:sh_row + 1, :]


def _mod_body(c_ref, w_ref, b_ref, o_ref):
    c = c_ref[...]
    cond = c * jax.nn.sigmoid(c)
    o_ref[0] = jnp.dot(cond, w_ref[0], preferred_element_type=F32,
                       precision=lax.Precision.HIGHEST) + b_ref[0]


def _modulation(c, mod_w, mod_b):
    depth, d, n = mod_w.shape
    b = c.shape[0]
    tn = 512
    return pl.pallas_call(
        _mod_body,
        grid=(depth, n // tn),
        in_specs=[pl.BlockSpec((b, d), lambda l, j: (0, 0)),
                  pl.BlockSpec((1, d, tn), lambda l, j: (l, 0, j)),
                  pl.BlockSpec((1, 1, tn), lambda l, j: (l, 0, j))],
        out_specs=pl.BlockSpec((1, b, tn), lambda l, j: (l, 0, j)),
        out_shape=jax.ShapeDtypeStruct((depth, b, n), F32),
        compiler_params=_cp(("arbitrary", "arbitrary")),
        name="modulation",
    )(c, mod_w, mod_b.reshape(depth, 1, n))


def _nm_matmul_body(x_ref, g_ref, mod_ref, w_ref, o_ref, h_scr):
    @pl.when(pl.program_id(1) == 0)
    def _():
        h_scr[...] = _norm_mod(x_ref[...], g_ref[...], mod_ref, SH_M, SC_M).astype(BF16)

    o_ref[...] = jnp.dot(h_scr[...], w_ref[...], preferred_element_type=F32).astype(o_ref.dtype)


def _nm_matmul(x2, g, mod, w, seq, tm, tn, out_dtype, name):
    t, d = x2.shape
    n = w.shape[1]
    return pl.pallas_call(
        _nm_matmul_body,
        grid=(t // tm, n // tn),
        in_specs=[pl.BlockSpec((tm, d), lambda i, j: (i, 0)),
                  pl.BlockSpec((1, d), lambda i, j: (0, 0)),
                  pl.BlockSpec((1, 6, d), lambda i, j: (i * tm // seq, 0, 0)),
                  pl.BlockSpec((d, tn), lambda i, j: (0, j))],
        out_specs=pl.BlockSpec((tm, tn), lambda i, j: (i, j)),
        out_shape=jax.ShapeDtypeStruct((t, n), out_dtype),
        scratch_shapes=[pltpu.VMEM((tm, d), BF16)],
        compiler_params=_cp(("arbitrary", "arbitrary")),
        name=name,
    )(x2, g.reshape(1, d), mod, w)


def _nm_glu_body(x_ref, g_ref, mod_ref, wa_ref, wg_ref, ba_ref, bg_ref, o_ref, h_scr):
    @pl.when(pl.program_id(1) == 0)
    def _():
        h_scr[...] = _norm_mod(x_ref[...], g_ref[...], mod_ref, SH_M, SC_M).astype(BF16)

    h = h_scr[...]
    a = jnp.dot(h, wa_ref[...], preferred_element_type=F32) + ba_ref[...]
    gt = jnp.dot(h, wg_ref[...], preferred_element_type=F32) + bg_ref[...]
    o_ref[...] = (a * jax.nn.sigmoid(gt)).astype(o_ref.dtype)


def _nm_glu(x2, g, mod, w1, b1, seq, tm, tn):
    t, d = x2.shape
    n = w1.shape[1] // 2
    nj = n // tn
    return pl.pallas_call(
        _nm_glu_body,
        grid=(t // tm, nj),
        in_specs=[pl.BlockSpec((tm, d), lambda i, j: (i, 0)),
                  pl.BlockSpec((1, d), lambda i, j: (0, 0)),
                  pl.BlockSpec((1, 6, d), lambda i, j: (i * tm // seq, 0, 0)),
                  pl.BlockSpec((d, tn), lambda i, j: (0, j)),
                  pl.BlockSpec((d, tn), lambda i, j: (0, j + nj)),
                  pl.BlockSpec((1, tn), lambda i, j: (0, j)),
                  pl.BlockSpec((1, tn), lambda i, j: (0, j + nj))],
        out_specs=pl.BlockSpec((tm, tn), lambda i, j: (i, j)),
        out_shape=jax.ShapeDtypeStruct((t, n), BF16),
        scratch_shapes=[pltpu.VMEM((tm, d), BF16)],
        compiler_params=_cp(("arbitrary", "arbitrary")),
        name="cfm_in_glu",
    )(x2, g.reshape(1, d), mod, w1, w1, b1.reshape(1, 2 * n), b1.reshape(1, 2 * n))


def _proj_res_body(*refs, n_parts, gate_row):
    part_refs = refs[:n_parts]
    w_ref, b_ref, x_ref, mod_ref, o_ref = refs[n_parts:]
    acc = None
    off = 0
    for p in part_refs:
        k = p.shape[1]
        y = jnp.dot(p[...], w_ref[off:off + k, :], preferred_element_type=F32)
        acc = y if acc is None else acc + y
        off += k
    y = acc + b_ref[...]
    o_ref[...] = x_ref[...] + mod_ref[0, gate_row:gate_row + 1, :] * y


def _proj_residual(parts, w, bias, x2, mod, gate_row, seq, tm, name):
    t, d = x2.shape
    n = w.shape[1]
    in_specs = [pl.BlockSpec((tm, p.shape[1]), lambda i: (i, 0)) for p in parts]
    in_specs += [pl.BlockSpec(w.shape, lambda i: (0, 0)),
                 pl.BlockSpec((1, n), lambda i: (0, 0)),
                 pl.BlockSpec((tm, d), lambda i: (i, 0)),
                 pl.BlockSpec((1, 6, d), lambda i: (i * tm // seq, 0, 0))]
    return pl.pallas_call(
        functools.partial(_proj_res_body, n_parts=len(parts), gate_row=gate_row),
        grid=(t // tm,),
        in_specs=in_specs,
        out_specs=pl.BlockSpec((tm, n), lambda i: (i, 0)),
        out_shape=jax.ShapeDtypeStruct((t, n), F32),
        compiler_params=_cp(("arbitrary",)),
        name=name,
    )(*parts, w, bias.reshape(1, n), x2, mod)


def _shift_rows(x, d, fill, row):
    return jnp.where(row >= d, pltpu.roll(x, d, 0), fill)


def _lru_body(lx_ref, lg_ref, cw_ref, cb_ref, wa_ref, ba_ref, wx_ref, bx_ref, lam_ref, o_ref):
    s, cw = lx_ref.shape
    x = lx_ref[...].astype(F32)
    row = lax.broadcasted_iota(jnp.int32, (s, cw), 0)
    xc = x * cw_ref[LRU_CONV - 1:LRU_CONV, :] + cb_ref[...]
    for d in range(1, LRU_CONV):
        xc = xc + _shift_rows(x, d, 0.0, row) * cw_ref[LRU_CONV - 1 - d:LRU_CONV - d, :]
    xb = xc.astype(BF16)
    ga = jax.nn.sigmoid(jnp.dot(xb, wa_ref[0].astype(BF16), preferred_element_type=F32) + ba_ref[...])
    gx = jax.nn.sigmoid(jnp.dot(xb, wx_ref[0].astype(BF16), preferred_element_type=F32) + bx_ref[...])
    z = -lam_ref[...]
    softplus = jnp.maximum(z, 0.0) + jnp.log(1.0 + jnp.exp(-jnp.abs(z)))
    log_a = (-LRU_C) * ga * softplus
    a = jnp.exp(log_a)
    mult = jnp.sqrt(1.0 - a * a)
    mult = jnp.where(row == 0, 1.0, mult)
    b = gx * xc * mult
    d = 1
    while d < s:
        a_sh = _shift_rows(a, d, 1.0, row)
        b_sh = _shift_rows(b, d, 0.0, row)
        b = a * b_sh + b
        a = a * a_sh
        d *= 2
    lg = lg_ref[...].astype(F32)
    gelu = 0.5 * lg * (1.0 + jnp.tanh(0.7978845608028654 * (lg + 0.044715 * lg * lg * lg)))
    o_ref[...] = (b * gelu).astype(o_ref.dtype)


def _rg_lru(proj, conv_w, conv_b, wa, ba, wx, bx, lam, batch, seq):
    cw = LRU_BW
    nc = LRU_WIDTH // cw
    vec = lambda v: v.reshape(1, LRU_WIDTH)
    vspec = pl.BlockSpec((1, cw), lambda b, c: (0, c))
    return pl.pallas_call(
        _lru_body,
        grid=(batch, nc),
        in_specs=[pl.BlockSpec((seq, cw), lambda b, c: (b, c)),
                  pl.BlockSpec((seq, cw), lambda b, c: (b, c + nc)),
                  pl.BlockSpec((LRU_CONV, cw), lambda b, c: (0, c)),
                  vspec,
                  pl.BlockSpec((1, cw, cw), lambda b, c: (c, 0, 0)),
                  vspec,
                  pl.BlockSpec((1, cw, cw), lambda b, c: (c, 0, 0)),
                  vspec, vspec],
        out_specs=pl.BlockSpec((seq, cw), lambda b, c: (b, c)),
        out_shape=jax.ShapeDtypeStruct((batch * seq, LRU_WIDTH), BF16),
        compiler_params=_cp(("arbitrary", "arbitrary")),
        name="rg_lru",
    )(proj, proj, conv_w, vec(conv_b), wa, vec(ba), wx, vec(bx), vec(lam))


def _rope(x, c, s1, s2, half):
    w = x.shape[1]
    return x * c + pltpu.roll(x, w - half, 1) * s1 + pltpu.roll(x, half, 1) * s2


def _sort_key(x):
    bits = pltpu.bitcast(x, jnp.int32)
    return bits ^ (jnp.right_shift(bits, 31) & 0x7FFFFFFF)


def _dsa_body(q_ref, k_ref, v_ref, iq_ref, ik_ref, ta_q_ref, ta_k_ref, ti_q_ref, ti_k_ref,
              o_ref, kr_scr, ikr_scr, key_scr, bias_scr, *, n_sel):
    qi = pl.program_id(1)
    tq = q_ref.shape[0]
    s = k_ref.shape[0]
    groups = N_KV_HEADS
    per_group = N_HEADS // N_KV_HEADS

    @pl.when(qi == 0)
    def _():
        ca, sa1, sa2 = ta_k_ref[0], ta_k_ref[1], ta_k_ref[2]
        for g in range(groups):
            kg = k_ref[:, g * HEAD_DIM:(g + 1) * HEAD_DIM].astype(F32)
            kr_scr[:, g * HEAD_DIM:(g + 1) * HEAD_DIM] = _rope(kg, ca, sa1, sa2, ROT_ATTN // 2).astype(BF16)
        ikp = _rope(ik_ref[...], ti_k_ref[0], ti_k_ref[1], ti_k_ref[2], ROT_IDX // 2)
        ikr_scr[...] = ikp[:, :IDX_DIM].astype(BF16)

    ci, si1, si2 = ti_q_ref[0], ti_q_ref[1], ti_q_ref[2]
    w_scale = (IDX_HEADS ** -0.5) * (IDX_DIM ** -0.5)
    iw = iq_ref[:, IDX_HEADS * IDX_DIM + IDX_DIM:IDX_HEADS * IDX_DIM + LANES] * w_scale
    ikr = ikr_scr[...]
    score = jnp.zeros((tq, s), F32)
    for hp in range(IDX_HEADS // 2):
        pair = _rope(iq_ref[:, hp * LANES:(hp + 1) * LANES], ci, si1, si2, ROT_IDX // 2).astype(BF16)
        for sub in range(2):
            h = 2 * hp + sub
            qh = pair[:, sub * IDX_DIM:(sub + 1) * IDX_DIM]
            dots = lax.dot_general(qh, ikr, (((1,), (1,)), ((), ())), preferred_element_type=F32)
            score = score + jnp.maximum(dots, 0.0) * iw[:, h:h + 1]

    t_row = qi * tq + lax.broadcasted_iota(jnp.int32, (tq, s), 0)
    col = lax.broadcasted_iota(jnp.int32, (tq, s), 1)
    causal = col <= t_row
    key_scr[...] = jnp.where(causal, _sort_key(score), INT_MIN)

    kf = float(n_sel)

    def count_ge(cand):
        return jnp.sum((key_scr[...] >= cand).astype(F32), axis=1, keepdims=True)

    tau0 = jnp.where(count_ge(jnp.zeros((tq, 1), jnp.int32)) >= kf, 0, INT_MIN).astype(jnp.int32)

    def bis(i, tau):
        cand = tau + jnp.left_shift(jnp.int32(1), 30 - i)
        return jnp.where(count_ge(cand) >= kf, cand, tau)

    tau = lax.fori_loop(0, 31, bis, tau0)

    keys = key_scr[...]
    n_gt = jnp.sum((keys > tau).astype(F32), axis=1, keepdims=True)
    n_ge = jnp.sum((keys >= tau).astype(F32), axis=1, keepdims=True)
    need = kf - n_gt
    tie = keys == tau
    excess = jnp.logical_and(n_ge > kf, tau > INT_MIN)
    any_excess = jnp.max(excess.astype(F32)) > 0.0

    bias_scr[...] = jnp.where(jnp.logical_and(keys >= tau, causal), 0.0, NEG_BIG)

    @pl.when(any_excess)
    def _():
        nbits = s.bit_length()

        def jb(i, jcur):
            cand = jcur + jnp.left_shift(jnp.int32(1), nbits - 1 - i)
            cnt = jnp.sum(jnp.logical_and(tie, col < cand).astype(F32), axis=1, keepdims=True)
            return jnp.where(cnt <= need, cand, jcur)

        jlim = lax.fori_loop(0, nbits, jb, jnp.zeros((tq, 1), jnp.int32))
        jlim = jnp.where(excess, jlim, s)
        sel = jnp.logical_or(keys > tau, jnp.logical_and(tie, col < jlim))
        bias_scr[...] = jnp.where(jnp.logical_and(sel, causal), 0.0, NEG_BIG)

    ca, sa1, sa2 = ta_q_ref[0], ta_q_ref[1], ta_q_ref[2]
    scale = HEAD_DIM ** -0.5
    for h in range(N_HEADS):
        g = h // per_group
        qh = _rope(q_ref[:, h * HEAD_DIM:(h + 1) * HEAD_DIM].astype(F32), ca, sa1, sa2, ROT_ATTN // 2)
        qh = qh.astype(BF16)
        kg = kr_scr[:, g * HEAD_DIM:(g + 1) * HEAD_DIM]
        logits = lax.dot_general(qh, kg, (((1,), (1,)), ((), ())), preferred_element_type=F32)
        logits = logits * scale + bias_scr[...]
        m = jnp.max(logits, axis=1, keepdims=True)
        p = jnp.exp(logits - m)
        den = jnp.sum(p, axis=1, keepdims=True)
        vg = v_ref[:, g * HEAD_DIM:(g + 1) * HEAD_DIM]
        o = jnp.dot(p.astype(BF16), vg, preferred_element_type=F32) / den
        o_ref[:, h * HEAD_DIM:(h + 1) * HEAD_DIM] = o.astype(o_ref.dtype)


def _dsa(proj, idx, tab_a, tab_i, batch, seq, n_sel, tq):
    nq = seq // tq
    q_col = 2 * LRU_WIDTH // ATT_WIDTH
    k_col = (2 * LRU_WIDTH + ATT_WIDTH) // KV_WIDTH
    ik_col = IDX_HEADS * IDX_DIM // LANES
    return pl.pallas_call(
        functools.partial(_dsa_body, n_sel=n_sel),
        grid=(batch, nq),
        in_specs=[pl.BlockSpec((tq, ATT_WIDTH), lambda b, i: (b * nq + i, q_col)),
                  pl.BlockSpec((seq, KV_WIDTH), lambda b, i: (b, k_col)),
                  pl.BlockSpec((seq, KV_WIDTH), lambda b, i: (b, k_col + 1)),
                  pl.BlockSpec((tq, IDX_PAD), lambda b, i: (b * nq + i, 0)),
                  pl.BlockSpec((seq, LANES), lambda b, i: (b, ik_col)),
                  pl.BlockSpec((3, tq, LANES), lambda b, i: (0, b * nq + i, 0)),
                  pl.BlockSpec((3, seq, LANES), lambda b, i: (0, b, 0)),
                  pl.BlockSpec((3, tq, LANES), lambda b, i: (0, b * nq + i, 0)),
                  pl.BlockSpec((3, seq, LANES), lambda b, i: (0, b, 0))],
        out_specs=pl.BlockSpec((tq, ATT_WIDTH), lambda b, i: (b * nq + i, 0)),
        out_shape=jax.ShapeDtypeStruct((batch * seq, ATT_WIDTH), BF16),
        scratch_shapes=[pltpu.VMEM((seq, KV_WIDTH), BF16),
                        pltpu.VMEM((seq, IDX_DIM), BF16),
                        pltpu.VMEM((tq, seq), jnp.int32),
                        pltpu.VMEM((tq, seq), F32)],
        compiler_params=_cp(("arbitrary", "arbitrary")),
        name="dsa_attention",
    )(proj, proj, proj, idx, idx, tab_a, tab_a, tab_i, tab_i)


def _rope_tables(positions, rot_dim, period):
    half = rot_dim // 2
    inv = jnp.power(jnp.float32(ROPE_THETA), -jnp.arange(0, rot_dim, 2, dtype=F32) / rot_dim)
    ang = positions.astype(F32).reshape(-1, 1) * inv
    cos, sin = jnp.cos(ang), jnp.sin(ang)
    t = cos.shape[0]
    ones = jnp.ones((t, period - rot_dim), F32)
    zeros = jnp.zeros((t, period - rot_dim), F32)
    zh = jnp.zeros((t, half), F32)
    c = jnp.concatenate([cos, cos, ones], axis=1)
    s1 = jnp.concatenate([-sin, zh, zeros], axis=1)
    s2 = jnp.concatenate([zh, sin, zeros], axis=1)
    reps = LANES // period
    return jnp.stack([jnp.tile(c, (1, reps)), jnp.tile(s1, (1, reps)), jnp.tile(s2, (1, reps))])


HALO = 32


def _cfm_conv_body(u_ref, halo_ref, dw_ref, dwb_ref, g_ref, b_ref, o_ref, ext_scr, acc_scr, *, seq, rc, cc):
    tm, c = u_ref.shape
    i = pl.program_id(0)
    at_start = (i * tm) % seq == 0
    halo = jnp.where(at_start, 0.0, halo_ref[...].astype(F32))
    ext_scr[0:HALO, :] = halo
    ext_scr[HALO:, :] = u_ref[...].astype(F32)
    base = HALO - (CFM_CONV - 1)

    def chunk(r, carry):
        r0 = pl.multiple_of(r * rc, rc)
        for c0 in range(0, c, cc):
            win = ext_scr[pl.ds(r0, rc + HALO), c0:c0 + cc]
            acc = jnp.zeros((rc, cc), F32) + dwb_ref[:, c0:c0 + cc]
            for sub in range(8):
                rolled = win if sub == 0 else pltpu.roll(win, rc + HALO - sub, 0)
                for j in range(CFM_CONV):
                    off = base + j
                    if off % 8 == sub:
                        a0 = off - sub
                        acc = acc + rolled[a0:a0 + rc, :] * dw_ref[j:j + 1, c0:c0 + cc]
            acc_scr[pl.ds(r0, rc), c0:c0 + cc] = acc
        return carry

    lax.fori_loop(0, tm // rc, chunk, 0)
    y = acc_scr[...]
    mu = jnp.mean(y, axis=-1, keepdims=True)
    yc = y - mu
    var = jnp.mean(yc * yc, axis=-1, keepdims=True)
    z = yc * lax.rsqrt(var + 1e-5) * g_ref[...] + b_ref[...]
    o_ref[...] = (z * jax.nn.sigmoid(z)).astype(o_ref.dtype)


def _cfm_conv(u, dw, dwb, ln_g, ln_b, seq, tm):
    t, c = u.shape
    hb = tm // HALO
    vec = lambda v: v.reshape(1, c)
    vspec = pl.BlockSpec((1, c), lambda i: (0, 0))
    return pl.pallas_call(
        functools.partial(_cfm_conv_body, seq=seq, rc=32, cc=256),
        grid=(t // tm,),
        in_specs=[pl.BlockSpec((tm, c), lambda i: (i, 0)),
                  pl.BlockSpec((HALO, c), lambda i: (jnp.maximum(i * hb - 1, 0), 0)),
                  pl.BlockSpec((CFM_CONV, c), lambda i: (0, 0)),
                  vspec, vspec, vspec],
        out_specs=pl.BlockSpec((tm, c), lambda i: (i, 0)),
        out_shape=jax.ShapeDtypeStruct((t, c), BF16),
        scratch_shapes=[pltpu.VMEM((tm + HALO, c), F32), pltpu.VMEM((tm, c), F32)],
        compiler_params=_cp(("arbitrary",)),
        name="cfm_conv_ln",
    )(u, u, dw, vec(dwb), vec(ln_g), vec(ln_b))


def _moe_pre_body(x_ref, g_ref, mod_ref, wr_ref, swg_ref, swu_ref, swd_ref, h_ref, lg_ref, sh_ref):
    h = _norm_mod(x_ref[...], g_ref[...], mod_ref, SH_F, SC_F)
    hb = h.astype(BF16)
    h_ref[...] = hb
    lg_ref[...] = jnp.dot(h, wr_ref[...], preferred_element_type=F32, precision=lax.Precision.HIGHEST)
    gt = jnp.dot(hb, swg_ref[...], preferred_element_type=F32)
    up = jnp.dot(hb, swu_ref[...], preferred_element_type=F32)
    mid = (gt * jax.nn.sigmoid(gt) * up).astype(BF16)
    sh_ref[...] = jnp.dot(mid, swd_ref[...], preferred_element_type=F32)


def _moe_pre(x2, g, mod, w_router, swg, swu, swd, seq, tm):
    t, d = x2.shape
    e = w_router.shape[1]
    ff = swg.shape[1]
    full = lambda a: pl.BlockSpec(a.shape, lambda i: (0, 0))
    return pl.pallas_call(
        _moe_pre_body,
        grid=(t // tm,),
        in_specs=[pl.BlockSpec((tm, d), lambda i: (i, 0)),
                  pl.BlockSpec((1, d), lambda i: (0, 0)),
                  pl.BlockSpec((1, 6, d), lambda i: (i * tm // seq, 0, 0)),
                  full(w_router), full(swg), full(swu), full(swd)],
        out_specs=[pl.BlockSpec((tm, d), lambda i: (i, 0)),
                   pl.BlockSpec((tm, e), lambda i: (i, 0)),
                   pl.BlockSpec((tm, d), lambda i: (i, 0))],
        out_shape=[jax.ShapeDtypeStruct((t, d), BF16),
                   jax.ShapeDtypeStruct((t, e), F32),
                   jax.ShapeDtypeStruct((t, d), F32)],
        compiler_params=_cp(("arbitrary",)),
        name="moe_pre",
    )(x2, g.reshape(1, d), mod, w_router, swg, swu, swd)


def _experts_body(be_ref, nu_ref, x_ref, wg_ref, wu_ref, wd_ref, o_ref, wg_scr, wu_scr, wd_scr):
    i = pl.program_id(0)

    @pl.when(i < nu_ref[0])
    def _():
        prev = be_ref[jnp.maximum(i - 1, 0)]
        changed = jnp.logical_or(i == 0, be_ref[i] != prev)

        @pl.when(changed)
        def _():
            wg_scr[...] = wg_ref[0].astype(BF16)
            wu_scr[...] = wu_ref[0].astype(BF16)
            wd_scr[...] = wd_ref[0].astype(BF16)

        x = x_ref[...]
        gt = jnp.dot(x, wg_scr[...], preferred_element_type=F32)
        up = jnp.dot(x, wu_scr[...], preferred_element_type=F32)
        mid = (gt * jax.nn.sigmoid(gt) * up).astype(BF16)
        o_ref[...] = jnp.dot(mid, wd_scr[...], preferred_element_type=F32).astype(o_ref.dtype)

    @pl.when(i >= nu_ref[0])
    def _():
        o_ref[...] = jnp.zeros_like(o_ref)


def _experts(x_sorted, block_exp, n_used, wg, wu, wd, tm):
    n_rows, d = x_sorted.shape
    ff = wg.shape[2]
    n_blocks = n_rows // tm
    grid_spec = pltpu.PrefetchScalarGridSpec(
        num_scalar_prefetch=2,
        grid=(n_blocks,),
        in_specs=[pl.BlockSpec((tm, d), lambda i, be, nu: (i, 0)),
                  pl.BlockSpec((1, d, ff), lambda i, be, nu: (be[i], 0, 0)),
                  pl.BlockSpec((1, d, ff), lambda i, be, nu: (be[i], 0, 0)),
                  pl.BlockSpec((1, ff, d), lambda i, be, nu: (be[i], 0, 0))],
        out_specs=pl.BlockSpec((tm, d), lambda i, be, nu: (i, 0)),
        scratch_shapes=[pltpu.VMEM((d, ff), BF16), pltpu.VMEM((d, ff), BF16), pltpu.VMEM((ff, d), BF16)],
    )
    return pl.pallas_call(
        _experts_body,
        grid_spec=grid_spec,
        out_shape=jax.ShapeDtypeStruct((n_rows, d), BF16),
        compiler_params=_cp(("arbitrary",)),
        name="moe_experts",
    )(block_exp, n_used, x_sorted, wg, wu, wd)


def _route(logits, e_bias):
    t = logits.shape[0]
    scores = jax.nn.sigmoid(logits)
    choice = scores + e_bias.astype(F32)
    grp = choice.reshape(t, N_GROUPS, N_EXPERTS // N_GROUPS)
    grp_score = lax.top_k(grp, 2)[0].sum(-1)
    _, gidx = lax.top_k(grp_score, TOPK_GROUPS)
    gmask = jnp.any(jnp.arange(N_GROUPS)[None, None, :] == gidx[..., None], axis=1)
    emask = jnp.repeat(gmask, N_EXPERTS // N_GROUPS, axis=1)
    _, eidx = lax.top_k(jnp.where(emask, choice, -jnp.inf), TOP_K)
    gw = jnp.take_along_axis(scores, eidx, axis=1)
    gw = gw / jnp.sum(gw, axis=-1, keepdims=True) * ROUTED_SCALE
    return eidx, gw


def _dispatch_plan(eidx, tm):
    t = eidx.shape[0]
    n_assign = t * TOP_K
    flat_e = eidx.reshape(-1).astype(jnp.int32)
    flat_tok = jnp.repeat(jnp.arange(t, dtype=jnp.int32), TOP_K)
    order = jnp.argsort(flat_e)
    e_s = flat_e[order]
    counts = jnp.bincount(flat_e, length=N_EXPERTS).astype(jnp.int32)
    padded = (counts + tm - 1) // tm * tm
    start = jnp.cumsum(counts) - counts
    ends = jnp.cumsum(padded)
    pstart = ends - padded
    dest = pstart[e_s] + (jnp.arange(n_assign, dtype=jnp.int32) - start[e_s])
    n_blocks = -(-n_assign // tm) + N_EXPERTS
    n_rows = n_blocks * tm
    row_tok = jnp.full((n_rows,), t, jnp.int32).at[dest].set(flat_tok[order])
    pos = jnp.zeros((n_assign,), jnp.int32).at[order].set(dest)
    block_exp = jnp.minimum(jnp.searchsorted(ends, jnp.arange(n_blocks, dtype=jnp.int32) * tm, side='right'),
                            N_EXPERTS - 1).astype(jnp.int32)
    n_used = (ends[-1] // tm).astype(jnp.int32).reshape(1)
    return row_tok, pos.reshape(t, TOP_K), block_exp, n_used


def _moe(x2, g, mod, w_router, e_bias, wg, wu, wd, swg, swu, swd, seq, tm, tme):
    t, d = x2.shape
    h, logits, shared = _moe_pre(x2, g, mod, w_router, swg.astype(BF16), swu.astype(BF16),
                                 swd.astype(BF16), seq, tm)
    eidx, gw = _route(logits, e_bias)
    row_tok, pos, block_exp, n_used = _dispatch_plan(eidx, tme)
    h_pad = jnp.concatenate([h, jnp.zeros((1, d), h.dtype)], axis=0)
    x_sorted = jnp.take(h_pad, row_tok, axis=0)
    y_sorted = _experts(x_sorted, block_exp, n_used, wg, wu, wd, tme)
    yg = jnp.take(y_sorted, pos.reshape(-1), axis=0).reshape(t, TOP_K, d)
    routed = jnp.sum(yg.astype(F32) * gw[:, :, None], axis=1)
    gate = jnp.repeat(mod[:, G_F, :], seq, axis=0)
    return x2 + gate * (routed + shared)


def _final_norm_body(x_ref, g_ref, o_ref):
    x = x_ref[...]
    ms = jnp.mean(x * x, axis=-1, keepdims=True)
    o_ref[...] = x * lax.rsqrt(ms + 1e-6) * g_ref[...]


def _final_norm(x2, g, tm):
    t, d = x2.shape
    return pl.pallas_call(
        _final_norm_body,
        grid=(t // tm,),
        in_specs=[pl.BlockSpec((tm, d), lambda i: (i, 0)), pl.BlockSpec((1, d), lambda i: (0, 0))],
        out_specs=pl.BlockSpec((tm, d), lambda i: (i, 0)),
        out_shape=jax.ShapeDtypeStruct((t, d), F32),
        compiler_params=_cp(("arbitrary",)),
        name="final_norm",
    )(x2, g.reshape(1, d))


def kernel(x, c, positions, mod_w, mod_b, norm_mix, norm_ffn, hyb_w_in, hyb_w_out, lru_conv_w, lru_conv_b, lru_wa, lru_ba, lru_wx, lru_bx, lru_lambda, cfm_w1, cfm_b1, cfm_dw, cfm_dwb, cfm_ln_g, cfm_ln_b, cfm_w2, cfm_b2, moe_router, moe_bias, moe_wg, moe_wu, moe_wd, sh_wg, sh_wu, sh_wd, final_norm):
    batch, seq, d = x.shape
    t = batch * seq
    n_sel = min(INDEX_TOPK, seq // 4)
    tm = min(512, seq)
    tq = min(256, seq)
    tme = MOE_ROWS

    mod_all = _modulation(c, mod_w, mod_b).reshape(mod_w.shape[0], batch, 6, d)
    tab_a = _rope_tables(positions, ROT_ATTN, HEAD_DIM)
    tab_i = _rope_tables(positions, ROT_IDX, IDX_DIM)
    x2 = x.reshape(t, d)
    zero_bias = jnp.zeros((d,), F32)

    mod = mod_all[0]
    w_in = hyb_w_in[0]
    w_main = w_in[:, :MAIN_COLS].astype(BF16)
    w_idx = jnp.pad(w_in[:, MAIN_COLS:], ((0, 0), (0, IDX_PAD - IDX_COLS))).astype(BF16)
    proj = _nm_matmul(x2, norm_mix[0], mod, w_main, seq, tm, 512, BF16, "hyb_in_main")
    idx = _nm_matmul(x2, norm_mix[0], mod, w_idx, seq, tm, IDX_PAD, F32, "hyb_in_idx")
    y_lru = _rg_lru(proj, lru_conv_w[0], lru_conv_b[0], lru_wa[0], lru_ba[0], lru_wx[0], lru_bx[0],
                    lru_lambda[0], batch, seq)
    y_att = _dsa(proj, idx, tab_a, tab_i, batch, seq, n_sel, tq)
    x2 = _proj_residual([y_lru, y_att], hyb_w_out[0].astype(BF16), zero_bias, x2, mod, G_M, seq, tm,
                        "hyb_out")
    x2 = _moe(x2, norm_ffn[0], mod, moe_router[0], moe_bias[0], moe_wg[0], moe_wu[0], moe_wd[0],
              sh_wg[0], sh_wu[0], sh_wd[0], seq, tm, tme)

    mod = mod_all[1]
    u = _nm_glu(x2, norm_mix[1], mod, cfm_w1[0].astype(BF16), cfm_b1[0], seq, tm, 512)
    z = _cfm_conv(u, cfm_dw[0], cfm_dwb[0], cfm_ln_g[0], cfm_ln_b[0], seq, tm)
    x2 = _proj_residual([z], cfm_w2[0].astype(BF16), cfm_b2[0], x2, mod, G_M, seq, tm, "cfm_out")
    x2 = _moe(x2, norm_ffn[1], mod, moe_router[1], moe_bias[1], moe_wg[1], moe_wu[1], moe_wd[1],
              sh_wg[1], sh_wu[1], sh_wd[1], seq, tm, tme)

    return _final_norm(x2, final_norm, tm).reshape(batch, seq, d)
```

```python
import functools

import jax
import jax.numpy as jnp
from jax import lax
from jax.experimental import pallas as pl
from jax.experimental.pallas import tpu as pltpu

F32 = jnp.float32
BF16 = jnp.bfloat16

LRU_WIDTH = 1024
LRU_BW = 128
LRU_CONV = 4
LRU_C = 8.0
N_HEADS = 8
N_KV_HEADS = 2
HEAD_DIM = 128
ATT_WIDTH = N_HEADS * HEAD_DIM
KV_WIDTH = N_KV_HEADS * HEAD_DIM
IDX_HEADS = 8
IDX_DIM = 64
INDEX_TOPK = 256
ROPE_THETA = 500000.0
ROT_ATTN = HEAD_DIM // 4
ROT_IDX = IDX_DIM // 4
CFM_CONV = 31
N_EXPERTS = 64
TOP_K = 8
N_GROUPS = 8
TOPK_GROUPS = 4
ROUTED_SCALE = 2.5

MAIN_COLS = 2 * LRU_WIDTH + ATT_WIDTH + 2 * KV_WIDTH
IDX_COLS = IDX_HEADS * IDX_DIM + IDX_DIM + IDX_HEADS
IDX_PAD = 640

MOE_ROWS = 512
LANES = 128
VMEM_LIMIT = 56 * 1024 * 1024
INT_MIN = -2147483648
NEG_BIG = -1e30

SH_M, SC_M, G_M, SH_F, SC_F, G_F = range(6)


def _cp(sem):
    return pltpu.CompilerParams(dimension_semantics=sem, vmem_limit_bytes=VMEM_LIMIT)


def _norm_mod(x, g, mod_ref, sh_row, sc_row):
    ms = jnp.mean(x * x, axis=-1, keepdims=True)
    y = x * lax.rsqrt(ms + 1e-6) * g
    return y * (1.0 + mod_ref[0, sc_row:sc_row + 1, :]) + mod_ref[0, sh_row:sh_row + 1, :]


def _mod_body(c_ref, w_ref, b_ref, o_ref):
    c = c_ref[...]
    cond = c * jax.nn.sigmoid(c)
    o_ref[0] = jnp.dot(cond, w_ref[0], preferred_element_type=F32,
                       precision=lax.Precision.HIGHEST) + b_ref[0]


def _modulation(c, mod_w, mod_b):
    depth, d, n = mod_w.shape
    b = c.shape[0]
    tn = 512
    return pl.pallas_call(
        _mod_body,
        grid=(depth, n // tn),
        in_specs=[pl.BlockSpec((b, d), lambda l, j: (0, 0)),
                  pl.BlockSpec((1, d, tn), lambda l, j: (l, 0, j)),
                  pl.BlockSpec((1, 1, tn), lambda l, j: (l, 0, j))],
        out_specs=pl.BlockSpec((1, b, tn), lambda l, j: (l, 0, j)),
        out_shape=jax.ShapeDtypeStruct((depth, b, n), F32),
        compiler_params=_cp(("arbitrary", "arbitrary")),
        name="modulation",
    )(c, mod_w, mod_b.reshape(depth, 1, n))


def _nm_matmul_body(x_ref, g_ref, mod_ref, w_ref, o_ref, h_scr):
    @pl.when(pl.program_id(1) == 0)
    def _():
        h_scr[...] = _norm_mod(x_ref[...], g_ref[...], mod_ref, SH_M, SC_M).astype(BF16)

    o_ref[...] = jnp.dot(h_scr[...], w_ref[...], preferred_element_type=F32).astype(o_ref.dtype)


def _nm_matmul(x2, g, mod, w, seq, tm, tn, out_dtype, name):
    t, d = x2.shape
    n = w.shape[1]
    return pl.pallas_call(
        _nm_matmul_body,
        grid=(t // tm, n // tn),
        in_specs=[pl.BlockSpec((tm, d), lambda i, j: (i, 0)),
                  pl.BlockSpec((1, d), lambda i, j: (0, 0)),
                  pl.BlockSpec((1, 6, d), lambda i, j: (i * tm // seq, 0, 0)),
                  pl.BlockSpec((d, tn), lambda i, j: (0, j))],
        out_specs=pl.BlockSpec((tm, tn), lambda i, j: (i, j)),
        out_shape=jax.ShapeDtypeStruct((t, n), out_dtype),
        scratch_shapes=[pltpu.VMEM((tm, d), BF16)],
        compiler_params=_cp(("arbitrary", "arbitrary")),
        name=name,
    )(x2, g.reshape(1, d), mod, w)


def _nm_glu_body(x_ref, g_ref, mod_ref, wa_ref, wg_ref, ba_ref, bg_ref, o_ref, h_scr):
    @pl.when(pl.program_id(1) == 0)
    def _():
        h_scr[...] = _norm_mod(x_ref[...], g_ref[...], mod_ref, SH_M, SC_M).astype(BF16)

    h = h_scr[...]
    a = jnp.dot(h, wa_ref[...], preferred_element_type=F32) + ba_ref[...]
    gt = jnp.dot(h, wg_ref[...], preferred_element_type=F32) + bg_ref[...]
    o_ref[...] = (a * jax.nn.sigmoid(gt)).astype(o_ref.dtype)


def _nm_glu(x2, g, mod, w1, b1, seq, tm, tn):
    t, d = x2.shape
    n = w1.shape[1] // 2
    nj = n // tn
    return pl.pallas_call(
        _nm_glu_body,
        grid=(t // tm, nj),
        in_specs=[pl.BlockSpec((tm, d), lambda i, j: (i, 0)),
                  pl.BlockSpec((1, d), lambda i, j: (0, 0)),
                  pl.BlockSpec((1, 6, d), lambda i, j: (i * tm // seq, 0, 0)),
                  pl.BlockSpec((d, tn), lambda i, j: (0, j)),
                  pl.BlockSpec((d, tn), lambda i, j: (0, j + nj)),
                  pl.BlockSpec((1, tn), lambda i, j: (0, j)),
                  pl.BlockSpec((1, tn), lambda i, j: (0, j + nj))],
        out_specs=pl.BlockSpec((tm, tn), lambda i, j: (i, j)),
        out_shape=jax.ShapeDtypeStruct((t, n), BF16),
        scratch_shapes=[pltpu.VMEM((tm, d), BF16)],
        compiler_params=_cp(("arbitrary", "arbitrary")),
        name="cfm_in_glu",
    )(x2, g.reshape(1, d), mod, w1, w1, b1.reshape(1, 2 * n), b1.reshape(1, 2 * n))


def _proj_res_body(*refs, n_parts, gate_row):
    part_refs = refs[:n_parts]
    w_ref, b_ref, x_ref, mod_ref, o_ref = refs[n_parts:]
    acc = None
    off = 0
    for p in part_refs:
        k = p.shape[1]
        y = jnp.dot(p[...], w_ref[off:off + k, :], preferred_element_type=F32)
        acc = y if acc is None else acc + y
        off += k
    y = acc + b_ref[...]
    o_ref[...] = x_ref[...] + mod_ref[0, gate_row:gate_row + 1, :] * y


def _proj_residual(parts, w, bias, x2, mod, gate_row, seq, tm, name):
    t, d = x2.shape
    n = w.shape[1]
    in_specs = [pl.BlockSpec((tm, p.shape[1]), lambda i: (i, 0)) for p in parts]
    in_specs += [pl.BlockSpec(w.shape, lambda i: (0, 0)),
                 pl.BlockSpec((1, n), lambda i: (0, 0)),
                 pl.BlockSpec((tm, d), lambda i: (i, 0)),
                 pl.BlockSpec((1, 6, d), lambda i: (i * tm // seq, 0, 0))]
    return pl.pallas_call(
        functools.partial(_proj_res_body, n_parts=len(parts), gate_row=gate_row),
        grid=(t // tm,),
        in_specs=in_specs,
        out_specs=pl.BlockSpec((tm, n), lambda i: (i, 0)),
        out_shape=jax.ShapeDtypeStruct((t, n), F32),
        compiler_params=_cp(("arbitrary",)),
        name=name,
    )(*parts, w, bias.reshape(1, n), x2, mod)


def _shift_rows(x, d, fill, row):
    return jnp.where(row >= d, pltpu.roll(x, d, 0), fill)


def _lru_body(lx_ref, lg_ref, cw_ref, cb_ref, wa_ref, ba_ref, wx_ref, bx_ref, lam_ref, o_ref):
    s, cw = lx_ref.shape
    x = lx_ref[...].astype(F32)
    row = lax.broadcasted_iota(jnp.int32, (s, cw), 0)
    xc = x * cw_ref[LRU_CONV - 1:LRU_CONV, :] + cb_ref[...]
    for d in range(1, LRU_CONV):
        xc = xc + _shift_rows(x, d, 0.0, row) * cw_ref[LRU_CONV - 1 - d:LRU_CONV - d, :]
    xb = xc.astype(BF16)
    ga = jax.nn.sigmoid(jnp.dot(xb, wa_ref[0].astype(BF16), preferred_element_type=F32) + ba_ref[...])
    gx = jax.nn.sigmoid(jnp.dot(xb, wx_ref[0].astype(BF16), preferred_element_type=F32) + bx_ref[...])
    z = -lam_ref[...]
    softplus = jnp.maximum(z, 0.0) + jnp.log(1.0 + jnp.exp(-jnp.abs(z)))
    log_a = (-LRU_C) * ga * softplus
    a = jnp.exp(log_a)
    mult = jnp.sqrt(1.0 - a * a)
    mult = jnp.where(row == 0, 1.0, mult)
    b = gx * xc * mult
    d = 1
    while d < s:
        a_sh = _shift_rows(a, d, 1.0, row)
        b_sh = _shift_rows(b, d, 0.0, row)
        b = a * b_sh + b
        a = a * a_sh
        d *= 2
    lg = lg_ref[...].astype(F32)
    gelu = 0.5 * lg * (1.0 + jnp.tanh(0.7978845608028654 * (lg + 0.044715 * lg * lg * lg)))
    o_ref[...] = (b * gelu).astype(o_ref.dtype)


def _rg_lru(proj, conv_w, conv_b, wa, ba, wx, bx, lam, batch, seq):
    cw = LRU_BW
    nc = LRU_WIDTH // cw
    vec = lambda v: v.reshape(1, LRU_WIDTH)
    vspec = pl.BlockSpec((1, cw), lambda b, c: (0, c))
    return pl.pallas_call(
        _lru_body,
        grid=(batch, nc),
        in_specs=[pl.BlockSpec((seq, cw), lambda b, c: (b, c)),
                  pl.BlockSpec((seq, cw), lambda b, c: (b, c + nc)),
                  pl.BlockSpec((LRU_CONV, cw), lambda b, c: (0, c)),
                  vspec,
                  pl.BlockSpec((1, cw, cw), lambda b, c: (c, 0, 0)),
                  vspec,
                  pl.BlockSpec((1, cw, cw), lambda b, c: (c, 0, 0)),
                  vspec, vspec],
        out_specs=pl.BlockSpec((seq, cw), lambda b, c: (b, c)),
        out_shape=jax.ShapeDtypeStruct((batch * seq, LRU_WIDTH), BF16),
        compiler_params=_cp(("arbitrary", "arbitrary")),
        name="rg_lru",
    )(proj, proj, conv_w, vec(conv_b), wa, vec(ba), wx, vec(bx), vec(lam))


def _rope(x, c, s1, s2, half):
    w = x.shape[1]
    return x * c + pltpu.roll(x, w - half, 1) * s1 + pltpu.roll(x, half, 1) * s2


def _sort_key(x):
    bits = pltpu.bitcast(x, jnp.int32)
    return bits ^ (jnp.right_shift(bits, 31) & 0x7FFFFFFF)


def _dsa_body(q_ref, k_ref, v_ref, iq_ref, ik_ref, ta_q_ref, ta_k_ref, ti_q_ref, ti_k_ref,
              o_ref, kr_scr, ikr_scr, key_scr, bias_scr, *, n_sel):
    qi = pl.program_id(1)
    tq = q_ref.shape[0]
    s = k_ref.shape[0]
    groups = N_KV_HEADS
    per_group = N_HEADS // N_KV_HEADS

    @pl.when(qi == 0)
    def _():
        ca, sa1, sa2 = ta_k_ref[0], ta_k_ref[1], ta_k_ref[2]
        for g in range(groups):
            kg = k_ref[:, g * HEAD_DIM:(g + 1) * HEAD_DIM].astype(F32)
            kr_scr[:, g * HEAD_DIM:(g + 1) * HEAD_DIM] = _rope(kg, ca, sa1, sa2, ROT_ATTN // 2).astype(BF16)
        ikp = _rope(ik_ref[...], ti_k_ref[0], ti_k_ref[1], ti_k_ref[2], ROT_IDX // 2)
        ikr_scr[...] = ikp[:, :IDX_DIM].astype(BF16)

    ci, si1, si2 = ti_q_ref[0], ti_q_ref[1], ti_q_ref[2]
    w_scale = (IDX_HEADS ** -0.5) * (IDX_DIM ** -0.5)
    iw = iq_ref[:, IDX_HEADS * IDX_DIM + IDX_DIM:IDX_HEADS * IDX_DIM + LANES] * w_scale
    ikr = ikr_scr[...]
    score = jnp.zeros((tq, s), F32)
    for hp in range(IDX_HEADS // 2):
        pair = _rope(iq_ref[:, hp * LANES:(hp + 1) * LANES], ci, si1, si2, ROT_IDX // 2).astype(BF16)
        for sub in range(2):
            h = 2 * hp + sub
            qh = pair[:, sub * IDX_DIM:(sub + 1) * IDX_DIM]
            dots = lax.dot_general(qh, ikr, (((1,), (1,)), ((), ())), preferred_element_type=F32)
            score = score + jnp.maximum(dots, 0.0) * iw[:, h:h + 1]

    t_row = qi * tq + lax.broadcasted_iota(jnp.int32, (tq, s), 0)
    col = lax.broadcasted_iota(jnp.int32, (tq, s), 1)
    causal = col <= t_row
    key_scr[...] = jnp.where(causal, _sort_key(score), INT_MIN)

    kf = float(n_sel)

    def count_ge(cand):
        return jnp.sum((key_scr[...] >= cand).astype(F32), axis=1, keepdims=True)

    tau0 = jnp.where(count_ge(jnp.zeros((tq, 1), jnp.int32)) >= kf, 0, INT_MIN).astype(jnp.int32)

    def bis(i, tau):
        cand = tau + jnp.left_shift(jnp.int32(1), 30 - i)
        return jnp.where(count_ge(cand) >= kf, cand, tau)

    tau = lax.fori_loop(0, 31, bis, tau0)

    keys = key_scr[...]
    n_gt = jnp.sum((keys > tau).astype(F32), axis=1, keepdims=True)
    n_ge = jnp.sum((keys >= tau).astype(F32), axis=1, keepdims=True)
    need = kf - n_gt
    tie = keys == tau
    excess = jnp.logical_and(n_ge > kf, tau > INT_MIN)
    any_excess = jnp.max(excess.astype(F32)) > 0.0

    bias_scr[...] = jnp.where(jnp.logical_and(keys >= tau, causal), 0.0, NEG_BIG)

    @pl.when(any_excess)
    def _():
        nbits = s.bit_length()

        def jb(i, jcur):
            cand = jcur + jnp.left_shift(jnp.int32(1), nbits - 1 - i)
            cnt = jnp.sum(jnp.logical_and(tie, col < cand).astype(F32), axis=1, keepdims=True)
            return jnp.where(cnt <= need, cand, jcur)

        jlim = lax.fori_loop(0, nbits, jb, jnp.zeros((tq, 1), jnp.int32))
        jlim = jnp.where(excess, jlim, s)
        sel = jnp.logical_or(keys > tau, jnp.logical_and(tie, col < jlim))
        bias_scr[...] = jnp.where(jnp.logical_and(sel, causal), 0.0, NEG_BIG)

    ca, sa1, sa2 = ta_q_ref[0], ta_q_ref[1], ta_q_ref[2]
    scale = HEAD_DIM ** -0.5
    for h in range(N_HEADS):
        g = h // per_group
        qh = _rope(q_ref[:, h * HEAD_DIM:(h + 1) * HEAD_DIM].astype(F32), ca, sa1, sa2, ROT_ATTN // 2)
        qh = qh.astype(BF16)
        kg = kr_scr[:, g * HEAD_DIM:(g + 1) * HEAD_DIM]
        logits = lax.dot_general(qh, kg, (((1,), (1,)), ((), ())), preferred_element_type=F32)
        logits = logits * scale + bias_scr[...]
        m = jnp.max(logits, axis=1, keepdims=True)
        p = jnp.exp(logits - m)
        den = jnp.sum(p, axis=1, keepdims=True)
        vg = v_ref[:, g * HEAD_DIM:(g + 1) * HEAD_DIM]
        o = jnp.dot(p.astype(BF16), vg, preferred_element_type=F32) / den
        o_ref[:, h * HEAD_DIM:(h + 1) * HEAD_DIM] = o.astype(o_ref.dtype)


def _dsa(proj, idx, tab_a, tab_i, batch, seq, n_sel, tq):
    nq = seq // tq
    q_col = 2 * LRU_WIDTH // ATT_WIDTH
    k_col = (2 * LRU_WIDTH + ATT_WIDTH) // KV_WIDTH
    ik_col = IDX_HEADS * IDX_DIM // LANES
    return pl.pallas_call(
        functools.partial(_dsa_body, n_sel=n_sel),
        grid=(batch, nq),
        in_specs=[pl.BlockSpec((tq, ATT_WIDTH), lambda b, i: (b * nq + i, q_col)),
                  pl.BlockSpec((seq, KV_WIDTH), lambda b, i: (b, k_col)),
                  pl.BlockSpec((seq, KV_WIDTH), lambda b, i: (b, k_col + 1)),
                  pl.BlockSpec((tq, IDX_PAD), lambda b, i: (b * nq + i, 0)),
                  pl.BlockSpec((seq, LANES), lambda b, i: (b, ik_col)),
                  pl.BlockSpec((3, tq, LANES), lambda b, i: (0, b * nq + i, 0)),
                  pl.BlockSpec((3, seq, LANES), lambda b, i: (0, b, 0)),
                  pl.BlockSpec((3, tq, LANES), lambda b, i: (0, b * nq + i, 0)),
                  pl.BlockSpec((3, seq, LANES), lambda b, i: (0, b, 0))],
        out_specs=pl.BlockSpec((tq, ATT_WIDTH), lambda b, i: (b * nq + i, 0)),
        out_shape=jax.ShapeDtypeStruct((batch * seq, ATT_WIDTH), BF16),
        scratch_shapes=[pltpu.VMEM((seq, KV_WIDTH), BF16),
                        pltpu.VMEM((seq, IDX_DIM), BF16),
                        pltpu.VMEM((tq, seq), jnp.int32),
                        pltpu.VMEM((tq, seq), F32)],
        compiler_params=_cp(("arbitrary", "arbitrary")),
        name="dsa_attention",
    )(proj, proj, proj, idx, idx, tab_a, tab_a, tab_i, tab_i)


def _rope_tables(positions, rot_dim, period):
    half = rot_dim // 2
    inv = jnp.power(jnp.float32(ROPE_THETA), -jnp.arange(0, rot_dim, 2, dtype=F32) / rot_dim)
    ang = positions.astype(F32).reshape(-1, 1) * inv
    cos, sin = jnp.cos(ang), jnp.sin(ang)
    t = cos.shape[0]
    ones = jnp.ones((t, period - rot_dim), F32)
    zeros = jnp.zeros((t, period - rot_dim), F32)
    zh = jnp.zeros((t, half), F32)
    c = jnp.concatenate([cos, cos, ones], axis=1)
    s1 = jnp.concatenate([-sin, zh, zeros], axis=1)
    s2 = jnp.concatenate([zh, sin, zeros], axis=1)
    reps = LANES // period
    return jnp.stack([jnp.tile(c, (1, reps)), jnp.tile(s1, (1, reps)), jnp.tile(s2, (1, reps))])


HALO = 32


def _cfm_conv_body(u_ref, halo_ref, dw_ref, dwb_ref, g_ref, b_ref, o_ref, ext_scr, acc_scr, *, seq, rc, cc):
    tm, c = u_ref.shape
    i = pl.program_id(0)
    at_start = (i * tm) % seq == 0
    halo = jnp.where(at_start, 0.0, halo_ref[...].astype(F32))
    ext_scr[0:HALO, :] = halo
    ext_scr[HALO:, :] = u_ref[...].astype(F32)
    base = HALO - (CFM_CONV - 1)

    def chunk(r, carry):
        r0 = pl.multiple_of(r * rc, rc)
        for c0 in range(0, c, cc):
            win = ext_scr[pl.ds(r0, rc + HALO), c0:c0 + cc]
            acc = jnp.zeros((rc, cc), F32) + dwb_ref[:, c0:c0 + cc]
            for sub in range(8):
                rolled = win if sub == 0 else pltpu.roll(win, rc + HALO - sub, 0)
                for j in range(CFM_CONV):
                    off = base + j
                    if off % 8 == sub:
                        a0 = off - sub
                        acc = acc + rolled[a0:a0 + rc, :] * dw_ref[j:j + 1, c0:c0 + cc]
            acc_scr[pl.ds(r0, rc), c0:c0 + cc] = acc
        return carry

    lax.fori_loop(0, tm // rc, chunk, 0)
    y = acc_scr[...]
    mu = jnp.mean(y, axis=-1, keepdims=True)
    yc = y - mu
    var = jnp.mean(yc * yc, axis=-1, keepdims=True)
    z = yc * lax.rsqrt(var + 1e-5) * g_ref[...] + b_ref[...]
    o_ref[...] = (z * jax.nn.sigmoid(z)).astype(o_ref.dtype)


def _cfm_conv(u, dw, dwb, ln_g, ln_b, seq, tm):
    t, c = u.shape
    hb = tm // HALO
    vec = lambda v: v.reshape(1, c)
    vspec = pl.BlockSpec((1, c), lambda i: (0, 0))
    return pl.pallas_call(
        functools.partial(_cfm_conv_body, seq=seq, rc=32, cc=256),
        grid=(t // tm,),
        in_specs=[pl.BlockSpec((tm, c), lambda i: (i, 0)),
                  pl.BlockSpec((HALO, c), lambda i: (jnp.maximum(i * hb - 1, 0), 0)),
                  pl.BlockSpec((CFM_CONV, c), lambda i: (0, 0)),
                  vspec, vspec, vspec],
        out_specs=pl.BlockSpec((tm, c), lambda i: (i, 0)),
        out_shape=jax.ShapeDtypeStruct((t, c), BF16),
        scratch_shapes=[pltpu.VMEM((tm + HALO, c), F32), pltpu.VMEM((tm, c), F32)],
        compiler_params=_cp(("arbitrary",)),
        name="cfm_conv_ln",
    )(u, u, dw, vec(dwb), vec(ln_g), vec(ln_b))


def _moe_pre_body(x_ref, g_ref, mod_ref, wr_ref, swg_ref, swu_ref, swd_ref, h_ref, lg_ref, sh_ref):
    h = _norm_mod(x_ref[...], g_ref[...], mod_ref, SH_F, SC_F)
    hb = h.astype(BF16)
    h_ref[...] = hb
    lg_ref[...] = lax.dot_general(wr_ref[...], h, (((1,), (1,)), ((), ())), preferred_element_type=F32,
                                  precision=lax.Precision.HIGHEST)
    gt = jnp.dot(hb, swg_ref[...], preferred_element_type=F32)
    up = jnp.dot(hb, swu_ref[...], preferred_element_type=F32)
    mid = (gt * jax.nn.sigmoid(gt) * up).astype(BF16)
    sh_ref[...] = jnp.dot(mid, swd_ref[...], preferred_element_type=F32)


def _moe_pre(x2, g, mod, w_router_t, swg, swu, swd, seq, tm):
    t, d = x2.shape
    e = w_router_t.shape[0]
    full = lambda a: pl.BlockSpec(a.shape, lambda i: (0, 0))
    return pl.pallas_call(
        _moe_pre_body,
        grid=(t // tm,),
        in_specs=[pl.BlockSpec((tm, d), lambda i: (i, 0)),
                  pl.BlockSpec((1, d), lambda i: (0, 0)),
                  pl.BlockSpec((1, 6, d), lambda i: (i * tm // seq, 0, 0)),
                  full(w_router_t), full(swg), full(swu), full(swd)],
        out_specs=[pl.BlockSpec((tm, d), lambda i: (i, 0)),
                   pl.BlockSpec((e, tm), lambda i: (0, i)),
                   pl.BlockSpec((tm, d), lambda i: (i, 0))],
        out_shape=[jax.ShapeDtypeStruct((t, d), BF16),
                   jax.ShapeDtypeStruct((e, t), F32),
                   jax.ShapeDtypeStruct((t, d), F32)],
        compiler_params=_cp(("arbitrary",)),
        name="moe_pre",
    )(x2, g.reshape(1, d), mod, w_router_t, swg, swu, swd)


def _first_max(v, ids, sentinel):
    m = jnp.max(v, axis=0, keepdims=True)
    first = jnp.min(jnp.where(v == m, ids, sentinel), axis=0, keepdims=True)
    return m, first


def _route_body(lg_ref, bias_ref, eidx_ref, rank_ref, gw_ref, cnt_ref):
    e, tm = lg_ref.shape
    gsz = e // N_GROUPS
    neg = -jnp.inf

    @pl.when(pl.program_id(0) == 0)
    def _():
        cnt_ref[...] = jnp.zeros_like(cnt_ref)

    scores = jax.nn.sigmoid(lg_ref[...])
    choice = scores + bias_ref[:, 0:1]
    row = lax.broadcasted_iota(jnp.int32, (e, tm), 0)
    sub = lax.broadcasted_iota(jnp.int32, (gsz, tm), 0)

    parts = []
    for g in range(N_GROUPS):
        vg = choice[g * gsz:(g + 1) * gsz, :]
        m1, f1 = _first_max(vg, sub, gsz)
        m2 = jnp.max(jnp.where(sub == f1, neg, vg), axis=0, keepdims=True)
        parts.append(jnp.broadcast_to(m1 + m2, (gsz, tm)))
    gscore = jnp.concatenate(parts, axis=0)

    gid = jnp.right_shift(row, gsz.bit_length() - 1)
    v = jnp.full((e, tm), neg, F32)
    for _ in range(TOPK_GROUPS):
        _, fg = _first_max(gscore, gid, N_GROUPS)
        hit = gid == fg
        v = jnp.where(hit, choice, v)
        gscore = jnp.where(hit, neg, gscore)

    picks = []
    self32 = jnp.zeros((e, tm), F32)
    for _ in range(TOP_K):
        _, fe = _first_max(v, row, e)
        hit = row == fe
        picks.append(fe)
        self32 = jnp.where(hit, 1.0, self32)
        v = jnp.where(hit, neg, v)

    picked = self32 * scores
    gwd = picked / jnp.sum(picked, axis=0, keepdims=True) * ROUTED_SCALE

    before = (lax.broadcasted_iota(jnp.int32, (tm, tm), 0)
              < lax.broadcasted_iota(jnp.int32, (tm, tm), 1)).astype(BF16)
    rank = jnp.dot(self32.astype(BF16), before, preferred_element_type=F32) + cnt_ref[:, 0:1]
    cnt_ref[...] = cnt_ref[...] + jnp.sum(self32, axis=1, keepdims=True)

    for k in range(TOP_K):
        hit = row == picks[k]
        eidx_ref[k:k + 1, :] = picks[k]
        rank_ref[k:k + 1, :] = jnp.sum(jnp.where(hit, rank, 0.0), axis=0, keepdims=True).astype(jnp.int32)
        gw_ref[k:k + 1, :] = jnp.sum(jnp.where(hit, gwd, 0.0), axis=0, keepdims=True)


def _route(logits_t, e_bias, tm):
    e, t = logits_t.shape
    kspec = pl.BlockSpec((TOP_K, tm), lambda i: (0, i))
    return pl.pallas_call(
        _route_body,
        grid=(t // tm,),
        in_specs=[pl.BlockSpec((e, tm), lambda i: (0, i)),
                  pl.BlockSpec((e, LANES), lambda i: (0, 0))],
        out_specs=[kspec, kspec, kspec, pl.BlockSpec((e, LANES), lambda i: (0, 0))],
        out_shape=[jax.ShapeDtypeStruct((TOP_K, t), jnp.int32),
                   jax.ShapeDtypeStruct((TOP_K, t), jnp.int32),
                   jax.ShapeDtypeStruct((TOP_K, t), F32),
                   jax.ShapeDtypeStruct((e, LANES), F32)],
        compiler_params=_cp(("arbitrary",)),
        name="moe_route",
    )(logits_t, jnp.broadcast_to(e_bias.astype(F32).reshape(e, 1), (e, LANES)))


def _experts_body(be_ref, nu_ref, x_ref, wg_ref, wu_ref, wd_ref, o_ref, wg_scr, wu_scr, wd_scr):
    i = pl.program_id(0)

    @pl.when(i < nu_ref[0])
    def _():
        prev = be_ref[jnp.maximum(i - 1, 0)]
        changed = jnp.logical_or(i == 0, be_ref[i] != prev)

        @pl.when(changed)
        def _():
            wg_scr[...] = wg_ref[0].astype(BF16)
            wu_scr[...] = wu_ref[0].astype(BF16)
            wd_scr[...] = wd_ref[0].astype(BF16)

        x = x_ref[...]
        gt = jnp.dot(x, wg_scr[...], preferred_element_type=F32)
        up = jnp.dot(x, wu_scr[...], preferred_element_type=F32)
        mid = (gt * jax.nn.sigmoid(gt) * up).astype(BF16)
        o_ref[...] = jnp.dot(mid, wd_scr[...], preferred_element_type=F32).astype(o_ref.dtype)

    @pl.when(i >= nu_ref[0])
    def _():
        o_ref[...] = jnp.zeros_like(o_ref)


def _experts(x_sorted, block_exp, n_used, wg, wu, wd, tm):
    n_rows, d = x_sorted.shape
    ff = wg.shape[2]
    n_blocks = n_rows // tm
    grid_spec = pltpu.PrefetchScalarGridSpec(
        num_scalar_prefetch=2,
        grid=(n_blocks,),
        in_specs=[pl.BlockSpec((tm, d), lambda i, be, nu: (i, 0)),
                  pl.BlockSpec((1, d, ff), lambda i, be, nu: (be[i], 0, 0)),
                  pl.BlockSpec((1, d, ff), lambda i, be, nu: (be[i], 0, 0)),
                  pl.BlockSpec((1, ff, d), lambda i, be, nu: (be[i], 0, 0))],
        out_specs=pl.BlockSpec((tm, d), lambda i, be, nu: (i, 0)),
        scratch_shapes=[pltpu.VMEM((d, ff), BF16), pltpu.VMEM((d, ff), BF16), pltpu.VMEM((ff, d), BF16)],
    )
    return pl.pallas_call(
        _experts_body,
        grid_spec=grid_spec,
        out_shape=jax.ShapeDtypeStruct((n_rows, d), BF16),
        compiler_params=_cp(("arbitrary",)),
        name="moe_experts",
    )(block_exp, n_used, x_sorted, wg, wu, wd)


def _dispatch_plan(eidx_t, rank_t, counts, tm):
    n_assign = eidx_t.size
    padded = (counts + tm - 1) // tm * tm
    ends = jnp.cumsum(padded)
    pstart = ends - padded
    pos_t = jnp.take(pstart, eidx_t) + rank_t
    n_blocks = -(-n_assign // tm) + N_EXPERTS
    block_exp = jnp.minimum(jnp.searchsorted(ends, jnp.arange(n_blocks, dtype=jnp.int32) * tm, side='right'),
                            N_EXPERTS - 1).astype(jnp.int32)
    n_used = (ends[-1] // tm).astype(jnp.int32).reshape(1)
    return pos_t, block_exp, n_used, n_blocks


def _moe(x2, g, mod, w_router, e_bias, wg, wu, wd, swg, swu, swd, seq, tm, tme):
    t, d = x2.shape
    h, logits_t, shared = _moe_pre(x2, g, mod, w_router.T, swg.astype(BF16), swu.astype(BF16),
                                   swd.astype(BF16), seq, tm)
    eidx_t, rank_t, gw_t, cnt = _route(logits_t, e_bias, tm)
    counts = cnt[:, 0].astype(jnp.int32)
    pos_t, block_exp, n_used, n_blocks = _dispatch_plan(eidx_t, rank_t, counts, tme)
    pos = pos_t.T
    tok = jnp.broadcast_to(jnp.arange(t, dtype=jnp.int32)[:, None], (t, TOP_K))
    row_tok = jnp.zeros((n_blocks * tme,), jnp.int32).at[pos.reshape(-1)].set(
        tok.reshape(-1), unique_indices=True)
    x_sorted = jnp.take(h, row_tok, axis=0)
    y_sorted = _experts(x_sorted, block_exp, n_used, wg, wu, wd, tme)
    yg = jnp.take(y_sorted, pos.reshape(-1), axis=0).reshape(t, TOP_K, d)
    routed = jnp.sum(yg.astype(F32) * gw_t.T[:, :, None], axis=1)
    gate = jnp.repeat(mod[:, G_F, :], seq, axis=0)
    return x2 + gate * (routed + shared)


def _final_norm_body(x_ref, g_ref, o_ref):
    x = x_ref[...]
    ms = jnp.mean(x * x, axis=-1, keepdims=True)
    o_ref[...] = x * lax.rsqrt(ms + 1e-6) * g_ref[...]


def _final_norm(x2, g, tm):
    t, d = x2.shape
    return pl.pallas_call(
        _final_norm_body,
        grid=(t // tm,),
        in_specs=[pl.BlockSpec((tm, d), lambda i: (i, 0)), pl.BlockSpec((1, d), lambda i: (0, 0))],
        out_specs=pl.BlockSpec((tm, d), lambda i: (i, 0)),
        out_shape=jax.ShapeDtypeStruct((t, d), F32),
        compiler_params=_cp(("arbitrary",)),
        name="final_norm",
    )(x2, g.reshape(1, d))


def kernel(x, c, positions, mod_w, mod_b, norm_mix, norm_ffn, hyb_w_in, hyb_w_out, lru_conv_w, lru_conv_b, lru_wa, lru_ba, lru_wx, lru_bx, lru_lambda, cfm_w1, cfm_b1, cfm_dw, cfm_dwb, cfm_ln_g, cfm_ln_b, cfm_w2, cfm_b2, moe_router, moe_bias, moe_wg, moe_wu, moe_wd, sh_wg, sh_wu, sh_wd, final_norm):
    batch, seq, d = x.shape
    t = batch * seq
    n_sel = min(INDEX_TOPK, seq // 4)
    tm = min(512, seq)
    tq = min(256, seq)
    tme = MOE_ROWS

    mod_all = _modulation(c, mod_w, mod_b).reshape(mod_w.shape[0], batch, 6, d)
    tab_a = _rope_tables(positions, ROT_ATTN, HEAD_DIM)
    tab_i = _rope_tables(positions, ROT_IDX, IDX_DIM)
    x2 = x.reshape(t, d)
    zero_bias = jnp.zeros((d,), F32)

    mod = mod_all[0]
    w_in = hyb_w_in[0]
    w_main = w_in[:, :MAIN_COLS].astype(BF16)
    w_idx = jnp.pad(w_in[:, MAIN_COLS:], ((0, 0), (0, IDX_PAD - IDX_COLS))).astype(BF16)
    proj = _nm_matmul(x2, norm_mix[0], mod, w_main, seq, tm, 512, BF16, "hyb_in_main")
    idx = _nm_matmul(x2, norm_mix[0], mod, w_idx, seq, tm, IDX_PAD, F32, "hyb_in_idx")
    y_lru = _rg_lru(proj, lru_conv_w[0], lru_conv_b[0], lru_wa[0], lru_ba[0], lru_wx[0], lru_bx[0],
                    lru_lambda[0], batch, seq)
    y_att = _dsa(proj, idx, tab_a, tab_i, batch, seq, n_sel, tq)
    x2 = _proj_residual([y_lru, y_att], hyb_w_out[0].astype(BF16), zero_bias, x2, mod, G_M, seq, tm,
                        "hyb_out")
    x2 = _moe(x2, norm_ffn[0], mod, moe_router[0], moe_bias[0], moe_wg[0], moe_wu[0], moe_wd[0],
              sh_wg[0], sh_wu[0], sh_wd[0], seq, tm, tme)

    mod = mod_all[1]
    u = _nm_glu(x2, norm_mix[1], mod, cfm_w1[0].astype(BF16), cfm_b1[0], seq, tm, 512)
    z = _cfm_conv(u, cfm_dw[0], cfm_dwb[0], cfm_ln_g[0], cfm_ln_b[0], seq, tm)
    x2 = _proj_residual([z], cfm_w2[0].astype(BF16), cfm_b2[0], x2, mod, G_M, seq, tm, "cfm_out")
    x2 = _moe(x2, norm_ffn[1], mod, moe_router[1], moe_bias[1], moe_wg[1], moe_wu[1], moe_wd[1],
              sh_wg[1], sh_wu[1], sh_wd[1], seq, tm, tme)

    return _final_norm(x2, final_norm, tm).reshape(batch, seq, d)
```

```python
import functools

import jax
import jax.numpy as jnp
from jax import lax
from jax.experimental import pallas as pl
from jax.experimental.pallas import tpu as pltpu

F32 = jnp.float32
BF16 = jnp.bfloat16

LRU_WIDTH = 1024
LRU_BW = 128
LRU_CONV = 4
LRU_C = 8.0
N_HEADS = 8
N_KV_HEADS = 2
HEAD_DIM = 128
ATT_WIDTH = N_HEADS * HEAD_DIM
KV_WIDTH = N_KV_HEADS * HEAD_DIM
IDX_HEADS = 8
IDX_DIM = 64
INDEX_TOPK = 256
ROPE_THETA = 500000.0
ROT_ATTN = HEAD_DIM // 4
ROT_IDX = IDX_DIM // 4
CFM_CONV = 31
N_EXPERTS = 64
TOP_K = 8
N_GROUPS = 8
TOPK_GROUPS = 4
ROUTED_SCALE = 2.5

MAIN_COLS = 2 * LRU_WIDTH + ATT_WIDTH + 2 * KV_WIDTH
IDX_COLS = IDX_HEADS * IDX_DIM + IDX_DIM + IDX_HEADS
IDX_PAD = 640

MOE_ROWS = 512
LANES = 128
VMEM_LIMIT = 56 * 1024 * 1024
INT_MIN = -2147483648
NEG_BIG = -1e30

SH_M, SC_M, G_M, SH_F, SC_F, G_F = range(6)


def _cp(sem):
    return pltpu.CompilerParams(dimension_semantics=sem, vmem_limit_bytes=VMEM_LIMIT)


def _weight_spec(block, index_map, n_col_blocks):
    if n_col_blocks == 1:
        return pl.BlockSpec(block, index_map, pipeline_mode=pl.Buffered(1))
    return pl.BlockSpec(block, index_map)


def _norm_mod(x, g, mod_ref, sh_row, sc_row):
    ms = jnp.mean(x * x, axis=-1, keepdims=True)
    y = x * lax.rsqrt(ms + 1e-6) * g
    return y * (1.0 + mod_ref[0, sc_row:sc_row + 1, :]) + mod_ref[0, sh_row:sh_row + 1, :]


def _mod_body(c_ref, w_ref, b_ref, o_ref):
    c = c_ref[...]
    cond = c * jax.nn.sigmoid(c)
    o_ref[0] = jnp.dot(cond, w_ref[0], preferred_element_type=F32,
                       precision=lax.Precision.HIGHEST) + b_ref[0]


def _modulation(c, mod_w, mod_b):
    depth, d, n = mod_w.shape
    b = c.shape[0]
    tn = 512
    return pl.pallas_call(
        _mod_body,
        grid=(depth, n // tn),
        in_specs=[pl.BlockSpec((b, d), lambda l, j: (0, 0)),
                  pl.BlockSpec((1, d, tn), lambda l, j: (l, 0, j)),
                  pl.BlockSpec((1, 1, tn), lambda l, j: (l, 0, j))],
        out_specs=pl.BlockSpec((1, b, tn), lambda l, j: (l, 0, j)),
        out_shape=jax.ShapeDtypeStruct((depth, b, n), F32),
        compiler_params=_cp(("arbitrary", "arbitrary")),
        name="modulation",
    )(c, mod_w, mod_b.reshape(depth, 1, n))


def _nm_matmul_body(x_ref, g_ref, mod_ref, w_ref, o_ref, h_scr):
    @pl.when(pl.program_id(1) == 0)
    def _():
        h_scr[...] = _norm_mod(x_ref[...], g_ref[...], mod_ref, SH_M, SC_M).astype(BF16)

    o_ref[...] = jnp.dot(h_scr[...], w_ref[...], preferred_element_type=F32).astype(o_ref.dtype)


def _nm_matmul(x2, g, mod, w, seq, tm, tn, out_dtype, name):
    t, d = x2.shape
    n = w.shape[1]
    return pl.pallas_call(
        _nm_matmul_body,
        grid=(t // tm, n // tn),
        in_specs=[pl.BlockSpec((tm, d), lambda i, j: (i, 0)),
                  pl.BlockSpec((1, d), lambda i, j: (0, 0)),
                  pl.BlockSpec((1, 6, d), lambda i, j: (i * tm // seq, 0, 0)),
                  _weight_spec((d, tn), lambda i, j: (0, j), n // tn)],
        out_specs=pl.BlockSpec((tm, tn), lambda i, j: (i, j)),
        out_shape=jax.ShapeDtypeStruct((t, n), out_dtype),
        scratch_shapes=[pltpu.VMEM((tm, d), BF16)],
        compiler_params=_cp(("arbitrary", "arbitrary")),
        name=name,
    )(x2, g.reshape(1, d), mod, w)


def _nm_glu_body(x_ref, g_ref, mod_ref, wa_ref, wg_ref, ba_ref, bg_ref, o_ref, h_scr):
    @pl.when(pl.program_id(1) == 0)
    def _():
        h_scr[...] = _norm_mod(x_ref[...], g_ref[...], mod_ref, SH_M, SC_M).astype(BF16)

    h = h_scr[...]
    a = jnp.dot(h, wa_ref[...], preferred_element_type=F32) + ba_ref[...]
    gt = jnp.dot(h, wg_ref[...], preferred_element_type=F32) + bg_ref[...]
    o_ref[...] = (a * jax.nn.sigmoid(gt)).astype(o_ref.dtype)


def _nm_glu(x2, g, mod, w1, b1, seq, tm, tn):
    t, d = x2.shape
    n = w1.shape[1] // 2
    nj = n // tn
    return pl.pallas_call(
        _nm_glu_body,
        grid=(t // tm, nj),
        in_specs=[pl.BlockSpec((tm, d), lambda i, j: (i, 0)),
                  pl.BlockSpec((1, d), lambda i, j: (0, 0)),
                  pl.BlockSpec((1, 6, d), lambda i, j: (i * tm // seq, 0, 0)),
                  _weight_spec((d, tn), lambda i, j: (0, j), nj),
                  _weight_spec((d, tn), lambda i, j: (0, j + nj), nj),
                  pl.BlockSpec((1, tn), lambda i, j: (0, j)),
                  pl.BlockSpec((1, tn), lambda i, j: (0, j + nj))],
        out_specs=pl.BlockSpec((tm, tn), lambda i, j: (i, j)),
        out_shape=jax.ShapeDtypeStruct((t, n), BF16),
        scratch_shapes=[pltpu.VMEM((tm, d), BF16)],
        compiler_params=_cp(("arbitrary", "arbitrary")),
        name="cfm_in_glu",
    )(x2, g.reshape(1, d), mod, w1, w1, b1.reshape(1, 2 * n), b1.reshape(1, 2 * n))


def _proj_res_body(*refs, n_parts, gate_row):
    part_refs = refs[:n_parts]
    w_ref, b_ref, x_ref, mod_ref, o_ref = refs[n_parts:]
    acc = None
    off = 0
    for p in part_refs:
        k = p.shape[1]
        y = jnp.dot(p[...], w_ref[off:off + k, :], preferred_element_type=F32)
        acc = y if acc is None else acc + y
        off += k
    y = acc + b_ref[...]
    o_ref[...] = x_ref[...] + mod_ref[0, gate_row:gate_row + 1, :] * y


def _proj_residual(parts, w, bias, x2, mod, gate_row, seq, tm, name):
    t, d = x2.shape
    n = w.shape[1]
    in_specs = [pl.BlockSpec((tm, p.shape[1]), lambda i: (i, 0)) for p in parts]
    in_specs += [pl.BlockSpec(w.shape, lambda i: (0, 0)),
                 pl.BlockSpec((1, n), lambda i: (0, 0)),
                 pl.BlockSpec((tm, d), lambda i: (i, 0)),
                 pl.BlockSpec((1, 6, d), lambda i: (i * tm // seq, 0, 0))]
    return pl.pallas_call(
        functools.partial(_proj_res_body, n_parts=len(parts), gate_row=gate_row),
        grid=(t // tm,),
        in_specs=in_specs,
        out_specs=pl.BlockSpec((tm, n), lambda i: (i, 0)),
        out_shape=jax.ShapeDtypeStruct((t, n), F32),
        compiler_params=_cp(("arbitrary",)),
        name=name,
    )(*parts, w, bias.reshape(1, n), x2, mod)


def _shift_rows(x, d, fill, row):
    return jnp.where(row >= d, pltpu.roll(x, d, 0), fill)


def _lru_body(lx_ref, lg_ref, cw_ref, cb_ref, wa_ref, ba_ref, wx_ref, bx_ref, lam_ref, o_ref):
    s, cw = lx_ref.shape
    x = lx_ref[...].astype(F32)
    row = lax.broadcasted_iota(jnp.int32, (s, cw), 0)
    xc = x * cw_ref[LRU_CONV - 1:LRU_CONV, :] + cb_ref[...]
    for d in range(1, LRU_CONV):
        xc = xc + _shift_rows(x, d, 0.0, row) * cw_ref[LRU_CONV - 1 - d:LRU_CONV - d, :]
    xb = xc.astype(BF16)
    ga = jax.nn.sigmoid(jnp.dot(xb, wa_ref[0].astype(BF16), preferred_element_type=F32) + ba_ref[...])
    gx = jax.nn.sigmoid(jnp.dot(xb, wx_ref[0].astype(BF16), preferred_element_type=F32) + bx_ref[...])
    z = -lam_ref[...]
    softplus = jnp.maximum(z, 0.0) + jnp.log(1.0 + jnp.exp(-jnp.abs(z)))
    log_a = (-LRU_C) * ga * softplus
    a = jnp.exp(log_a)
    mult = jnp.sqrt(1.0 - a * a)
    mult = jnp.where(row == 0, 1.0, mult)
    b = gx * xc * mult
    d = 1
    while d < s:
        a_sh = _shift_rows(a, d, 1.0, row)
        b_sh = _shift_rows(b, d, 0.0, row)
        b = a * b_sh + b
        a = a * a_sh
        d *= 2
    lg = lg_ref[...].astype(F32)
    gelu = 0.5 * lg * (1.0 + jnp.tanh(0.7978845608028654 * (lg + 0.044715 * lg * lg * lg)))
    o_ref[...] = (b * gelu).astype(o_ref.dtype)


def _rg_lru(proj, conv_w, conv_b, wa, ba, wx, bx, lam, batch, seq):
    cw = LRU_BW
    nc = LRU_WIDTH // cw
    vec = lambda v: v.reshape(1, LRU_WIDTH)
    vspec = pl.BlockSpec((1, cw), lambda b, c: (0, c))
    return pl.pallas_call(
        _lru_body,
        grid=(batch, nc),
        in_specs=[pl.BlockSpec((seq, cw), lambda b, c: (b, c)),
                  pl.BlockSpec((seq, cw), lambda b, c: (b, c + nc)),
                  pl.BlockSpec((LRU_CONV, cw), lambda b, c: (0, c)),
                  vspec,
                  pl.BlockSpec((1, cw, cw), lambda b, c: (c, 0, 0)),
                  vspec,
                  pl.BlockSpec((1, cw, cw), lambda b, c: (c, 0, 0)),
                  vspec, vspec],
        out_specs=pl.BlockSpec((seq, cw), lambda b, c: (b, c)),
        out_shape=jax.ShapeDtypeStruct((batch * seq, LRU_WIDTH), BF16),
        compiler_params=_cp(("arbitrary", "arbitrary")),
        name="rg_lru",
    )(proj, proj, conv_w, vec(conv_b), wa, vec(ba), wx, vec(bx), vec(lam))


def _rope(x, c, s1, s2, half):
    w = x.shape[1]
    return x * c + pltpu.roll(x, w - half, 1) * s1 + pltpu.roll(x, half, 1) * s2


def _sort_key(x):
    bits = pltpu.bitcast(x, jnp.int32)
    return bits ^ (jnp.right_shift(bits, 31) & 0x7FFFFFFF)


def _dsa_tile(sk, q_ref, iq_ref, ta_q_ref, ti_q_ref, o_ref, kr_scr, va_scr, ikr_scr, key_scr, bias_scr,
              qs_scr, n_sel):
    qi = pl.program_id(1)
    tq = q_ref.shape[0]
    groups = N_KV_HEADS
    per_group = N_HEADS // N_KV_HEADS

    ci, si1, si2 = ti_q_ref[0], ti_q_ref[1], ti_q_ref[2]
    w_scale = (IDX_HEADS ** -0.5) * (IDX_DIM ** -0.5)
    iw = iq_ref[:, IDX_HEADS * IDX_DIM + IDX_DIM:IDX_HEADS * IDX_DIM + LANES] * w_scale
    ikr = ikr_scr[:sk, :]
    score = jnp.zeros((tq, sk), F32)
    for hp in range(IDX_HEADS // 2):
        pair = _rope(iq_ref[:, hp * LANES:(hp + 1) * LANES], ci, si1, si2, ROT_IDX // 2).astype(BF16)
        for sub in range(2):
            h = 2 * hp + sub
            qh = pair[:, sub * IDX_DIM:(sub + 1) * IDX_DIM]
            dots = lax.dot_general(qh, ikr, (((1,), (1,)), ((), ())), preferred_element_type=F32)
            score = score + jnp.maximum(dots, 0.0) * iw[:, h:h + 1]

    t_row = qi * tq + lax.broadcasted_iota(jnp.int32, (tq, sk), 0)
    col = lax.broadcasted_iota(jnp.int32, (tq, sk), 1)
    causal = col <= t_row
    key_scr[:, :sk] = jnp.where(causal, _sort_key(score), INT_MIN)

    kf = float(n_sel)

    def count_ge(cand):
        return jnp.sum((key_scr[:, :sk] >= cand).astype(F32), axis=1, keepdims=True)

    tau0 = jnp.where(count_ge(jnp.zeros((tq, 1), jnp.int32)) >= kf, 0, INT_MIN).astype(jnp.int32)

    def bis(i, tau):
        cand = tau + jnp.left_shift(jnp.int32(1), 30 - i)
        return jnp.where(count_ge(cand) >= kf, cand, tau)

    tau = lax.fori_loop(0, 31, bis, tau0)

    keys = key_scr[:, :sk]
    n_gt = jnp.sum((keys > tau).astype(F32), axis=1, keepdims=True)
    n_ge = jnp.sum((keys >= tau).astype(F32), axis=1, keepdims=True)
    need = kf - n_gt
    tie = keys == tau
    excess = jnp.logical_and(n_ge > kf, tau > INT_MIN)
    any_excess = jnp.max(excess.astype(F32)) > 0.0

    bias_scr[:, :sk] = jnp.where(jnp.logical_and(keys >= tau, causal), 0.0, NEG_BIG)

    @pl.when(any_excess)
    def _():
        nbits = sk.bit_length()

        def jb(i, jcur):
            cand = jcur + jnp.left_shift(jnp.int32(1), nbits - 1 - i)
            cnt = jnp.sum(jnp.logical_and(tie, col < cand).astype(F32), axis=1, keepdims=True)
            return jnp.where(cnt <= need, cand, jcur)

        jlim = lax.fori_loop(0, nbits, jb, jnp.zeros((tq, 1), jnp.int32))
        jlim = jnp.where(excess, jlim, sk)
        sel = jnp.logical_or(keys > tau, jnp.logical_and(tie, col < jlim))
        bias_scr[:, :sk] = jnp.where(jnp.logical_and(sel, causal), 0.0, NEG_BIG)

    ca, sa1, sa2 = ta_q_ref[0], ta_q_ref[1], ta_q_ref[2]
    scale = HEAD_DIM ** -0.5
    for g in range(groups):
        for r in range(per_group):
            h = g * per_group + r
            qh = _rope(q_ref[:, h * HEAD_DIM:(h + 1) * HEAD_DIM].astype(F32), ca, sa1, sa2, ROT_ATTN // 2)
            qs_scr[r * tq:(r + 1) * tq, :] = (qh * scale).astype(BF16)
        kg = kr_scr[:sk, g * HEAD_DIM:(g + 1) * HEAD_DIM]
        logits = lax.dot_general(qs_scr[...], kg, (((1,), (1,)), ((), ())), preferred_element_type=F32)
        ps = []
        for r in range(per_group):
            lg = logits[r * tq:(r + 1) * tq, :] + bias_scr[:, :sk]
            m = jnp.max(lg, axis=1, keepdims=True)
            ps.append(jnp.exp(lg - m).astype(BF16))
        pv = jnp.dot(jnp.concatenate(ps, axis=0), va_scr[:sk, g * 2 * HEAD_DIM:(g + 1) * 2 * HEAD_DIM],
                     preferred_element_type=F32)
        for r in range(per_group):
            h = g * per_group + r
            blk = pv[r * tq:(r + 1) * tq, :]
            o = blk[:, :HEAD_DIM] / blk[:, HEAD_DIM:HEAD_DIM + 1]
            o_ref[:, h * HEAD_DIM:(h + 1) * HEAD_DIM] = o.astype(o_ref.dtype)


def _dsa_body(q_ref, k_ref, v_ref, iq_ref, ik_ref, ta_q_ref, ta_k_ref, ti_q_ref, ti_k_ref,
              o_ref, kr_scr, va_scr, ikr_scr, key_scr, bias_scr, qs_scr, *, n_sel):
    qi = pl.program_id(1)
    tq = q_ref.shape[0]
    s = k_ref.shape[0]

    @pl.when(qi == 0)
    def _():
        ca, sa1, sa2 = ta_k_ref[0], ta_k_ref[1], ta_k_ref[2]
        for g in range(N_KV_HEADS):
            kg = k_ref[:, g * HEAD_DIM:(g + 1) * HEAD_DIM].astype(F32)
            kr_scr[:, g * HEAD_DIM:(g + 1) * HEAD_DIM] = _rope(kg, ca, sa1, sa2, ROT_ATTN // 2).astype(BF16)
            va_scr[:, 2 * g * HEAD_DIM:(2 * g + 1) * HEAD_DIM] = v_ref[:, g * HEAD_DIM:(g + 1) * HEAD_DIM]
            va_scr[:, (2 * g + 1) * HEAD_DIM:(2 * g + 2) * HEAD_DIM] = jnp.ones((s, HEAD_DIM), BF16)
        ikp = _rope(ik_ref[...], ti_k_ref[0], ti_k_ref[1], ti_k_ref[2], ROT_IDX // 2)
        ikr_scr[...] = ikp[:, :IDX_DIM].astype(BF16)

    for n in range(s // tq):
        @pl.when(qi == n)
        def _():
            _dsa_tile((n + 1) * tq, q_ref, iq_ref, ta_q_ref, ti_q_ref, o_ref, kr_scr, va_scr, ikr_scr,
                      key_scr, bias_scr, qs_scr, n_sel)


def _dsa(proj, idx, tab_a, tab_i, batch, seq, n_sel, tq):
    nq = seq // tq
    q_col = 2 * LRU_WIDTH // ATT_WIDTH
    k_col = (2 * LRU_WIDTH + ATT_WIDTH) // KV_WIDTH
    ik_col = IDX_HEADS * IDX_DIM // LANES
    return pl.pallas_call(
        functools.partial(_dsa_body, n_sel=n_sel),
        grid=(batch, nq),
        in_specs=[pl.BlockSpec((tq, ATT_WIDTH), lambda b, i: (b * nq + i, q_col)),
                  pl.BlockSpec((seq, KV_WIDTH), lambda b, i: (b, k_col)),
                  pl.BlockSpec((seq, KV_WIDTH), lambda b, i: (b, k_col + 1)),
                  pl.BlockSpec((tq, IDX_PAD), lambda b, i: (b * nq + i, 0)),
                  pl.BlockSpec((seq, LANES), lambda b, i: (b, ik_col)),
                  pl.BlockSpec((3, tq, LANES), lambda b, i: (0, b * nq + i, 0)),
                  pl.BlockSpec((3, seq, LANES), lambda b, i: (0, b, 0)),
                  pl.BlockSpec((3, tq, LANES), lambda b, i: (0, b * nq + i, 0)),
                  pl.BlockSpec((3, seq, LANES), lambda b, i: (0, b, 0))],
        out_specs=pl.BlockSpec((tq, ATT_WIDTH), lambda b, i: (b * nq + i, 0)),
        out_shape=jax.ShapeDtypeStruct((batch * seq, ATT_WIDTH), BF16),
        scratch_shapes=[pltpu.VMEM((seq, KV_WIDTH), BF16),
                        pltpu.VMEM((seq, 2 * KV_WIDTH), BF16),
                        pltpu.VMEM((seq, IDX_DIM), BF16),
                        pltpu.VMEM((tq, seq), jnp.int32),
                        pltpu.VMEM((tq, seq), F32),
                        pltpu.VMEM((N_HEADS // N_KV_HEADS * tq, HEAD_DIM), BF16)],
        compiler_params=_cp(("arbitrary", "arbitrary")),
        name="dsa_attention",
    )(proj, proj, proj, idx, idx, tab_a, tab_a, tab_i, tab_i)


def _rope_tables(positions, rot_dim, period):
    half = rot_dim // 2
    inv = jnp.power(jnp.float32(ROPE_THETA), -jnp.arange(0, rot_dim, 2, dtype=F32) / rot_dim)
    ang = positions.astype(F32).reshape(-1, 1) * inv
    cos, sin = jnp.cos(ang), jnp.sin(ang)
    t = cos.shape[0]
    ones = jnp.ones((t, period - rot_dim), F32)
    zeros = jnp.zeros((t, period - rot_dim), F32)
    zh = jnp.zeros((t, half), F32)
    c = jnp.concatenate([cos, cos, ones], axis=1)
    s1 = jnp.concatenate([-sin, zh, zeros], axis=1)
    s2 = jnp.concatenate([zh, sin, zeros], axis=1)
    reps = LANES // period
    return jnp.stack([jnp.tile(c, (1, reps)), jnp.tile(s1, (1, reps)), jnp.tile(s2, (1, reps))])


HALO = 32


def _cfm_conv_body(u_ref, halo_ref, dw_ref, dwb_ref, g_ref, b_ref, o_ref, ext_scr, acc_scr, *, seq, rc, cc):
    tm, c = u_ref.shape
    i = pl.program_id(0)
    at_start = (i * tm) % seq == 0
    halo = jnp.where(at_start, 0.0, halo_ref[...].astype(F32))
    ext_scr[0:HALO, :] = halo
    ext_scr[HALO:, :] = u_ref[...].astype(F32)
    base = HALO - (CFM_CONV - 1)

    def chunk(r, carry):
        r0 = pl.multiple_of(r * rc, rc)
        for c0 in range(0, c, cc):
            win = ext_scr[pl.ds(r0, rc + HALO), c0:c0 + cc]
            acc = jnp.zeros((rc, cc), F32) + dwb_ref[:, c0:c0 + cc]
            for sub in range(8):
                rolled = win if sub == 0 else pltpu.roll(win, rc + HALO - sub, 0)
                for j in range(CFM_CONV):
                    off = base + j
                    if off % 8 == sub:
                        a0 = off - sub
                        acc = acc + rolled[a0:a0 + rc, :] * dw_ref[j:j + 1, c0:c0 + cc]
            acc_scr[pl.ds(r0, rc), c0:c0 + cc] = acc
        return carry

    lax.fori_loop(0, tm // rc, chunk, 0)
    y = acc_scr[...]
    mu = jnp.mean(y, axis=-1, keepdims=True)
    yc = y - mu
    var = jnp.mean(yc * yc, axis=-1, keepdims=True)
    z = yc * lax.rsqrt(var + 1e-5) * g_ref[...] + b_ref[...]
    o_ref[...] = (z * jax.nn.sigmoid(z)).astype(o_ref.dtype)


def _cfm_conv(u, dw, dwb, ln_g, ln_b, seq, tm):
    t, c = u.shape
    hb = tm // HALO
    vec = lambda v: v.reshape(1, c)
    vspec = pl.BlockSpec((1, c), lambda i: (0, 0))
    return pl.pallas_call(
        functools.partial(_cfm_conv_body, seq=seq, rc=32, cc=256),
        grid=(t // tm,),
        in_specs=[pl.BlockSpec((tm, c), lambda i: (i, 0)),
                  pl.BlockSpec((HALO, c), lambda i: (jnp.maximum(i * hb - 1, 0), 0)),
                  pl.BlockSpec((CFM_CONV, c), lambda i: (0, 0)),
                  vspec, vspec, vspec],
        out_specs=pl.BlockSpec((tm, c), lambda i: (i, 0)),
        out_shape=jax.ShapeDtypeStruct((t, c), BF16),
        scratch_shapes=[pltpu.VMEM((tm + HALO, c), F32), pltpu.VMEM((tm, c), F32)],
        compiler_params=_cp(("arbitrary",)),
        name="cfm_conv_ln",
    )(u, u, dw, vec(dwb), vec(ln_g), vec(ln_b))


HI_MASK = -65536


def _pack_rows(y):
    c = y.shape[1] // 2
    lo = pltpu.bitcast(y[:, :c].astype(BF16).astype(F32), jnp.int32)
    hi = pltpu.bitcast(y[:, c:].astype(BF16).astype(F32), jnp.int32)
    return (hi & HI_MASK) | (jnp.right_shift(lo, 16) & 0xFFFF)


def _unpack_rows(w):
    lo = pltpu.bitcast(jnp.left_shift(w, 16), F32)
    hi = pltpu.bitcast(w & HI_MASK, F32)
    return lo, hi


def _moe_pre_body(x_ref, g_ref, mod_ref, wr_ref, swg_ref, swu_ref, swd_ref, h_ref, lg_ref, sh_ref):
    h = _norm_mod(x_ref[...], g_ref[...], mod_ref, SH_F, SC_F)
    hb = h.astype(BF16)
    h_ref[...] = _pack_rows(h)
    lg_ref[...] = jnp.dot(h, wr_ref[...], preferred_element_type=F32, precision=lax.Precision.HIGHEST).T
    gt = jnp.dot(hb, swg_ref[...], preferred_element_type=F32)
    up = jnp.dot(hb, swu_ref[...], preferred_element_type=F32)
    mid = (gt * jax.nn.sigmoid(gt) * up).astype(BF16)
    sh_ref[...] = jnp.dot(mid, swd_ref[...], preferred_element_type=F32)


def _moe_pre(x2, g, mod, w_router, swg, swu, swd, seq, tm):
    t, d = x2.shape
    e = w_router.shape[1]
    full = lambda a: pl.BlockSpec(a.shape, lambda i: (0, 0))
    return pl.pallas_call(
        _moe_pre_body,
        grid=(t // tm,),
        in_specs=[pl.BlockSpec((tm, d), lambda i: (i, 0)),
                  pl.BlockSpec((1, d), lambda i: (0, 0)),
                  pl.BlockSpec((1, 6, d), lambda i: (i * tm // seq, 0, 0)),
                  full(w_router), full(swg), full(swu), full(swd)],
        out_specs=[pl.BlockSpec((tm, d // 2), lambda i: (i, 0)),
                   pl.BlockSpec((e, tm), lambda i: (0, i)),
                   pl.BlockSpec((tm, d), lambda i: (i, 0))],
        out_shape=[jax.ShapeDtypeStruct((t, d // 2), jnp.int32),
                   jax.ShapeDtypeStruct((e, t), F32),
                   jax.ShapeDtypeStruct((t, d), F32)],
        compiler_params=_cp(("arbitrary",)),
        name="moe_pre",
    )(x2, g.reshape(1, d), mod, w_router, swg, swu, swd)


def _first_max(v, ids, sentinel):
    m = jnp.max(v, axis=0, keepdims=True)
    first = jnp.min(jnp.where(v == m, ids, sentinel), axis=0, keepdims=True)
    return m, first


def _route_body(lg_ref, bias_ref, eidx_ref, rank_ref, gw_ref, cnt_ref):
    e, tm = lg_ref.shape
    gsz = e // N_GROUPS
    neg = -jnp.inf

    @pl.when(pl.program_id(0) == 0)
    def _():
        cnt_ref[...] = jnp.zeros_like(cnt_ref)

    scores = jax.nn.sigmoid(lg_ref[...])
    choice = scores + bias_ref[:, 0:1]
    row = lax.broadcasted_iota(jnp.int32, (e, tm), 0)
    sub = lax.broadcasted_iota(jnp.int32, (gsz, tm), 0)

    parts = []
    for g in range(N_GROUPS):
        vg = choice[g * gsz:(g + 1) * gsz, :]
        m1, f1 = _first_max(vg, sub, gsz)
        m2 = jnp.max(jnp.where(sub == f1, neg, vg), axis=0, keepdims=True)
        parts.append(jnp.broadcast_to(m1 + m2, (gsz, tm)))
    gscore = jnp.concatenate(parts, axis=0)

    gid = jnp.right_shift(row, gsz.bit_length() - 1)
    v = jnp.full((e, tm), neg, F32)
    for _ in range(TOPK_GROUPS):
        _, fg = _first_max(gscore, gid, N_GROUPS)
        hit = gid == fg
        v = jnp.where(hit, choice, v)
        gscore = jnp.where(hit, neg, gscore)

    picks = []
    self32 = jnp.zeros((e, tm), F32)
    for _ in range(TOP_K):
        _, fe = _first_max(v, row, e)
        hit = row == fe
        picks.append(fe)
        self32 = jnp.where(hit, 1.0, self32)
        v = jnp.where(hit, neg, v)

    picked = self32 * scores
    gwd = picked / jnp.sum(picked, axis=0, keepdims=True) * ROUTED_SCALE

    before = (lax.broadcasted_iota(jnp.int32, (tm, tm), 0)
              < lax.broadcasted_iota(jnp.int32, (tm, tm), 1)).astype(BF16)
    rank = jnp.dot(self32.astype(BF16), before, preferred_element_type=F32) + cnt_ref[:, 0:1]
    cnt_ref[...] = cnt_ref[...] + jnp.sum(self32, axis=1, keepdims=True)

    for k in range(TOP_K):
        hit = row == picks[k]
        eidx_ref[k:k + 1, :] = picks[k]
        rank_ref[k:k + 1, :] = jnp.sum(jnp.where(hit, rank, 0.0), axis=0, keepdims=True).astype(jnp.int32)
        gw_ref[k:k + 1, :] = jnp.sum(jnp.where(hit, gwd, 0.0), axis=0, keepdims=True)


def _route(logits_t, e_bias, tm):
    e, t = logits_t.shape
    kspec = pl.BlockSpec((TOP_K, tm), lambda i: (0, i))
    return pl.pallas_call(
        _route_body,
        grid=(t // tm,),
        in_specs=[pl.BlockSpec((e, tm), lambda i: (0, i)),
                  pl.BlockSpec((e, LANES), lambda i: (0, 0))],
        out_specs=[kspec, kspec, kspec, pl.BlockSpec((e, LANES), lambda i: (0, 0))],
        out_shape=[jax.ShapeDtypeStruct((TOP_K, t), jnp.int32),
                   jax.ShapeDtypeStruct((TOP_K, t), jnp.int32),
                   jax.ShapeDtypeStruct((TOP_K, t), F32),
                   jax.ShapeDtypeStruct((e, LANES), F32)],
        compiler_params=_cp(("arbitrary",)),
        name="moe_route",
    )(logits_t, jnp.broadcast_to(e_bias.astype(F32).reshape(e, 1), (e, LANES)))


def _experts_body(be_ref, nu_ref, x_ref, wg_ref, wu_ref, wd_ref, o_ref, wg_scr, wu_scr, wd_scr):
    i = pl.program_id(0)
    half = x_ref.shape[1]

    @pl.when(i >= nu_ref[0])
    def _():
        o_ref[...] = jnp.zeros_like(o_ref)

    @pl.when(i < nu_ref[0])
    def _():
        prev = be_ref[jnp.maximum(i - 1, 0)]
        changed = jnp.logical_or(i == 0, be_ref[i] != prev)

        @pl.when(changed)
        def _():
            wg_scr[...] = wg_ref[0, 0].astype(BF16)
            wu_scr[...] = wu_ref[0, 0].astype(BF16)
            wd_scr[...] = wd_ref[0, 0].astype(BF16)

        lo, hi = _unpack_rows(x_ref[...])
        lo = lo.astype(BF16)
        hi = hi.astype(BF16)
        gt = (jnp.dot(lo, wg_scr[:half, :], preferred_element_type=F32)
              + jnp.dot(hi, wg_scr[half:, :], preferred_element_type=F32))
        up = (jnp.dot(lo, wu_scr[:half, :], preferred_element_type=F32)
              + jnp.dot(hi, wu_scr[half:, :], preferred_element_type=F32))
        mid = (gt * jax.nn.sigmoid(gt) * up).astype(BF16)
        o_ref[...] = _pack_rows(jnp.dot(mid, wd_scr[...], preferred_element_type=F32))


def _experts(x_sorted, block_exp, n_used, wg, wu, wd, layer, tm):
    n_rows, half = x_sorted.shape
    d, ff = wg.shape[2], wg.shape[3]
    n_blocks = n_rows // tm
    row_map = lambda i, be, nu: (jnp.minimum(i, nu[0] - 1), 0)
    grid_spec = pltpu.PrefetchScalarGridSpec(
        num_scalar_prefetch=2,
        grid=(n_blocks,),
        in_specs=[pl.BlockSpec((tm, half), row_map),
                  pl.BlockSpec((1, 1, d, ff), lambda i, be, nu: (layer, be[i], 0, 0)),
                  pl.BlockSpec((1, 1, d, ff), lambda i, be, nu: (layer, be[i], 0, 0)),
                  pl.BlockSpec((1, 1, ff, d), lambda i, be, nu: (layer, be[i], 0, 0))],
        out_specs=pl.BlockSpec((tm, half), lambda i, be, nu: (i, 0)),
        scratch_shapes=[pltpu.VMEM((d, ff), BF16), pltpu.VMEM((d, ff), BF16), pltpu.VMEM((ff, d), BF16)],
    )
    return pl.pallas_call(
        _experts_body,
        grid_spec=grid_spec,
        out_shape=jax.ShapeDtypeStruct((n_rows, half), jnp.int32),
        compiler_params=_cp(("arbitrary",)),
        name="moe_experts",
    )(block_exp, n_used, x_sorted, wg, wu, wd)


TOK_STEP = 1024
TOK_SUB = 256


def _stage_pos(pos_hbm, pos_smem, sem, step, n_tok):
    copies = [pltpu.make_async_copy(pos_hbm.at[pl.ds(k * n_tok + step * TOK_STEP, TOK_STEP)],
                                    pos_smem.at[pl.ds(k * TOK_STEP, TOK_STEP)], sem)
              for k in range(TOP_K)]
    for cp in copies:
        cp.start()
    for cp in copies:
        cp.wait()


def _dispatch_body(ends_ref, padded_ref, pos_hbm, h_ref, xs_hbm, pos_smem, zero_scr, sem_pos, sem_rows,
                   *, n_tok, tme):
    i, j = pl.program_id(0), pl.program_id(1)
    sub = h_ref.shape[0]

    @pl.when(jnp.logical_and(i == 0, j == 0))
    def _():
        zero_scr[...] = jnp.zeros_like(zero_scr)

        def fill(e):
            start = pl.multiple_of(ends_ref[e] - tme, tme)
            return pltpu.make_async_copy(zero_scr, xs_hbm.at[pl.ds(start, tme)], sem_rows)

        for e in range(N_EXPERTS):
            @pl.when(padded_ref[e] > 0)
            def _():
                fill(e).start()
        for e in range(N_EXPERTS):
            @pl.when(padded_ref[e] > 0)
            def _():
                fill(e).wait()

        def tail(b):
            return pltpu.make_async_copy(zero_scr, xs_hbm.at[pl.ds(pl.multiple_of(b * tme, tme), tme)], sem_rows)

        n_used = ends_ref[N_EXPERTS - 1] // tme
        n_blocks = xs_hbm.shape[0] // tme
        lax.fori_loop(n_used, n_blocks, lambda b, c: (tail(b).start(), c)[1], 0)
        lax.fori_loop(n_used, n_blocks, lambda b, c: (tail(b).wait(), c)[1], 0)

    @pl.when(j == 0)
    def _():
        _stage_pos(pos_hbm, pos_smem, sem_pos, i, n_tok)

    def tok(t, carry):
        for k in range(TOP_K):
            p = pos_smem[k * TOK_STEP + j * sub + t]
            pltpu.make_async_copy(h_ref.at[t], xs_hbm.at[p], sem_rows).start()
        return carry

    lax.fori_loop(0, sub, tok, 0, unroll=4)
    for k in range(TOP_K):
        pltpu.make_async_copy(h_ref, xs_hbm.at[pl.ds(0, sub)], sem_rows).wait()


def _dispatch(h, pos_flat, ends, padded, n_rows, tme):
    t, half = h.shape
    sub = min(TOK_SUB, t)
    grid_spec = pltpu.PrefetchScalarGridSpec(
        num_scalar_prefetch=2,
        grid=(t // TOK_STEP, TOK_STEP // sub),
        in_specs=[pl.BlockSpec(memory_space=pl.ANY),
                  pl.BlockSpec((sub, half), lambda i, j, en, pd: (i * (TOK_STEP // sub) + j, 0))],
        out_specs=pl.BlockSpec(memory_space=pl.ANY),
        scratch_shapes=[pltpu.SMEM((TOP_K * TOK_STEP,), jnp.int32),
                        pltpu.VMEM((tme, half), jnp.int32),
                        pltpu.SemaphoreType.DMA, pltpu.SemaphoreType.DMA],
    )
    return pl.pallas_call(
        functools.partial(_dispatch_body, n_tok=t, tme=tme),
        grid_spec=grid_spec,
        out_shape=jax.ShapeDtypeStruct((n_rows, half), jnp.int32),
        compiler_params=_cp(("arbitrary", "arbitrary")),
        name="moe_dispatch",
    )(ends, padded, pos_flat, h)


def _combine_body(pos_hbm, ys_hbm, gw_ref, x_ref, sh_ref, mod_ref, fg_ref, o_ref, pos_smem, g_scr,
                  sem_pos, sem_rows, *, n_tok, final_norm):
    i, j = pl.program_id(0), pl.program_id(1)
    sub, d = x_ref.shape
    half = d // 2

    @pl.when(j == 0)
    def _():
        _stage_pos(pos_hbm, pos_smem, sem_pos, i, n_tok)

    def tok(t, carry):
        for k in range(TOP_K):
            p = pos_smem[k * TOK_STEP + j * sub + t]
            pltpu.make_async_copy(ys_hbm.at[p], g_scr.at[k, t], sem_rows).start()
        return carry

    lax.fori_loop(0, sub, tok, 0, unroll=4)
    for k in range(TOP_K):
        pltpu.make_async_copy(ys_hbm.at[pl.ds(0, sub)], g_scr.at[k], sem_rows).wait()

    acc_lo = sh_ref[:, :half]
    acc_hi = sh_ref[:, half:]
    for k in range(TOP_K):
        lo, hi = _unpack_rows(g_scr[k])
        w = gw_ref[:, k:k + 1]
        acc_lo = acc_lo + w * lo
        acc_hi = acc_hi + w * hi
    gate = mod_ref[0, G_F:G_F + 1, :]
    y_lo = x_ref[:, :half] + gate[:, :half] * acc_lo
    y_hi = x_ref[:, half:] + gate[:, half:] * acc_hi
    if final_norm:
        ms = (jnp.sum(y_lo * y_lo, axis=-1, keepdims=True) + jnp.sum(y_hi * y_hi, axis=-1, keepdims=True)) / d
        r = lax.rsqrt(ms + 1e-6)
        y_lo = y_lo * r * fg_ref[:, :half]
        y_hi = y_hi * r * fg_ref[:, half:]
    o_ref[:, :half] = y_lo
    o_ref[:, half:] = y_hi


def _combine(ys, pos_flat, gw, x2, shared, mod, final_g, seq, final_norm):
    t, d = x2.shape
    sub = min(TOK_SUB, t)
    nj = TOK_STEP // sub
    row = lambda i, j: (i * nj + j, 0)
    return pl.pallas_call(
        functools.partial(_combine_body, n_tok=t, final_norm=final_norm),
        grid=(t // TOK_STEP, nj),
        in_specs=[pl.BlockSpec(memory_space=pl.ANY),
                  pl.BlockSpec(memory_space=pl.ANY),
                  pl.BlockSpec((sub, TOP_K), row),
                  pl.BlockSpec((sub, d), row),
                  pl.BlockSpec((sub, d), row),
                  pl.BlockSpec((1, 6, d), lambda i, j: ((i * nj + j) * sub // seq, 0, 0)),
                  pl.BlockSpec((1, d), lambda i, j: (0, 0))],
        out_specs=pl.BlockSpec((sub, d), row),
        out_shape=jax.ShapeDtypeStruct((t, d), F32),
        scratch_shapes=[pltpu.SMEM((TOP_K * TOK_STEP,), jnp.int32),
                        pltpu.VMEM((TOP_K, sub, d // 2), jnp.int32),
                        pltpu.SemaphoreType.DMA, pltpu.SemaphoreType.DMA],
        compiler_params=_cp(("arbitrary", "arbitrary")),
        name="moe_combine",
    )(pos_flat, ys, gw, x2, shared, mod, final_g.reshape(1, d))


def _dispatch_plan(eidx_t, rank_t, counts, tm):
    n_assign = eidx_t.size
    padded = (counts + tm - 1) // tm * tm
    ends = jnp.cumsum(padded)
    pstart = ends - padded
    pos_t = rank_t
    for e in range(N_EXPERTS):
        pos_t = pos_t + jnp.where(eidx_t == e, pstart[e], 0)
    n_blocks = -(-n_assign // tm) + N_EXPERTS
    blk_start = jnp.arange(n_blocks, dtype=jnp.int32) * tm
    block_exp = jnp.minimum(jnp.sum((ends[None, :] <= blk_start[:, None]).astype(jnp.int32), axis=1),
                            N_EXPERTS - 1)
    n_used = (ends[-1] // tm).astype(jnp.int32).reshape(1)
    return pos_t.reshape(-1), ends, padded, block_exp, n_used, n_blocks


def _moe(x2, g, mod, w_router, e_bias, wg, wu, wd, layer, swg, swu, swd, final_g, seq, tm, tme, final_norm):
    h, logits_t, shared = _moe_pre(x2, g, mod, w_router, swg.astype(BF16), swu.astype(BF16),
                                   swd.astype(BF16), seq, tm)
    eidx_t, rank_t, gw_t, cnt = _route(logits_t, e_bias, tm)
    counts = cnt[:, 0].astype(jnp.int32)
    pos_flat, ends, padded, block_exp, n_used, n_blocks = _dispatch_plan(eidx_t, rank_t, counts, tme)
    x_sorted = _dispatch(h, pos_flat, ends, padded, n_blocks * tme, tme)
    y_sorted = _experts(x_sorted, block_exp, n_used, wg, wu, wd, layer, tme)
    return _combine(y_sorted, pos_flat, gw_t.T, x2, shared, mod, final_g, seq, final_norm)


def kernel(x, c, positions, mod_w, mod_b, norm_mix, norm_ffn, hyb_w_in, hyb_w_out, lru_conv_w, lru_conv_b, lru_wa, lru_ba, lru_wx, lru_bx, lru_lambda, cfm_w1, cfm_b1, cfm_dw, cfm_dwb, cfm_ln_g, cfm_ln_b, cfm_w2, cfm_b2, moe_router, moe_bias, moe_wg, moe_wu, moe_wd, sh_wg, sh_wu, sh_wd, final_norm):
    batch, seq, d = x.shape
    t = batch * seq
    n_sel = min(INDEX_TOPK, seq // 4)
    tm = min(512, seq)
    tq = min(256, seq)
    tme = MOE_ROWS

    mod_all = _modulation(c, mod_w, mod_b).reshape(mod_w.shape[0], batch, 6, d)
    tab_a = _rope_tables(positions, ROT_ATTN, HEAD_DIM)
    tab_i = _rope_tables(positions, ROT_IDX, IDX_DIM)
    x2 = x.reshape(t, d)
    zero_bias = jnp.zeros((d,), F32)

    mod = mod_all[0]
    w_in = hyb_w_in[0]
    w_main = w_in[:, :MAIN_COLS].astype(BF16)
    w_idx = jnp.pad(w_in[:, MAIN_COLS:], ((0, 0), (0, IDX_PAD - IDX_COLS))).astype(BF16)
    proj = _nm_matmul(x2, norm_mix[0], mod, w_main, seq, tm, MAIN_COLS, BF16, "hyb_in_main")
    idx = _nm_matmul(x2, norm_mix[0], mod, w_idx, seq, tm, IDX_PAD, F32, "hyb_in_idx")
    y_lru = _rg_lru(proj, lru_conv_w[0], lru_conv_b[0], lru_wa[0], lru_ba[0], lru_wx[0], lru_bx[0],
                    lru_lambda[0], batch, seq)
    y_att = _dsa(proj, idx, tab_a, tab_i, batch, seq, n_sel, tq)
    x2 = _proj_residual([y_lru, y_att], hyb_w_out[0].astype(BF16), zero_bias, x2, mod, G_M, seq, tm,
                        "hyb_out")
    x2 = _moe(x2, norm_ffn[0], mod, moe_router[0], moe_bias[0], moe_wg, moe_wu, moe_wd, 0,
              sh_wg[0], sh_wu[0], sh_wd[0], final_norm, seq, tm, tme, False)

    mod = mod_all[1]
    u = _nm_glu(x2, norm_mix[1], mod, cfm_w1[0].astype(BF16), cfm_b1[0], seq, tm, d)
    z = _cfm_conv(u, cfm_dw[0], cfm_dwb[0], cfm_ln_g[0], cfm_ln_b[0], seq, tm)
    x2 = _proj_residual([z], cfm_w2[0].astype(BF16), cfm_b2[0], x2, mod, G_M, seq, tm, "cfm_out")
    out = _moe(x2, norm_ffn[1], mod, moe_router[1], moe_bias[1], moe_wg, moe_wu, moe_wd, 1,
               sh_wg[1], sh_wu[1], sh_wd[1], final_norm, seq, tm, tme, True)
    return out.reshape(batch, seq, d)
```

```python
import functools

import jax
import jax.numpy as jnp
from jax import lax
from jax.experimental import pallas as pl
from jax.experimental.pallas import tpu as pltpu

F32 = jnp.float32
BF16 = jnp.bfloat16

LRU_WIDTH = 1024
LRU_BW = 128
LRU_CONV = 4
LRU_C = 8.0
N_HEADS = 8
N_KV_HEADS = 2
HEAD_DIM = 128
ATT_WIDTH = N_HEADS * HEAD_DIM
KV_WIDTH = N_KV_HEADS * HEAD_DIM
IDX_HEADS = 8
IDX_DIM = 64
INDEX_TOPK = 256
ROPE_THETA = 500000.0
ROT_ATTN = HEAD_DIM // 4
ROT_IDX = IDX_DIM // 4
CFM_CONV = 31
N_EXPERTS = 64
TOP_K = 8
N_GROUPS = 8
TOPK_GROUPS = 4
ROUTED_SCALE = 2.5

MAIN_COLS = 2 * LRU_WIDTH + ATT_WIDTH + 2 * KV_WIDTH
IDX_COLS = IDX_HEADS * IDX_DIM + IDX_DIM + IDX_HEADS
IDX_PAD = 640

MOE_ROWS = 512
LANES = 128
VMEM_LIMIT = 56 * 1024 * 1024
INT_MIN = -2147483648
NEG_BIG = -1e30

SH_M, SC_M, G_M, SH_F, SC_F, G_F = range(6)


def _cp(sem):
    return pltpu.CompilerParams(dimension_semantics=sem, vmem_limit_bytes=VMEM_LIMIT)


def _weight_spec(block, index_map, n_col_blocks):
    if n_col_blocks == 1:
        return pl.BlockSpec(block, index_map, pipeline_mode=pl.Buffered(1))
    return pl.BlockSpec(block, index_map)


def _norm_mod(x, g, mod_ref, sh_row, sc_row):
    ms = jnp.mean(x * x, axis=-1, keepdims=True)
    y = x * lax.rsqrt(ms + 1e-6) * g
    return y * (1.0 + mod_ref[0, sc_row:sc_row + 1, :]) + mod_ref[0, sh_row:sh_row + 1, :]


def _mod_body(c_ref, w_ref, b_ref, o_ref):
    c = c_ref[...]
    cond = c * jax.nn.sigmoid(c)
    o_ref[0] = jnp.dot(cond, w_ref[0], preferred_element_type=F32,
                       precision=lax.Precision.HIGHEST) + b_ref[0]


def _modulation(c, mod_w, mod_b):
    depth, d, n = mod_w.shape
    b = c.shape[0]
    tn = 512
    return pl.pallas_call(
        _mod_body,
        grid=(depth, n // tn),
        in_specs=[pl.BlockSpec((b, d), lambda l, j: (0, 0)),
                  pl.BlockSpec((1, d, tn), lambda l, j: (l, 0, j)),
                  pl.BlockSpec((1, 1, tn), lambda l, j: (l, 0, j))],
        out_specs=pl.BlockSpec((1, b, tn), lambda l, j: (l, 0, j)),
        out_shape=jax.ShapeDtypeStruct((depth, b, n), F32),
        compiler_params=_cp(("arbitrary", "arbitrary")),
        name="modulation",
    )(c, mod_w, mod_b.reshape(depth, 1, n))


def _nm_matmul_body(x_ref, g_ref, mod_ref, w_ref, o_ref, h_scr):
    @pl.when(pl.program_id(1) == 0)
    def _():
        h_scr[...] = _norm_mod(x_ref[...], g_ref[...], mod_ref, SH_M, SC_M).astype(BF16)

    o_ref[...] = jnp.dot(h_scr[...], w_ref[...], preferred_element_type=F32).astype(o_ref.dtype)


def _nm_matmul(x2, g, mod, w, seq, tm, tn, out_dtype, name):
    t, d = x2.shape
    n = w.shape[1]
    return pl.pallas_call(
        _nm_matmul_body,
        grid=(t // tm, n // tn),
        in_specs=[pl.BlockSpec((tm, d), lambda i, j: (i, 0)),
                  pl.BlockSpec((1, d), lambda i, j: (0, 0)),
                  pl.BlockSpec((1, 6, d), lambda i, j: (i * tm // seq, 0, 0)),
                  _weight_spec((d, tn), lambda i, j: (0, j), n // tn)],
        out_specs=pl.BlockSpec((tm, tn), lambda i, j: (i, j)),
        out_shape=jax.ShapeDtypeStruct((t, n), out_dtype),
        scratch_shapes=[pltpu.VMEM((tm, d), BF16)],
        compiler_params=_cp(("arbitrary", "arbitrary")),
        name=name,
    )(x2, g.reshape(1, d), mod, w)


def _nm_glu_body(x_ref, g_ref, mod_ref, wa_ref, wg_ref, ba_ref, bg_ref, o_ref, h_scr):
    @pl.when(pl.program_id(1) == 0)
    def _():
        h_scr[...] = _norm_mod(x_ref[...], g_ref[...], mod_ref, SH_M, SC_M).astype(BF16)

    h = h_scr[...]
    a = jnp.dot(h, wa_ref[...], preferred_element_type=F32) + ba_ref[...]
    gt = jnp.dot(h, wg_ref[...], preferred_element_type=F32) + bg_ref[...]
    o_ref[...] = (a * jax.nn.sigmoid(gt)).astype(o_ref.dtype)


def _nm_glu(x2, g, mod, w1, b1, seq, tm, tn):
    t, d = x2.shape
    n = w1.shape[1] // 2
    nj = n // tn
    return pl.pallas_call(
        _nm_glu_body,
        grid=(t // tm, nj),
        in_specs=[pl.BlockSpec((tm, d), lambda i, j: (i, 0)),
                  pl.BlockSpec((1, d), lambda i, j: (0, 0)),
                  pl.BlockSpec((1, 6, d), lambda i, j: (i * tm // seq, 0, 0)),
                  _weight_spec((d, tn), lambda i, j: (0, j), nj),
                  _weight_spec((d, tn), lambda i, j: (0, j + nj), nj),
                  pl.BlockSpec((1, tn), lambda i, j: (0, j)),
                  pl.BlockSpec((1, tn), lambda i, j: (0, j + nj))],
        out_specs=pl.BlockSpec((tm, tn), lambda i, j: (i, j)),
        out_shape=jax.ShapeDtypeStruct((t, n), BF16),
        scratch_shapes=[pltpu.VMEM((tm, d), BF16)],
        compiler_params=_cp(("arbitrary", "arbitrary")),
        name="cfm_in_glu",
    )(x2, g.reshape(1, d), mod, w1, w1, b1.reshape(1, 2 * n), b1.reshape(1, 2 * n))


def _proj_res_body(*refs, n_parts, gate_row):
    part_refs = refs[:n_parts]
    w_ref, b_ref, x_ref, mod_ref, o_ref = refs[n_parts:]
    acc = None
    off = 0
    for p in part_refs:
        k = p.shape[1]
        y = jnp.dot(p[...], w_ref[off:off + k, :], preferred_element_type=F32)
        acc = y if acc is None else acc + y
        off += k
    y = acc + b_ref[...]
    o_ref[...] = x_ref[...] + mod_ref[0, gate_row:gate_row + 1, :] * y


def _proj_residual(parts, w, bias, x2, mod, gate_row, seq, tm, name):
    t, d = x2.shape
    n = w.shape[1]
    in_specs = [pl.BlockSpec((tm, p.shape[1]), lambda i: (i, 0)) for p in parts]
    in_specs += [pl.BlockSpec(w.shape, lambda i: (0, 0)),
                 pl.BlockSpec((1, n), lambda i: (0, 0)),
                 pl.BlockSpec((tm, d), lambda i: (i, 0)),
                 pl.BlockSpec((1, 6, d), lambda i: (i * tm // seq, 0, 0))]
    return pl.pallas_call(
        functools.partial(_proj_res_body, n_parts=len(parts), gate_row=gate_row),
        grid=(t // tm,),
        in_specs=in_specs,
        out_specs=pl.BlockSpec((tm, n), lambda i: (i, 0)),
        out_shape=jax.ShapeDtypeStruct((t, n), F32),
        compiler_params=_cp(("arbitrary",)),
        name=name,
    )(*parts, w, bias.reshape(1, n), x2, mod)


def _shift_rows(x, d, fill, row):
    return jnp.where(row >= d, pltpu.roll(x, d, 0), fill)


def _lru_body(lx_ref, lg_ref, cw_ref, cb_ref, wa_ref, ba_ref, wx_ref, bx_ref, lam_ref, o_ref):
    s, cw = lx_ref.shape
    x = lx_ref[...].astype(F32)
    row = lax.broadcasted_iota(jnp.int32, (s, cw), 0)
    xc = x * cw_ref[LRU_CONV - 1:LRU_CONV, :] + cb_ref[...]
    for d in range(1, LRU_CONV):
        xc = xc + _shift_rows(x, d, 0.0, row) * cw_ref[LRU_CONV - 1 - d:LRU_CONV - d, :]
    xb = xc.astype(BF16)
    ga = jax.nn.sigmoid(jnp.dot(xb, wa_ref[0].astype(BF16), preferred_element_type=F32) + ba_ref[...])
    gx = jax.nn.sigmoid(jnp.dot(xb, wx_ref[0].astype(BF16), preferred_element_type=F32) + bx_ref[...])
    z = -lam_ref[...]
    softplus = jnp.maximum(z, 0.0) + jnp.log(1.0 + jnp.exp(-jnp.abs(z)))
    log_a = (-LRU_C) * ga * softplus
    a = jnp.exp(log_a)
    mult = jnp.sqrt(1.0 - a * a)
    mult = jnp.where(row == 0, 1.0, mult)
    b = gx * xc * mult
    d = 1
    while d < s:
        a_sh = _shift_rows(a, d, 1.0, row)
        b_sh = _shift_rows(b, d, 0.0, row)
        b = a * b_sh + b
        a = a * a_sh
        d *= 2
    lg = lg_ref[...].astype(F32)
    gelu = 0.5 * lg * (1.0 + jnp.tanh(0.7978845608028654 * (lg + 0.044715 * lg * lg * lg)))
    o_ref[...] = (b * gelu).astype(o_ref.dtype)


def _rg_lru(proj, conv_w, conv_b, wa, ba, wx, bx, lam, batch, seq):
    cw = LRU_BW
    nc = LRU_WIDTH // cw
    vec = lambda v: v.reshape(1, LRU_WIDTH)
    vspec = pl.BlockSpec((1, cw), lambda b, c: (0, c))
    return pl.pallas_call(
        _lru_body,
        grid=(batch, nc),
        in_specs=[pl.BlockSpec((seq, cw), lambda b, c: (b, c)),
                  pl.BlockSpec((seq, cw), lambda b, c: (b, c + nc)),
                  pl.BlockSpec((LRU_CONV, cw), lambda b, c: (0, c)),
                  vspec,
                  pl.BlockSpec((1, cw, cw), lambda b, c: (c, 0, 0)),
                  vspec,
                  pl.BlockSpec((1, cw, cw), lambda b, c: (c, 0, 0)),
                  vspec, vspec],
        out_specs=pl.BlockSpec((seq, cw), lambda b, c: (b, c)),
        out_shape=jax.ShapeDtypeStruct((batch * seq, LRU_WIDTH), BF16),
        compiler_params=_cp(("arbitrary", "arbitrary")),
        name="rg_lru",
    )(proj, proj, conv_w, vec(conv_b), wa, vec(ba), wx, vec(bx), vec(lam))


def _rope(x, c, s1, s2, half):
    w = x.shape[1]
    return x * c + pltpu.roll(x, w - half, 1) * s1 + pltpu.roll(x, half, 1) * s2


def _sort_key(x):
    bits = pltpu.bitcast(x, jnp.int32)
    return bits ^ (jnp.right_shift(bits, 31) & 0x7FFFFFFF)


def _dsa_tile(sk, q_ref, v_ref, iq_ref, ta_q_ref, ti_q_ref, o_ref, kr_scr, ikr_scr, key_scr, bias_scr, n_sel):
    qi = pl.program_id(1)
    tq = q_ref.shape[0]
    per_group = N_HEADS // N_KV_HEADS

    ci, si1, si2 = ti_q_ref[0], ti_q_ref[1], ti_q_ref[2]
    w_scale = (IDX_HEADS ** -0.5) * (IDX_DIM ** -0.5)
    iw = iq_ref[:, IDX_HEADS * IDX_DIM + IDX_DIM:IDX_HEADS * IDX_DIM + LANES] * w_scale
    ikr = ikr_scr[:sk, :]
    score = jnp.zeros((tq, sk), F32)
    for hp in range(IDX_HEADS // 2):
        pair = _rope(iq_ref[:, hp * LANES:(hp + 1) * LANES], ci, si1, si2, ROT_IDX // 2).astype(BF16)
        for sub in range(2):
            h = 2 * hp + sub
            qh = pair[:, sub * IDX_DIM:(sub + 1) * IDX_DIM]
            dots = lax.dot_general(qh, ikr, (((1,), (1,)), ((), ())), preferred_element_type=F32)
            score = score + jnp.maximum(dots, 0.0) * iw[:, h:h + 1]

    t_row = qi * tq + lax.broadcasted_iota(jnp.int32, (tq, sk), 0)
    col = lax.broadcasted_iota(jnp.int32, (tq, sk), 1)
    causal = col <= t_row
    key_scr[:, :sk] = jnp.where(causal, _sort_key(score), INT_MIN)

    kf = float(n_sel)

    def count_ge(cand):
        return jnp.sum((key_scr[:, :sk] >= cand).astype(F32), axis=1, keepdims=True)

    tau0 = jnp.where(count_ge(jnp.zeros((tq, 1), jnp.int32)) >= kf, 0, INT_MIN).astype(jnp.int32)

    def bis(i, tau):
        cand = tau + jnp.left_shift(jnp.int32(1), 30 - i)
        return jnp.where(count_ge(cand) >= kf, cand, tau)

    tau = lax.fori_loop(0, 31, bis, tau0)

    keys = key_scr[:, :sk]
    n_gt = jnp.sum((keys > tau).astype(F32), axis=1, keepdims=True)
    n_ge = jnp.sum((keys >= tau).astype(F32), axis=1, keepdims=True)
    need = kf - n_gt
    tie = keys == tau
    excess = jnp.logical_and(n_ge > kf, tau > INT_MIN)
    any_excess = jnp.max(excess.astype(F32)) > 0.0

    bias_scr[:, :sk] = jnp.where(jnp.logical_and(keys >= tau, causal), 0.0, NEG_BIG)

    @pl.when(any_excess)
    def _():
        nbits = sk.bit_length()

        def jb(i, jcur):
            cand = jcur + jnp.left_shift(jnp.int32(1), nbits - 1 - i)
            cnt = jnp.sum(jnp.logical_and(tie, col < cand).astype(F32), axis=1, keepdims=True)
            return jnp.where(cnt <= need, cand, jcur)

        jlim = lax.fori_loop(0, nbits, jb, jnp.zeros((tq, 1), jnp.int32))
        jlim = jnp.where(excess, jlim, sk)
        sel = jnp.logical_or(keys > tau, jnp.logical_and(tie, col < jlim))
        bias_scr[:, :sk] = jnp.where(jnp.logical_and(sel, causal), 0.0, NEG_BIG)

    ca, sa1, sa2 = ta_q_ref[0], ta_q_ref[1], ta_q_ref[2]
    scale = HEAD_DIM ** -0.5
    for h in range(N_HEADS):
        g = h // per_group
        qh = _rope(q_ref[:, h * HEAD_DIM:(h + 1) * HEAD_DIM].astype(F32), ca, sa1, sa2, ROT_ATTN // 2)
        qh = qh.astype(BF16)
        kg = kr_scr[:sk, g * HEAD_DIM:(g + 1) * HEAD_DIM]
        logits = lax.dot_general(qh, kg, (((1,), (1,)), ((), ())), preferred_element_type=F32)
        logits = logits * scale + bias_scr[:, :sk]
        m = jnp.max(logits, axis=1, keepdims=True)
        p = jnp.exp(logits - m)
        den = jnp.sum(p, axis=1, keepdims=True)
        vg = v_ref[:sk, g * HEAD_DIM:(g + 1) * HEAD_DIM]
        o = jnp.dot(p.astype(BF16), vg, preferred_element_type=F32) / den
        o_ref[:, h * HEAD_DIM:(h + 1) * HEAD_DIM] = o.astype(o_ref.dtype)


def _dsa_body(q_ref, k_ref, v_ref, iq_ref, ik_ref, ta_q_ref, ta_k_ref, ti_q_ref, ti_k_ref,
              o_ref, kr_scr, ikr_scr, key_scr, bias_scr, *, n_sel):
    qi = pl.program_id(1)
    s = k_ref.shape[0]

    @pl.when(qi == 0)
    def _():
        ca, sa1, sa2 = ta_k_ref[0], ta_k_ref[1], ta_k_ref[2]
        for g in range(N_KV_HEADS):
            kg = k_ref[:, g * HEAD_DIM:(g + 1) * HEAD_DIM].astype(F32)
            kr_scr[:, g * HEAD_DIM:(g + 1) * HEAD_DIM] = _rope(kg, ca, sa1, sa2, ROT_ATTN // 2).astype(BF16)
        ikp = _rope(ik_ref[...], ti_k_ref[0], ti_k_ref[1], ti_k_ref[2], ROT_IDX // 2)
        ikr_scr[...] = ikp[:, :IDX_DIM].astype(BF16)

    _dsa_tile(s, q_ref, v_ref, iq_ref, ta_q_ref, ti_q_ref, o_ref, kr_scr, ikr_scr, key_scr, bias_scr, n_sel)


def _dsa(proj, idx, tab_a, tab_i, batch, seq, n_sel, tq):
    nq = seq // tq
    q_col = 2 * LRU_WIDTH // ATT_WIDTH
    k_col = (2 * LRU_WIDTH + ATT_WIDTH) // KV_WIDTH
    ik_col = IDX_HEADS * IDX_DIM // LANES
    return pl.pallas_call(
        functools.partial(_dsa_body, n_sel=n_sel),
        grid=(batch, nq),
        in_specs=[pl.BlockSpec((tq, ATT_WIDTH), lambda b, i: (b * nq + i, q_col)),
                  pl.BlockSpec((seq, KV_WIDTH), lambda b, i: (b, k_col)),
                  pl.BlockSpec((seq, KV_WIDTH), lambda b, i: (b, k_col + 1)),
                  pl.BlockSpec((tq, IDX_PAD), lambda b, i: (b * nq + i, 0)),
                  pl.BlockSpec((seq, LANES), lambda b, i: (b, ik_col)),
                  pl.BlockSpec((3, tq, LANES), lambda b, i: (0, b * nq + i, 0)),
                  pl.BlockSpec((3, seq, LANES), lambda b, i: (0, b, 0)),
                  pl.BlockSpec((3, tq, LANES), lambda b, i: (0, b * nq + i, 0)),
                  pl.BlockSpec((3, seq, LANES), lambda b, i: (0, b, 0))],
        out_specs=pl.BlockSpec((tq, ATT_WIDTH), lambda b, i: (b * nq + i, 0)),
        out_shape=jax.ShapeDtypeStruct((batch * seq, ATT_WIDTH), BF16),
        scratch_shapes=[pltpu.VMEM((seq, KV_WIDTH), BF16),
                        pltpu.VMEM((seq, IDX_DIM), BF16),
                        pltpu.VMEM((tq, seq), jnp.int32),
                        pltpu.VMEM((tq, seq), F32)],
        compiler_params=_cp(("arbitrary", "arbitrary")),
        name="dsa_attention",
    )(proj, proj, proj, idx, idx, tab_a, tab_a, tab_i, tab_i)


def _rope_tables(positions, rot_dim, period):
    half = rot_dim // 2
    inv = jnp.power(jnp.float32(ROPE_THETA), -jnp.arange(0, rot_dim, 2, dtype=F32) / rot_dim)
    ang = positions.astype(F32).reshape(-1, 1) * inv
    cos, sin = jnp.cos(ang), jnp.sin(ang)
    t = cos.shape[0]
    ones = jnp.ones((t, period - rot_dim), F32)
    zeros = jnp.zeros((t, period - rot_dim), F32)
    zh = jnp.zeros((t, half), F32)
    c = jnp.concatenate([cos, cos, ones], axis=1)
    s1 = jnp.concatenate([-sin, zh, zeros], axis=1)
    s2 = jnp.concatenate([zh, sin, zeros], axis=1)
    reps = LANES // period
    return jnp.stack([jnp.tile(c, (1, reps)), jnp.tile(s1, (1, reps)), jnp.tile(s2, (1, reps))])


HALO = 32


def _cfm_conv_body(u_ref, halo_ref, dw_ref, dwb_ref, g_ref, b_ref, o_ref, ext_scr, acc_scr, *, seq, rc, cc):
    tm, c = u_ref.shape
    i = pl.program_id(0)
    at_start = (i * tm) % seq == 0
    halo = jnp.where(at_start, 0.0, halo_ref[...].astype(F32))
    ext_scr[0:HALO, :] = halo
    ext_scr[HALO:, :] = u_ref[...].astype(F32)
    base = HALO - (CFM_CONV - 1)

    def chunk(r, carry):
        r0 = pl.multiple_of(r * rc, rc)
        for c0 in range(0, c, cc):
            win = ext_scr[pl.ds(r0, rc + HALO), c0:c0 + cc]
            acc = jnp.zeros((rc, cc), F32) + dwb_ref[:, c0:c0 + cc]
            for sub in range(8):
                rolled = win if sub == 0 else pltpu.roll(win, rc + HALO - sub, 0)
                for j in range(CFM_CONV):
                    off = base + j
                    if off % 8 == sub:
                        a0 = off - sub
                        acc = acc + rolled[a0:a0 + rc, :] * dw_ref[j:j + 1, c0:c0 + cc]
            acc_scr[pl.ds(r0, rc), c0:c0 + cc] = acc
        return carry

    lax.fori_loop(0, tm // rc, chunk, 0)
    y = acc_scr[...]
    mu = jnp.mean(y, axis=-1, keepdims=True)
    yc = y - mu
    var = jnp.mean(yc * yc, axis=-1, keepdims=True)
    z = yc * lax.rsqrt(var + 1e-5) * g_ref[...] + b_ref[...]
    o_ref[...] = (z * jax.nn.sigmoid(z)).astype(o_ref.dtype)


def _cfm_conv(u, dw, dwb, ln_g, ln_b, seq, tm):
    t, c = u.shape
    hb = tm // HALO
    vec = lambda v: v.reshape(1, c)
    vspec = pl.BlockSpec((1, c), lambda i: (0, 0))
    return pl.pallas_call(
        functools.partial(_cfm_conv_body, seq=seq, rc=32, cc=256),
        grid=(t // tm,),
        in_specs=[pl.BlockSpec((tm, c), lambda i: (i, 0)),
                  pl.BlockSpec((HALO, c), lambda i: (jnp.maximum(i * hb - 1, 0), 0)),
                  pl.BlockSpec((CFM_CONV, c), lambda i: (0, 0)),
                  vspec, vspec, vspec],
        out_specs=pl.BlockSpec((tm, c), lambda i: (i, 0)),
        out_shape=jax.ShapeDtypeStruct((t, c), BF16),
        scratch_shapes=[pltpu.VMEM((tm + HALO, c), F32), pltpu.VMEM((tm, c), F32)],
        compiler_params=_cp(("arbitrary",)),
        name="cfm_conv_ln",
    )(u, u, dw, vec(dwb), vec(ln_g), vec(ln_b))


HI_MASK = -65536


def _pack_rows(y):
    c = y.shape[1] // 2
    lo = pltpu.bitcast(y[:, :c].astype(BF16).astype(F32), jnp.int32)
    hi = pltpu.bitcast(y[:, c:].astype(BF16).astype(F32), jnp.int32)
    return (hi & HI_MASK) | (jnp.right_shift(lo, 16) & 0xFFFF)


def _unpack_rows(w):
    lo = pltpu.bitcast(jnp.left_shift(w, 16), F32)
    hi = pltpu.bitcast(w & HI_MASK, F32)
    return lo, hi


def _moe_pre_body(x_ref, g_ref, mod_ref, wr_ref, swg_ref, swu_ref, swd_ref, h_ref, lg_ref, sh_ref):
    h = _norm_mod(x_ref[...], g_ref[...], mod_ref, SH_F, SC_F)
    hb = h.astype(BF16)
    h_ref[...] = _pack_rows(h)
    lg_ref[...] = jnp.dot(h, wr_ref[...], preferred_element_type=F32, precision=lax.Precision.HIGHEST).T
    gt = jnp.dot(hb, swg_ref[...], preferred_element_type=F32)
    up = jnp.dot(hb, swu_ref[...], preferred_element_type=F32)
    mid = (gt * jax.nn.sigmoid(gt) * up).astype(BF16)
    sh_ref[...] = jnp.dot(mid, swd_ref[...], preferred_element_type=F32)


def _moe_pre(x2, g, mod, w_router, swg, swu, swd, seq, tm):
    t, d = x2.shape
    e = w_router.shape[1]
    full = lambda a: pl.BlockSpec(a.shape, lambda i: (0, 0))
    return pl.pallas_call(
        _moe_pre_body,
        grid=(t // tm,),
        in_specs=[pl.BlockSpec((tm, d), lambda i: (i, 0)),
                  pl.BlockSpec((1, d), lambda i: (0, 0)),
                  pl.BlockSpec((1, 6, d), lambda i: (i * tm // seq, 0, 0)),
                  full(w_router), full(swg), full(swu), full(swd)],
        out_specs=[pl.BlockSpec((tm, d // 2), lambda i: (i, 0)),
                   pl.BlockSpec((e, tm), lambda i: (0, i)),
                   pl.BlockSpec((tm, d), lambda i: (i, 0))],
        out_shape=[jax.ShapeDtypeStruct((t, d // 2), jnp.int32),
                   jax.ShapeDtypeStruct((e, t), F32),
                   jax.ShapeDtypeStruct((t, d), F32)],
        compiler_params=_cp(("arbitrary",)),
        name="moe_pre",
    )(x2, g.reshape(1, d), mod, w_router, swg, swu, swd)


def _first_max(v, ids, sentinel):
    m = jnp.max(v, axis=0, keepdims=True)
    first = jnp.min(jnp.where(v == m, ids, sentinel), axis=0, keepdims=True)
    return m, first


def _route_body(lg_ref, bias_ref, eidx_ref, rank_ref, gw_ref, cnt_ref):
    e, tm = lg_ref.shape
    gsz = e // N_GROUPS
    neg = -jnp.inf

    @pl.when(pl.program_id(0) == 0)
    def _():
        cnt_ref[...] = jnp.zeros_like(cnt_ref)

    scores = jax.nn.sigmoid(lg_ref[...])
    choice = scores + bias_ref[:, 0:1]
    row = lax.broadcasted_iota(jnp.int32, (e, tm), 0)
    sub = lax.broadcasted_iota(jnp.int32, (gsz, tm), 0)

    parts = []
    for g in range(N_GROUPS):
        vg = choice[g * gsz:(g + 1) * gsz, :]
        m1, f1 = _first_max(vg, sub, gsz)
        m2 = jnp.max(jnp.where(sub == f1, neg, vg), axis=0, keepdims=True)
        parts.append(jnp.broadcast_to(m1 + m2, (gsz, tm)))
    gscore = jnp.concatenate(parts, axis=0)

    gid = jnp.right_shift(row, gsz.bit_length() - 1)
    v = jnp.full((e, tm), neg, F32)
    for _ in range(TOPK_GROUPS):
        _, fg = _first_max(gscore, gid, N_GROUPS)
        hit = gid == fg
        v = jnp.where(hit, choice, v)
        gscore = jnp.where(hit, neg, gscore)

    picks = []
    self32 = jnp.zeros((e, tm), F32)
    for _ in range(TOP_K):
        _, fe = _first_max(v, row, e)
        hit = row == fe
        picks.append(fe)
        self32 = jnp.where(hit, 1.0, self32)
        v = jnp.where(hit, neg, v)

    picked = self32 * scores
    gwd = picked / jnp.sum(picked, axis=0, keepdims=True) * ROUTED_SCALE

    before = (lax.broadcasted_iota(jnp.int32, (tm, tm), 0)
              < lax.broadcasted_iota(jnp.int32, (tm, tm), 1)).astype(BF16)
    rank = jnp.dot(self32.astype(BF16), before, preferred_element_type=F32) + cnt_ref[:, 0:1]
    cnt_ref[...] = cnt_ref[...] + jnp.sum(self32, axis=1, keepdims=True)

    for k in range(TOP_K):
        hit = row == picks[k]
        eidx_ref[k:k + 1, :] = picks[k]
        rank_ref[k:k + 1, :] = jnp.sum(jnp.where(hit, rank, 0.0), axis=0, keepdims=True).astype(jnp.int32)
        gw_ref[k:k + 1, :] = jnp.sum(jnp.where(hit, gwd, 0.0), axis=0, keepdims=True)


def _route(logits_t, e_bias, tm):
    e, t = logits_t.shape
    kspec = pl.BlockSpec((TOP_K, tm), lambda i: (0, i))
    return pl.pallas_call(
        _route_body,
        grid=(t // tm,),
        in_specs=[pl.BlockSpec((e, tm), lambda i: (0, i)),
                  pl.BlockSpec((e, LANES), lambda i: (0, 0))],
        out_specs=[kspec, kspec, kspec, pl.BlockSpec((e, LANES), lambda i: (0, 0))],
        out_shape=[jax.ShapeDtypeStruct((TOP_K, t), jnp.int32),
                   jax.ShapeDtypeStruct((TOP_K, t), jnp.int32),
                   jax.ShapeDtypeStruct((TOP_K, t), F32),
                   jax.ShapeDtypeStruct((e, LANES), F32)],
        compiler_params=_cp(("arbitrary",)),
        name="moe_route",
    )(logits_t, jnp.broadcast_to(e_bias.astype(F32).reshape(e, 1), (e, LANES)))


def _experts_body(be_ref, nu_ref, x_ref, wg_ref, wu_ref, wd_ref, o_ref, wg_scr, wu_scr, wd_scr):
    i = pl.program_id(0)
    half = x_ref.shape[1]

    @pl.when(i >= nu_ref[0])
    def _():
        o_ref[...] = jnp.zeros_like(o_ref)

    @pl.when(i < nu_ref[0])
    def _():
        prev = be_ref[jnp.maximum(i - 1, 0)]
        changed = jnp.logical_or(i == 0, be_ref[i] != prev)

        @pl.when(changed)
        def _():
            wg_scr[...] = wg_ref[0, 0].astype(BF16)
            wu_scr[...] = wu_ref[0, 0].astype(BF16)
            wd_scr[...] = wd_ref[0, 0].astype(BF16)

        lo, hi = _unpack_rows(x_ref[...])
        lo = lo.astype(BF16)
        hi = hi.astype(BF16)
        gt = (jnp.dot(lo, wg_scr[:half, :], preferred_element_type=F32)
              + jnp.dot(hi, wg_scr[half:, :], preferred_element_type=F32))
        up = (jnp.dot(lo, wu_scr[:half, :], preferred_element_type=F32)
              + jnp.dot(hi, wu_scr[half:, :], preferred_element_type=F32))
        mid = (gt * jax.nn.sigmoid(gt) * up).astype(BF16)
        o_ref[...] = _pack_rows(jnp.dot(mid, wd_scr[...], preferred_element_type=F32))


def _experts(x_sorted, block_exp, n_used, wg, wu, wd, layer, tm):
    n_rows, half = x_sorted.shape
    d, ff = wg.shape[2], wg.shape[3]
    n_blocks = n_rows // tm
    row_map = lambda i, be, nu: (jnp.minimum(i, nu[0] - 1), 0)
    grid_spec = pltpu.PrefetchScalarGridSpec(
        num_scalar_prefetch=2,
        grid=(n_blocks,),
        in_specs=[pl.BlockSpec((tm, half), row_map),
                  pl.BlockSpec((1, 1, d, ff), lambda i, be, nu: (layer, be[i], 0, 0)),
                  pl.BlockSpec((1, 1, d, ff), lambda i, be, nu: (layer, be[i], 0, 0)),
                  pl.BlockSpec((1, 1, ff, d), lambda i, be, nu: (layer, be[i], 0, 0))],
        out_specs=pl.BlockSpec((tm, half), lambda i, be, nu: (i, 0)),
        scratch_shapes=[pltpu.VMEM((d, ff), BF16), pltpu.VMEM((d, ff), BF16), pltpu.VMEM((ff, d), BF16)],
    )
    return pl.pallas_call(
        _experts_body,
        grid_spec=grid_spec,
        out_shape=jax.ShapeDtypeStruct((n_rows, half), jnp.int32),
        compiler_params=_cp(("arbitrary",)),
        name="moe_experts",
    )(block_exp, n_used, x_sorted, wg, wu, wd)


TOK_STEP = 1024
TOK_SUB = 256


def _stage_pos(pos_hbm, pos_smem, sem, step, n_tok):
    copies = [pltpu.make_async_copy(pos_hbm.at[pl.ds(k * n_tok + step * TOK_STEP, TOK_STEP)],
                                    pos_smem.at[pl.ds(k * TOK_STEP, TOK_STEP)], sem)
              for k in range(TOP_K)]
    for cp in copies:
        cp.start()
    for cp in copies:
        cp.wait()


def _dispatch_body(ends_ref, padded_ref, pos_hbm, h_hbm, xs_hbm, pos_smem, zero_scr, sem_pos, sem_fill, sem_rows,
                   *, n_tok, tme, sub):
    i, j = pl.program_id(0), pl.program_id(1)
    nj = pl.num_programs(1)
    step = i * nj + j
    slot = step % 2

    @pl.when(step == 0)
    def _():
        zero_scr[...] = jnp.zeros_like(zero_scr)

        def fill(e):
            start = pl.multiple_of(ends_ref[e] - tme, tme)
            return pltpu.make_async_copy(zero_scr, xs_hbm.at[pl.ds(start, tme)], sem_fill)

        for e in range(N_EXPERTS):
            @pl.when(padded_ref[e] > 0)
            def _():
                fill(e).start()
        for e in range(N_EXPERTS):
            @pl.when(padded_ref[e] > 0)
            def _():
                fill(e).wait()

        def tail(b):
            return pltpu.make_async_copy(zero_scr, xs_hbm.at[pl.ds(pl.multiple_of(b * tme, tme), tme)], sem_fill)

        n_used = ends_ref[N_EXPERTS - 1] // tme
        n_blocks = xs_hbm.shape[0] // tme
        lax.fori_loop(n_used, n_blocks, lambda b, c: (tail(b).start(), c)[1], 0)
        lax.fori_loop(n_used, n_blocks, lambda b, c: (tail(b).wait(), c)[1], 0)

    @pl.when(j == 0)
    def _():
        _stage_pos(pos_hbm, pos_smem, sem_pos, i, n_tok)

    def tok(t, carry):
        for k in range(TOP_K):
            p = pos_smem[k * TOK_STEP + j * sub + t]
            pltpu.make_async_copy(h_hbm.at[step * sub + t], xs_hbm.at[p], sem_rows.at[slot]).start()
        return carry

    lax.fori_loop(0, sub, tok, 0, unroll=4)

    def drain(which):
        for k in range(TOP_K):
            pltpu.make_async_copy(h_hbm.at[pl.ds(0, sub)], xs_hbm.at[pl.ds(0, sub)], sem_rows.at[which]).wait()

    @pl.when(step > 0)
    def _():
        drain(1 - slot)

    @pl.when(step == pl.num_programs(0) * nj - 1)
    def _():
        drain(slot)


def _dispatch(h, pos_flat, ends, padded, n_rows, tme):
    t, half = h.shape
    sub = min(TOK_SUB, t)
    grid_spec = pltpu.PrefetchScalarGridSpec(
        num_scalar_prefetch=2,
        grid=(t // TOK_STEP, TOK_STEP // sub),
        in_specs=[pl.BlockSpec(memory_space=pl.ANY),
                  pl.BlockSpec(memory_space=pl.ANY)],
        out_specs=pl.BlockSpec(memory_space=pl.ANY),
        scratch_shapes=[pltpu.SMEM((TOP_K * TOK_STEP,), jnp.int32),
                        pltpu.VMEM((tme, half), jnp.int32),
                        pltpu.SemaphoreType.DMA, pltpu.SemaphoreType.DMA, pltpu.SemaphoreType.DMA((2,))],
    )
    return pl.pallas_call(
        functools.partial(_dispatch_body, n_tok=t, tme=tme, sub=sub),
        grid_spec=grid_spec,
        out_shape=jax.ShapeDtypeStruct((n_rows, half), jnp.int32),
        compiler_params=_cp(("arbitrary", "arbitrary")),
        name="moe_dispatch",
    )(ends, padded, pos_flat, h)


def _combine_body(pos_hbm, ys_hbm, gw_ref, x_ref, sh_ref, mod_ref, fg_ref, o_ref, pos_smem, g_scr,
                  sem_pos, sem_rows, *, n_tok, final_norm):
    i, j = pl.program_id(0), pl.program_id(1)
    sub, d = x_ref.shape
    half = d // 2

    nj = pl.num_programs(1)
    slot = j % 2

    def gather(jj, into):
        def tok(t, carry):
            for k in range(TOP_K):
                p = pos_smem[k * TOK_STEP + jj * sub + t]
                pltpu.make_async_copy(ys_hbm.at[p], g_scr.at[into, k, t], sem_rows.at[into]).start()
            return carry

        lax.fori_loop(0, sub, tok, 0, unroll=4)

    @pl.when(j == 0)
    def _():
        _stage_pos(pos_hbm, pos_smem, sem_pos, i, n_tok)
        gather(j, slot)

    @pl.when(j + 1 < nj)
    def _():
        gather(j + 1, 1 - slot)

    for k in range(TOP_K):
        pltpu.make_async_copy(ys_hbm.at[pl.ds(0, sub)], g_scr.at[slot, k], sem_rows.at[slot]).wait()

    acc_lo = sh_ref[:, :half]
    acc_hi = sh_ref[:, half:]
    for k in range(TOP_K):
        lo, hi = _unpack_rows(g_scr[slot, k])
        w = gw_ref[:, k:k + 1]
        acc_lo = acc_lo + w * lo
        acc_hi = acc_hi + w * hi
    gate = mod_ref[0, G_F:G_F + 1, :]
    y_lo = x_ref[:, :half] + gate[:, :half] * acc_lo
    y_hi = x_ref[:, half:] + gate[:, half:] * acc_hi
    if final_norm:
        ms = (jnp.sum(y_lo * y_lo, axis=-1, keepdims=True) + jnp.sum(y_hi * y_hi, axis=-1, keepdims=True)) / d
        r = lax.rsqrt(ms + 1e-6)
        y_lo = y_lo * r * fg_ref[:, :half]
        y_hi = y_hi * r * fg_ref[:, half:]
    o_ref[:, :half] = y_lo
    o_ref[:, half:] = y_hi


def _combine(ys, pos_flat, gw, x2, shared, mod, final_g, seq, final_norm):
    t, d = x2.shape
    sub = min(TOK_SUB, t)
    nj = TOK_STEP // sub
    row = lambda i, j: (i * nj + j, 0)
    return pl.pallas_call(
        functools.partial(_combine_body, n_tok=t, final_norm=final_norm),
        grid=(t // TOK_STEP, nj),
        in_specs=[pl.BlockSpec(memory_space=pl.ANY),
                  pl.BlockSpec(memory_space=pl.ANY),
                  pl.BlockSpec((sub, TOP_K), row),
                  pl.BlockSpec((sub, d), row),
                  pl.BlockSpec((sub, d), row),
                  pl.BlockSpec((1, 6, d), lambda i, j: ((i * nj + j) * sub // seq, 0, 0)),
                  pl.BlockSpec((1, d), lambda i, j: (0, 0))],
        out_specs=pl.BlockSpec((sub, d), row),
        out_shape=jax.ShapeDtypeStruct((t, d), F32),
        scratch_shapes=[pltpu.SMEM((TOP_K * TOK_STEP,), jnp.int32),
                        pltpu.VMEM((2, TOP_K, sub, d // 2), jnp.int32),
                        pltpu.SemaphoreType.DMA, pltpu.SemaphoreType.DMA((2,))],
        compiler_params=_cp(("arbitrary", "arbitrary")),
        name="moe_combine",
    )(pos_flat, ys, gw, x2, shared, mod, final_g.reshape(1, d))


def _dispatch_plan(eidx_t, rank_t, counts, tm):
    n_assign = eidx_t.size
    padded = (counts + tm - 1) // tm * tm
    ends = jnp.cumsum(padded)
    pstart = ends - padded
    pos_t = rank_t
    for e in range(N_EXPERTS):
        pos_t = pos_t + jnp.where(eidx_t == e, pstart[e], 0)
    n_blocks = -(-n_assign // tm) + N_EXPERTS
    blk_start = jnp.arange(n_blocks, dtype=jnp.int32) * tm
    block_exp = jnp.minimum(jnp.sum((ends[None, :] <= blk_start[:, None]).astype(jnp.int32), axis=1),
                            N_EXPERTS - 1)
    n_used = (ends[-1] // tm).astype(jnp.int32).reshape(1)
    return pos_t.reshape(-1), ends, padded, block_exp, n_used, n_blocks


def _moe(x2, g, mod, w_router, e_bias, wg, wu, wd, layer, swg, swu, swd, final_g, seq, tm, tme, final_norm):
    h, logits_t, shared = _moe_pre(x2, g, mod, w_router, swg.astype(BF16), swu.astype(BF16),
                                   swd.astype(BF16), seq, tm)
    eidx_t, rank_t, gw_t, cnt = _route(logits_t, e_bias, tm)
    counts = cnt[:, 0].astype(jnp.int32)
    pos_flat, ends, padded, block_exp, n_used, n_blocks = _dispatch_plan(eidx_t, rank_t, counts, tme)
    x_sorted = _dispatch(h, pos_flat, ends, padded, n_blocks * tme, tme)
    y_sorted = _experts(x_sorted, block_exp, n_used, wg, wu, wd, layer, tme)
    return _combine(y_sorted, pos_flat, gw_t.T, x2, shared, mod, final_g, seq, final_norm)


def kernel(x, c, positions, mod_w, mod_b, norm_mix, norm_ffn, hyb_w_in, hyb_w_out, lru_conv_w, lru_conv_b, lru_wa, lru_ba, lru_wx, lru_bx, lru_lambda, cfm_w1, cfm_b1, cfm_dw, cfm_dwb, cfm_ln_g, cfm_ln_b, cfm_w2, cfm_b2, moe_router, moe_bias, moe_wg, moe_wu, moe_wd, sh_wg, sh_wu, sh_wd, final_norm):
    batch, seq, d = x.shape
    t = batch * seq
    n_sel = min(INDEX_TOPK, seq // 4)
    tm = min(512, seq)
    tq = min(256, seq)
    tme = MOE_ROWS

    mod_all = _modulation(c, mod_w, mod_b).reshape(mod_w.shape[0], batch, 6, d)
    tab_a = _rope_tables(positions, ROT_ATTN, HEAD_DIM)
    tab_i = _rope_tables(positions, ROT_IDX, IDX_DIM)
    x2 = x.reshape(t, d)
    zero_bias = jnp.zeros((d,), F32)

    mod = mod_all[0]
    w_in = hyb_w_in[0]
    w_main = w_in[:, :MAIN_COLS].astype(BF16)
    w_idx = jnp.pad(w_in[:, MAIN_COLS:], ((0, 0), (0, IDX_PAD - IDX_COLS))).astype(BF16)
    proj = _nm_matmul(x2, norm_mix[0], mod, w_main, seq, tm, MAIN_COLS, BF16, "hyb_in_main")
    idx = _nm_matmul(x2, norm_mix[0], mod, w_idx, seq, tm, IDX_PAD, F32, "hyb_in_idx")
    y_lru = _rg_lru(proj, lru_conv_w[0], lru_conv_b[0], lru_wa[0], lru_ba[0], lru_wx[0], lru_bx[0],
                    lru_lambda[0], batch, seq)
    y_att = _dsa(proj, idx, tab_a, tab_i, batch, seq, n_sel, tq)
    x2 = _proj_residual([y_lru, y_att], hyb_w_out[0].astype(BF16), zero_bias, x2, mod, G_M, seq, tm,
                        "hyb_out")
    x2 = _moe(x2, norm_ffn[0], mod, moe_router[0], moe_bias[0], moe_wg, moe_wu, moe_wd, 0,
              sh_wg[0], sh_wu[0], sh_wd[0], final_norm, seq, tm, tme, False)

    mod = mod_all[1]
    u = _nm_glu(x2, norm_mix[1], mod, cfm_w1[0].astype(BF16), cfm_b1[0], seq, tm, d)
    z = _cfm_conv(u, cfm_dw[0], cfm_dwb[0], cfm_ln_g[0], cfm_ln_b[0], seq, tm)
    x2 = _proj_residual([z], cfm_w2[0].astype(BF16), cfm_b2[0], x2, mod, G_M, seq, tm, "cfm_out")
    out = _moe(x2, norm_ffn[1], mod, moe_router[1], moe_bias[1], moe_wg, moe_wu, moe_wd, 1,
               sh_wg[1], sh_wu[1], sh_wd[1], final_norm, seq, tm, tme, True)
    return out.reshape(batch, seq, d)
```

```python
import functools

import jax
import jax.numpy as jnp
from jax import lax
from jax.experimental import pallas as pl
from jax.experimental.pallas import tpu as pltpu

F32 = jnp.float32
BF16 = jnp.bfloat16

LRU_WIDTH = 1024
LRU_BW = 128
LRU_CONV = 4
LRU_C = 8.0
N_HEADS = 8
N_KV_HEADS = 2
HEAD_DIM = 128
ATT_WIDTH = N_HEADS * HEAD_DIM
KV_WIDTH = N_KV_HEADS * HEAD_DIM
IDX_HEADS = 8
IDX_DIM = 64
INDEX_TOPK = 256
ROPE_THETA = 500000.0
ROT_ATTN = HEAD_DIM // 4
ROT_IDX = IDX_DIM // 4
CFM_CONV = 31
N_EXPERTS = 64
TOP_K = 8
N_GROUPS = 8
TOPK_GROUPS = 4
ROUTED_SCALE = 2.5

MAIN_COLS = 2 * LRU_WIDTH + ATT_WIDTH + 2 * KV_WIDTH
IDX_COLS = IDX_HEADS * IDX_DIM + IDX_DIM + IDX_HEADS
IDX_PAD = 640

MOE_ROWS = 512
LANES = 128
VMEM_LIMIT = 56 * 1024 * 1024
INT_MIN = -2147483648
NEG_BIG = -1e30

SH_M, SC_M, G_M, SH_F, SC_F, G_F = range(6)


def _cp(sem):
    return pltpu.CompilerParams(dimension_semantics=sem, vmem_limit_bytes=VMEM_LIMIT)


def _weight_spec(block, index_map, n_col_blocks):
    if n_col_blocks == 1:
        return pl.BlockSpec(block, index_map, pipeline_mode=pl.Buffered(1))
    return pl.BlockSpec(block, index_map)


def _norm_mod(x, g, mod_ref, sh_row, sc_row):
    ms = jnp.mean(x * x, axis=-1, keepdims=True)
    y = x * lax.rsqrt(ms + 1e-6) * g
    return y * (1.0 + mod_ref[0, sc_row:sc_row + 1, :]) + mod_ref[0, sh_row:sh_row + 1, :]


def _mod_body(c_ref, w_ref, b_ref, o_ref):
    c = c_ref[...]
    cond = c * jax.nn.sigmoid(c)
    o_ref[0] = jnp.dot(cond, w_ref[0], preferred_element_type=F32,
                       precision=lax.Precision.HIGHEST) + b_ref[0]


def _modulation(c, mod_w, mod_b):
    depth, d, n = mod_w.shape
    b = c.shape[0]
    tn = 512
    return pl.pallas_call(
        _mod_body,
        grid=(depth, n // tn),
        in_specs=[pl.BlockSpec((b, d), lambda l, j: (0, 0)),
                  pl.BlockSpec((1, d, tn), lambda l, j: (l, 0, j)),
                  pl.BlockSpec((1, 1, tn), lambda l, j: (l, 0, j))],
        out_specs=pl.BlockSpec((1, b, tn), lambda l, j: (l, 0, j)),
        out_shape=jax.ShapeDtypeStruct((depth, b, n), F32),
        compiler_params=_cp(("arbitrary", "arbitrary")),
        name="modulation",
    )(c, mod_w, mod_b.reshape(depth, 1, n))


def _nm_matmul_body(x_ref, g_ref, mod_ref, w_ref, o_ref, h_scr):
    @pl.when(pl.program_id(1) == 0)
    def _():
        h_scr[...] = _norm_mod(x_ref[...], g_ref[...], mod_ref, SH_M, SC_M).astype(BF16)

    o_ref[...] = jnp.dot(h_scr[...], w_ref[...], preferred_element_type=F32).astype(o_ref.dtype)


def _nm_matmul(x2, g, mod, w, seq, tm, tn, out_dtype, name):
    t, d = x2.shape
    n = w.shape[1]
    return pl.pallas_call(
        _nm_matmul_body,
        grid=(t // tm, n // tn),
        in_specs=[pl.BlockSpec((tm, d), lambda i, j: (i, 0)),
                  pl.BlockSpec((1, d), lambda i, j: (0, 0)),
                  pl.BlockSpec((1, 6, d), lambda i, j: (i * tm // seq, 0, 0)),
                  _weight_spec((d, tn), lambda i, j: (0, j), n // tn)],
        out_specs=pl.BlockSpec((tm, tn), lambda i, j: (i, j)),
        out_shape=jax.ShapeDtypeStruct((t, n), out_dtype),
        scratch_shapes=[pltpu.VMEM((tm, d), BF16)],
        compiler_params=_cp(("arbitrary", "arbitrary")),
        name=name,
    )(x2, g.reshape(1, d), mod, w)


def _nm_glu_body(x_ref, g_ref, mod_ref, wa_ref, wg_ref, ba_ref, bg_ref, o_ref, h_scr):
    @pl.when(pl.program_id(1) == 0)
    def _():
        h_scr[...] = _norm_mod(x_ref[...], g_ref[...], mod_ref, SH_M, SC_M).astype(BF16)

    h = h_scr[...]
    a = jnp.dot(h, wa_ref[...], preferred_element_type=F32) + ba_ref[...]
    gt = jnp.dot(h, wg_ref[...], preferred_element_type=F32) + bg_ref[...]
    o_ref[...] = (a * jax.nn.sigmoid(gt)).astype(o_ref.dtype)


def _nm_glu(x2, g, mod, w1, b1, seq, tm, tn):
    t, d = x2.shape
    n = w1.shape[1] // 2
    nj = n // tn
    return pl.pallas_call(
        _nm_glu_body,
        grid=(t // tm, nj),
        in_specs=[pl.BlockSpec((tm, d), lambda i, j: (i, 0)),
                  pl.BlockSpec((1, d), lambda i, j: (0, 0)),
                  pl.BlockSpec((1, 6, d), lambda i, j: (i * tm // seq, 0, 0)),
                  _weight_spec((d, tn), lambda i, j: (0, j), nj),
                  _weight_spec((d, tn), lambda i, j: (0, j + nj), nj),
                  pl.BlockSpec((1, tn), lambda i, j: (0, j)),
                  pl.BlockSpec((1, tn), lambda i, j: (0, j + nj))],
        out_specs=pl.BlockSpec((tm, tn), lambda i, j: (i, j)),
        out_shape=jax.ShapeDtypeStruct((t, n), BF16),
        scratch_shapes=[pltpu.VMEM((tm, d), BF16)],
        compiler_params=_cp(("arbitrary", "arbitrary")),
        name="cfm_in_glu",
    )(x2, g.reshape(1, d), mod, w1, w1, b1.reshape(1, 2 * n), b1.reshape(1, 2 * n))


def _proj_res_body(*refs, n_parts, gate_row):
    part_refs = refs[:n_parts]
    w_ref, b_ref, x_ref, mod_ref, o_ref = refs[n_parts:]
    acc = None
    off = 0
    for p in part_refs:
        k = p.shape[1]
        y = jnp.dot(p[...], w_ref[off:off + k, :], preferred_element_type=F32)
        acc = y if acc is None else acc + y
        off += k
    y = acc + b_ref[...]
    o_ref[...] = x_ref[...] + mod_ref[0, gate_row:gate_row + 1, :] * y


def _proj_residual(parts, w, bias, x2, mod, gate_row, seq, tm, name):
    t, d = x2.shape
    n = w.shape[1]
    in_specs = [pl.BlockSpec((tm, p.shape[1]), lambda i: (i, 0)) for p in parts]
    in_specs += [pl.BlockSpec(w.shape, lambda i: (0, 0)),
                 pl.BlockSpec((1, n), lambda i: (0, 0)),
                 pl.BlockSpec((tm, d), lambda i: (i, 0)),
                 pl.BlockSpec((1, 6, d), lambda i: (i * tm // seq, 0, 0))]
    return pl.pallas_call(
        functools.partial(_proj_res_body, n_parts=len(parts), gate_row=gate_row),
        grid=(t // tm,),
        in_specs=in_specs,
        out_specs=pl.BlockSpec((tm, n), lambda i: (i, 0)),
        out_shape=jax.ShapeDtypeStruct((t, n), F32),
        compiler_params=_cp(("arbitrary",)),
        name=name,
    )(*parts, w, bias.reshape(1, n), x2, mod)


def _shift_rows(x, d, fill, row):
    return jnp.where(row >= d, pltpu.roll(x, d, 0), fill)


def _lru_body(lx_ref, lg_ref, cw_ref, cb_ref, wa_ref, ba_ref, wx_ref, bx_ref, lam_ref, o_ref):
    s, cw = lx_ref.shape
    x = lx_ref[...].astype(F32)
    row = lax.broadcasted_iota(jnp.int32, (s, cw), 0)
    xc = x * cw_ref[LRU_CONV - 1:LRU_CONV, :] + cb_ref[...]
    for d in range(1, LRU_CONV):
        xc = xc + _shift_rows(x, d, 0.0, row) * cw_ref[LRU_CONV - 1 - d:LRU_CONV - d, :]
    xb = xc.astype(BF16)
    ga = jax.nn.sigmoid(jnp.dot(xb, wa_ref[0].astype(BF16), preferred_element_type=F32) + ba_ref[...])
    gx = jax.nn.sigmoid(jnp.dot(xb, wx_ref[0].astype(BF16), preferred_element_type=F32) + bx_ref[...])
    z = -lam_ref[...]
    softplus = jnp.maximum(z, 0.0) + jnp.log(1.0 + jnp.exp(-jnp.abs(z)))
    log_a = (-LRU_C) * ga * softplus
    a = jnp.exp(log_a)
    mult = jnp.sqrt(1.0 - a * a)
    mult = jnp.where(row == 0, 1.0, mult)
    b = gx * xc * mult
    d = 1
    while d < s:
        a_sh = _shift_rows(a, d, 1.0, row)
        b_sh = _shift_rows(b, d, 0.0, row)
        b = a * b_sh + b
        a = a * a_sh
        d *= 2
    lg = lg_ref[...].astype(F32)
    gelu = 0.5 * lg * (1.0 + jnp.tanh(0.7978845608028654 * (lg + 0.044715 * lg * lg * lg)))
    o_ref[...] = (b * gelu).astype(o_ref.dtype)


def _rg_lru(proj, conv_w, conv_b, wa, ba, wx, bx, lam, batch, seq):
    cw = LRU_BW
    nc = LRU_WIDTH // cw
    vec = lambda v: v.reshape(1, LRU_WIDTH)
    vspec = pl.BlockSpec((1, cw), lambda b, c: (0, c))
    return pl.pallas_call(
        _lru_body,
        grid=(batch, nc),
        in_specs=[pl.BlockSpec((seq, cw), lambda b, c: (b, c)),
                  pl.BlockSpec((seq, cw), lambda b, c: (b, c + nc)),
                  pl.BlockSpec((LRU_CONV, cw), lambda b, c: (0, c)),
                  vspec,
                  pl.BlockSpec((1, cw, cw), lambda b, c: (c, 0, 0)),
                  vspec,
                  pl.BlockSpec((1, cw, cw), lambda b, c: (c, 0, 0)),
                  vspec, vspec],
        out_specs=pl.BlockSpec((seq, cw), lambda b, c: (b, c)),
        out_shape=jax.ShapeDtypeStruct((batch * seq, LRU_WIDTH), BF16),
        compiler_params=_cp(("arbitrary", "arbitrary")),
        name="rg_lru",
    )(proj, proj, conv_w, vec(conv_b), wa, vec(ba), wx, vec(bx), vec(lam))


def _rope(x, c, s1, s2, half):
    w = x.shape[1]
    return x * c + pltpu.roll(x, w - half, 1) * s1 + pltpu.roll(x, half, 1) * s2


def _sort_key(x):
    bits = pltpu.bitcast(x, jnp.int32)
    return bits ^ (jnp.right_shift(bits, 31) & 0x7FFFFFFF)


def _dsa_tile(sk, q_ref, v_ref, iq_ref, ta_q_ref, ti_q_ref, o_ref, kr_scr, ikr_scr, key_scr, bias_scr, n_sel):
    qi = pl.program_id(1)
    tq = q_ref.shape[0]
    per_group = N_HEADS // N_KV_HEADS

    ci, si1, si2 = ti_q_ref[0], ti_q_ref[1], ti_q_ref[2]
    w_scale = (IDX_HEADS ** -0.5) * (IDX_DIM ** -0.5)
    iw = iq_ref[:, IDX_HEADS * IDX_DIM + IDX_DIM:IDX_HEADS * IDX_DIM + LANES] * w_scale
    ikr = ikr_scr[:sk, :]
    score = jnp.zeros((tq, sk), F32)
    for hp in range(IDX_HEADS // 2):
        pair = _rope(iq_ref[:, hp * LANES:(hp + 1) * LANES], ci, si1, si2, ROT_IDX // 2).astype(BF16)
        for sub in range(2):
            h = 2 * hp + sub
            qh = pair[:, sub * IDX_DIM:(sub + 1) * IDX_DIM]
            dots = lax.dot_general(qh, ikr, (((1,), (1,)), ((), ())), preferred_element_type=F32)
            score = score + jnp.maximum(dots, 0.0) * iw[:, h:h + 1]

    t_row = qi * tq + lax.broadcasted_iota(jnp.int32, (tq, sk), 0)
    col = lax.broadcasted_iota(jnp.int32, (tq, sk), 1)
    causal = col <= t_row
    key_scr[:, :sk] = jnp.where(causal, _sort_key(score), INT_MIN)

    kf = float(n_sel)

    def count_ge(cand):
        return jnp.sum((key_scr[:, :sk] >= cand).astype(F32), axis=1, keepdims=True)

    tau0 = jnp.where(count_ge(jnp.zeros((tq, 1), jnp.int32)) >= kf, 0, INT_MIN).astype(jnp.int32)

    def bis(i, tau):
        cand = tau + jnp.left_shift(jnp.int32(1), 30 - i)
        return jnp.where(count_ge(cand) >= kf, cand, tau)

    tau = lax.fori_loop(0, 31, bis, tau0)

    keys = key_scr[:, :sk]
    n_gt = jnp.sum((keys > tau).astype(F32), axis=1, keepdims=True)
    n_ge = jnp.sum((keys >= tau).astype(F32), axis=1, keepdims=True)
    need = kf - n_gt
    tie = keys == tau
    excess = jnp.logical_and(n_ge > kf, tau > INT_MIN)
    any_excess = jnp.max(excess.astype(F32)) > 0.0

    bias_scr[:, :sk] = jnp.where(jnp.logical_and(keys >= tau, causal), 0.0, NEG_BIG)

    @pl.when(any_excess)
    def _():
        nbits = sk.bit_length()

        def jb(i, jcur):
            cand = jcur + jnp.left_shift(jnp.int32(1), nbits - 1 - i)
            cnt = jnp.sum(jnp.logical_and(tie, col < cand).astype(F32), axis=1, keepdims=True)
            return jnp.where(cnt <= need, cand, jcur)

        jlim = lax.fori_loop(0, nbits, jb, jnp.zeros((tq, 1), jnp.int32))
        jlim = jnp.where(excess, jlim, sk)
        sel = jnp.logical_or(keys > tau, jnp.logical_and(tie, col < jlim))
        bias_scr[:, :sk] = jnp.where(jnp.logical_and(sel, causal), 0.0, NEG_BIG)

    ca, sa1, sa2 = ta_q_ref[0], ta_q_ref[1], ta_q_ref[2]
    scale = HEAD_DIM ** -0.5
    for h in range(N_HEADS):
        g = h // per_group
        qh = _rope(q_ref[:, h * HEAD_DIM:(h + 1) * HEAD_DIM].astype(F32), ca, sa1, sa2, ROT_ATTN // 2)
        qh = qh.astype(BF16)
        kg = kr_scr[:sk, g * HEAD_DIM:(g + 1) * HEAD_DIM]
        logits = lax.dot_general(qh, kg, (((1,), (1,)), ((), ())), preferred_element_type=F32)
        logits = logits * scale + bias_scr[:, :sk]
        m = jnp.max(logits, axis=1, keepdims=True)
        p = jnp.exp(logits - m)
        den = jnp.sum(p, axis=1, keepdims=True)
        vg = v_ref[:sk, g * HEAD_DIM:(g + 1) * HEAD_DIM]
        o = jnp.dot(p.astype(BF16), vg, preferred_element_type=F32) / den
        o_ref[:, h * HEAD_DIM:(h + 1) * HEAD_DIM] = o.astype(o_ref.dtype)


def _dsa_body(q_ref, k_ref, v_ref, iq_ref, ik_ref, ta_q_ref, ta_k_ref, ti_q_ref, ti_k_ref,
              o_ref, kr_scr, ikr_scr, key_scr, bias_scr, *, n_sel):
    qi = pl.program_id(1)
    s = k_ref.shape[0]

    @pl.when(qi == 0)
    def _():
        ca, sa1, sa2 = ta_k_ref[0], ta_k_ref[1], ta_k_ref[2]
        for g in range(N_KV_HEADS):
            kg = k_ref[:, g * HEAD_DIM:(g + 1) * HEAD_DIM].astype(F32)
            kr_scr[:, g * HEAD_DIM:(g + 1) * HEAD_DIM] = _rope(kg, ca, sa1, sa2, ROT_ATTN // 2).astype(BF16)
        ikp = _rope(ik_ref[...], ti_k_ref[0], ti_k_ref[1], ti_k_ref[2], ROT_IDX // 2)
        ikr_scr[...] = ikp[:, :IDX_DIM].astype(BF16)

    _dsa_tile(s, q_ref, v_ref, iq_ref, ta_q_ref, ti_q_ref, o_ref, kr_scr, ikr_scr, key_scr, bias_scr, n_sel)


def _dsa(proj, idx, tab_a, tab_i, batch, seq, n_sel, tq):
    nq = seq // tq
    q_col = 2 * LRU_WIDTH // ATT_WIDTH
    k_col = (2 * LRU_WIDTH + ATT_WIDTH) // KV_WIDTH
    ik_col = IDX_HEADS * IDX_DIM // LANES
    return pl.pallas_call(
        functools.partial(_dsa_body, n_sel=n_sel),
        grid=(batch, nq),
        in_specs=[pl.BlockSpec((tq, ATT_WIDTH), lambda b, i: (b * nq + i, q_col)),
                  pl.BlockSpec((seq, KV_WIDTH), lambda b, i: (b, k_col)),
                  pl.BlockSpec((seq, KV_WIDTH), lambda b, i: (b, k_col + 1)),
                  pl.BlockSpec((tq, IDX_PAD), lambda b, i: (b * nq + i, 0)),
                  pl.BlockSpec((seq, LANES), lambda b, i: (b, ik_col)),
                  pl.BlockSpec((3, tq, LANES), lambda b, i: (0, b * nq + i, 0)),
                  pl.BlockSpec((3, seq, LANES), lambda b, i: (0, b, 0)),
                  pl.BlockSpec((3, tq, LANES), lambda b, i: (0, b * nq + i, 0)),
                  pl.BlockSpec((3, seq, LANES), lambda b, i: (0, b, 0))],
        out_specs=pl.BlockSpec((tq, ATT_WIDTH), lambda b, i: (b * nq + i, 0)),
        out_shape=jax.ShapeDtypeStruct((batch * seq, ATT_WIDTH), BF16),
        scratch_shapes=[pltpu.VMEM((seq, KV_WIDTH), BF16),
                        pltpu.VMEM((seq, IDX_DIM), BF16),
                        pltpu.VMEM((tq, seq), jnp.int32),
                        pltpu.VMEM((tq, seq), F32)],
        compiler_params=_cp(("arbitrary", "arbitrary")),
        name="dsa_attention",
    )(proj, proj, proj, idx, idx, tab_a, tab_a, tab_i, tab_i)


def _rope_tables(positions, rot_dim, period):
    half = rot_dim // 2
    inv = jnp.power(jnp.float32(ROPE_THETA), -jnp.arange(0, rot_dim, 2, dtype=F32) / rot_dim)
    ang = positions.astype(F32).reshape(-1, 1) * inv
    cos, sin = jnp.cos(ang), jnp.sin(ang)
    t = cos.shape[0]
    ones = jnp.ones((t, period - rot_dim), F32)
    zeros = jnp.zeros((t, period - rot_dim), F32)
    zh = jnp.zeros((t, half), F32)
    c = jnp.concatenate([cos, cos, ones], axis=1)
    s1 = jnp.concatenate([-sin, zh, zeros], axis=1)
    s2 = jnp.concatenate([zh, sin, zeros], axis=1)
    reps = LANES // period
    return jnp.stack([jnp.tile(c, (1, reps)), jnp.tile(s1, (1, reps)), jnp.tile(s2, (1, reps))])


HALO = 32


def _cfm_conv_body(u_ref, halo_ref, dw_ref, dwb_ref, g_ref, b_ref, o_ref, ext_scr, acc_scr, *, seq, rc, cc):
    tm, c = u_ref.shape
    i = pl.program_id(0)
    at_start = (i * tm) % seq == 0
    halo = jnp.where(at_start, 0.0, halo_ref[...].astype(F32))
    ext_scr[0:HALO, :] = halo
    ext_scr[HALO:, :] = u_ref[...].astype(F32)
    base = HALO - (CFM_CONV - 1)

    def chunk(r, carry):
        r0 = pl.multiple_of(r * rc, rc)
        for c0 in range(0, c, cc):
            win = ext_scr[pl.ds(r0, rc + HALO), c0:c0 + cc]
            acc = jnp.zeros((rc, cc), F32) + dwb_ref[:, c0:c0 + cc]
            for sub in range(8):
                rolled = win if sub == 0 else pltpu.roll(win, rc + HALO - sub, 0)
                for j in range(CFM_CONV):
                    off = base + j
                    if off % 8 == sub:
                        a0 = off - sub
                        acc = acc + rolled[a0:a0 + rc, :] * dw_ref[j:j + 1, c0:c0 + cc]
            acc_scr[pl.ds(r0, rc), c0:c0 + cc] = acc
        return carry

    lax.fori_loop(0, tm // rc, chunk, 0)
    y = acc_scr[...]
    mu = jnp.mean(y, axis=-1, keepdims=True)
    yc = y - mu
    var = jnp.mean(yc * yc, axis=-1, keepdims=True)
    z = yc * lax.rsqrt(var + 1e-5) * g_ref[...] + b_ref[...]
    o_ref[...] = (z * jax.nn.sigmoid(z)).astype(o_ref.dtype)


def _cfm_conv(u, dw, dwb, ln_g, ln_b, seq, tm):
    t, c = u.shape
    hb = tm // HALO
    vec = lambda v: v.reshape(1, c)
    vspec = pl.BlockSpec((1, c), lambda i: (0, 0))
    return pl.pallas_call(
        functools.partial(_cfm_conv_body, seq=seq, rc=32, cc=256),
        grid=(t // tm,),
        in_specs=[pl.BlockSpec((tm, c), lambda i: (i, 0)),
                  pl.BlockSpec((HALO, c), lambda i: (jnp.maximum(i * hb - 1, 0), 0)),
                  pl.BlockSpec((CFM_CONV, c), lambda i: (0, 0)),
                  vspec, vspec, vspec],
        out_specs=pl.BlockSpec((tm, c), lambda i: (i, 0)),
        out_shape=jax.ShapeDtypeStruct((t, c), BF16),
        scratch_shapes=[pltpu.VMEM((tm + HALO, c), F32), pltpu.VMEM((tm, c), F32)],
        compiler_params=_cp(("arbitrary",)),
        name="cfm_conv_ln",
    )(u, u, dw, vec(dwb), vec(ln_g), vec(ln_b))


HI_MASK = -65536


def _pack_rows(y):
    c = y.shape[1] // 2
    lo = pltpu.bitcast(y[:, :c].astype(BF16).astype(F32), jnp.int32)
    hi = pltpu.bitcast(y[:, c:].astype(BF16).astype(F32), jnp.int32)
    return (hi & HI_MASK) | (jnp.right_shift(lo, 16) & 0xFFFF)


def _unpack_rows(w):
    lo = pltpu.bitcast(jnp.left_shift(w, 16), F32)
    hi = pltpu.bitcast(w & HI_MASK, F32)
    return lo, hi


def _moe_pre_body(x_ref, g_ref, mod_ref, wr_ref, swg_ref, swu_ref, swd_ref, h_ref, lg_ref, sh_ref):
    h = _norm_mod(x_ref[...], g_ref[...], mod_ref, SH_F, SC_F)
    hb = h.astype(BF16)
    h_ref[...] = _pack_rows(h)
    lg_ref[...] = jnp.dot(h, wr_ref[...], preferred_element_type=F32, precision=lax.Precision.HIGHEST).T
    gt = jnp.dot(hb, swg_ref[...], preferred_element_type=F32)
    up = jnp.dot(hb, swu_ref[...], preferred_element_type=F32)
    mid = (gt * jax.nn.sigmoid(gt) * up).astype(BF16)
    sh_ref[...] = jnp.dot(mid, swd_ref[...], preferred_element_type=F32)


def _moe_pre(x2, g, mod, w_router, swg, swu, swd, seq, tm):
    t, d = x2.shape
    e = w_router.shape[1]
    full = lambda a: pl.BlockSpec(a.shape, lambda i: (0, 0))
    return pl.pallas_call(
        _moe_pre_body,
        grid=(t // tm,),
        in_specs=[pl.BlockSpec((tm, d), lambda i: (i, 0)),
                  pl.BlockSpec((1, d), lambda i: (0, 0)),
                  pl.BlockSpec((1, 6, d), lambda i: (i * tm // seq, 0, 0)),
                  full(w_router), full(swg), full(swu), full(swd)],
        out_specs=[pl.BlockSpec((tm, d // 2), lambda i: (i, 0)),
                   pl.BlockSpec((e, tm), lambda i: (0, i)),
                   pl.BlockSpec((tm, d), lambda i: (i, 0))],
        out_shape=[jax.ShapeDtypeStruct((t, d // 2), jnp.int32),
                   jax.ShapeDtypeStruct((e, t), F32),
                   jax.ShapeDtypeStruct((t, d), F32)],
        compiler_params=_cp(("arbitrary",)),
        name="moe_pre",
    )(x2, g.reshape(1, d), mod, w_router, swg, swu, swd)


def _first_max(v, ids, sentinel):
    m = jnp.max(v, axis=0, keepdims=True)
    first = jnp.min(jnp.where(v == m, ids, sentinel), axis=0, keepdims=True)
    return m, first


def _route_body(lg_ref, bias_ref, eidx_ref, rank_ref, gw_ref, cnt_ref):
    e, tm = lg_ref.shape
    gsz = e // N_GROUPS
    neg = -jnp.inf

    @pl.when(pl.program_id(0) == 0)
    def _():
        cnt_ref[...] = jnp.zeros_like(cnt_ref)

    scores = jax.nn.sigmoid(lg_ref[...])
    choice = scores + bias_ref[:, 0:1]
    row = lax.broadcasted_iota(jnp.int32, (e, tm), 0)
    sub = lax.broadcasted_iota(jnp.int32, (gsz, tm), 0)

    parts = []
    for g in range(N_GROUPS):
        vg = choice[g * gsz:(g + 1) * gsz, :]
        m1, f1 = _first_max(vg, sub, gsz)
        m2 = jnp.max(jnp.where(sub == f1, neg, vg), axis=0, keepdims=True)
        parts.append(jnp.broadcast_to(m1 + m2, (gsz, tm)))
    gscore = jnp.concatenate(parts, axis=0)

    gid = jnp.right_shift(row, gsz.bit_length() - 1)
    v = jnp.full((e, tm), neg, F32)
    for _ in range(TOPK_GROUPS):
        _, fg = _first_max(gscore, gid, N_GROUPS)
        hit = gid == fg
        v = jnp.where(hit, choice, v)
        gscore = jnp.where(hit, neg, gscore)

    picks = []
    self32 = jnp.zeros((e, tm), F32)
    for _ in range(TOP_K):
        _, fe = _first_max(v, row, e)
        hit = row == fe
        picks.append(fe)
        self32 = jnp.where(hit, 1.0, self32)
        v = jnp.where(hit, neg, v)

    picked = self32 * scores
    gwd = picked / jnp.sum(picked, axis=0, keepdims=True) * ROUTED_SCALE

    before = (lax.broadcasted_iota(jnp.int32, (tm, tm), 0)
              < lax.broadcasted_iota(jnp.int32, (tm, tm), 1)).astype(BF16)
    rank = jnp.dot(self32.astype(BF16), before, preferred_element_type=F32) + cnt_ref[:, 0:1]
    cnt_ref[...] = cnt_ref[...] + jnp.sum(self32, axis=1, keepdims=True)

    for k in range(TOP_K):
        hit = row == picks[k]
        eidx_ref[k:k + 1, :] = picks[k]
        rank_ref[k:k + 1, :] = jnp.sum(jnp.where(hit, rank, 0.0), axis=0, keepdims=True).astype(jnp.int32)
        gw_ref[k:k + 1, :] = jnp.sum(jnp.where(hit, gwd, 0.0), axis=0, keepdims=True)


def _route(logits_t, e_bias, tm):
    e, t = logits_t.shape
    kspec = pl.BlockSpec((TOP_K, tm), lambda i: (0, i))
    return pl.pallas_call(
        _route_body,
        grid=(t // tm,),
        in_specs=[pl.BlockSpec((e, tm), lambda i: (0, i)),
                  pl.BlockSpec((e, LANES), lambda i: (0, 0))],
        out_specs=[kspec, kspec, kspec, pl.BlockSpec((e, LANES), lambda i: (0, 0))],
        out_shape=[jax.ShapeDtypeStruct((TOP_K, t), jnp.int32),
                   jax.ShapeDtypeStruct((TOP_K, t), jnp.int32),
                   jax.ShapeDtypeStruct((TOP_K, t), F32),
                   jax.ShapeDtypeStruct((e, LANES), F32)],
        compiler_params=_cp(("arbitrary",)),
        name="moe_route",
    )(logits_t, jnp.broadcast_to(e_bias.astype(F32).reshape(e, 1), (e, LANES)))


def _experts_body(be_ref, nu_ref, x_ref, wg_ref, wu_ref, wd_ref, o_ref, wg_scr, wu_scr, wd_scr):
    i = pl.program_id(0)
    half = x_ref.shape[1]

    @pl.when(i >= nu_ref[0])
    def _():
        o_ref[...] = jnp.zeros_like(o_ref)

    @pl.when(i < nu_ref[0])
    def _():
        prev = be_ref[jnp.maximum(i - 1, 0)]
        changed = jnp.logical_or(i == 0, be_ref[i] != prev)

        @pl.when(changed)
        def _():
            wg_scr[...] = wg_ref[0, 0].astype(BF16)
            wu_scr[...] = wu_ref[0, 0].astype(BF16)
            wd_scr[...] = wd_ref[0, 0].astype(BF16)

        lo, hi = _unpack_rows(x_ref[...])
        lo = lo.astype(BF16)
        hi = hi.astype(BF16)
        gt = (jnp.dot(lo, wg_scr[:half, :], preferred_element_type=F32)
              + jnp.dot(hi, wg_scr[half:, :], preferred_element_type=F32))
        up = (jnp.dot(lo, wu_scr[:half, :], preferred_element_type=F32)
              + jnp.dot(hi, wu_scr[half:, :], preferred_element_type=F32))
        mid = (gt * jax.nn.sigmoid(gt) * up).astype(BF16)
        o_ref[...] = _pack_rows(jnp.dot(mid, wd_scr[...], preferred_element_type=F32))


def _experts(x_sorted, block_exp, n_used, wg, wu, wd, layer, tm):
    n_rows, half = x_sorted.shape
    d, ff = wg.shape[2], wg.shape[3]
    n_blocks = n_rows // tm
    row_map = lambda i, be, nu: (jnp.minimum(i, nu[0] - 1), 0)
    grid_spec = pltpu.PrefetchScalarGridSpec(
        num_scalar_prefetch=2,
        grid=(n_blocks,),
        in_specs=[pl.BlockSpec((tm, half), row_map),
                  pl.BlockSpec((1, 1, d, ff), lambda i, be, nu: (layer, be[i], 0, 0)),
                  pl.BlockSpec((1, 1, d, ff), lambda i, be, nu: (layer, be[i], 0, 0)),
                  pl.BlockSpec((1, 1, ff, d), lambda i, be, nu: (layer, be[i], 0, 0))],
        out_specs=pl.BlockSpec((tm, half), lambda i, be, nu: (i, 0)),
        scratch_shapes=[pltpu.VMEM((d, ff), BF16), pltpu.VMEM((d, ff), BF16), pltpu.VMEM((ff, d), BF16)],
    )
    return pl.pallas_call(
        _experts_body,
        grid_spec=grid_spec,
        out_shape=jax.ShapeDtypeStruct((n_rows, half), jnp.int32),
        compiler_params=_cp(("arbitrary",)),
        name="moe_experts",
    )(block_exp, n_used, x_sorted, wg, wu, wd)


TOK_STEP = 1024
TOK_SUB = 256


def _stage_pos(pos_hbm, pos_smem, sem, step, n_tok):
    copies = [pltpu.make_async_copy(pos_hbm.at[pl.ds(k * n_tok + step * TOK_STEP, TOK_STEP)],
                                    pos_smem.at[pl.ds(k * TOK_STEP, TOK_STEP)], sem)
              for k in range(TOP_K)]
    for cp in copies:
        cp.start()
    for cp in copies:
        cp.wait()


def _dispatch_body(ends_ref, padded_ref, pos_hbm, h_ref, xs_hbm, pos_smem, zero_scr, sem_pos, sem_fill, sem_rows,
                   *, n_tok, tme):
    i, j = pl.program_id(0), pl.program_id(1)
    sub = h_ref.shape[0]

    @pl.when(jnp.logical_and(i == 0, j == 0))
    def _():
        zero_scr[...] = jnp.zeros_like(zero_scr)

        def fill(e):
            start = pl.multiple_of(ends_ref[e] - tme, tme)
            return pltpu.make_async_copy(zero_scr, xs_hbm.at[pl.ds(start, tme)], sem_fill)

        for e in range(N_EXPERTS):
            @pl.when(padded_ref[e] > 0)
            def _():
                fill(e).start()
        for e in range(N_EXPERTS):
            @pl.when(padded_ref[e] > 0)
            def _():
                fill(e).wait()

        def tail(b):
            return pltpu.make_async_copy(zero_scr, xs_hbm.at[pl.ds(pl.multiple_of(b * tme, tme), tme)], sem_fill)

        n_used = ends_ref[N_EXPERTS - 1] // tme
        n_blocks = xs_hbm.shape[0] // tme
        lax.fori_loop(n_used, n_blocks, lambda b, c: (tail(b).start(), c)[1], 0)
        lax.fori_loop(n_used, n_blocks, lambda b, c: (tail(b).wait(), c)[1], 0)

    @pl.when(j == 0)
    def _():
        _stage_pos(pos_hbm, pos_smem, sem_pos, i, n_tok)

    def tok(t, carry):
        for k in range(TOP_K):
            p = pos_smem[k * TOK_STEP + j * sub + t]
            pltpu.make_async_copy(h_ref.at[t], xs_hbm.at[p], sem_rows).start()
        return carry

    lax.fori_loop(0, sub, tok, 0, unroll=4)
    for k in range(TOP_K):
        pltpu.make_async_copy(h_ref, xs_hbm.at[pl.ds(0, sub)], sem_rows).wait()


def _dispatch(h, pos_flat, ends, padded, n_rows, tme):
    t, half = h.shape
    sub = min(TOK_SUB, t)
    grid_spec = pltpu.PrefetchScalarGridSpec(
        num_scalar_prefetch=2,
        grid=(t // TOK_STEP, TOK_STEP // sub),
        in_specs=[pl.BlockSpec(memory_space=pl.ANY),
                  pl.BlockSpec((sub, half), lambda i, j, en, pd: (i * (TOK_STEP // sub) + j, 0))],
        out_specs=pl.BlockSpec(memory_space=pl.ANY),
        scratch_shapes=[pltpu.SMEM((TOP_K * TOK_STEP,), jnp.int32),
                        pltpu.VMEM((tme, half), jnp.int32),
                        pltpu.SemaphoreType.DMA, pltpu.SemaphoreType.DMA, pltpu.SemaphoreType.DMA],
    )
    return pl.pallas_call(
        functools.partial(_dispatch_body, n_tok=t, tme=tme),
        grid_spec=grid_spec,
        out_shape=jax.ShapeDtypeStruct((n_rows, half), jnp.int32),
        compiler_params=_cp(("arbitrary", "arbitrary")),
        name="moe_dispatch",
    )(ends, padded, pos_flat, h)


def _combine_body(pos_hbm, ys_hbm, gw_ref, x_ref, sh_ref, mod_ref, fg_ref, o_ref, pos_smem, g_scr,
                  sem_pos, sem_rows, *, n_tok, final_norm):
    i, j = pl.program_id(0), pl.program_id(1)
    sub, d = x_ref.shape
    half = d // 2

    @pl.when(j == 0)
    def _():
        _stage_pos(pos_hbm, pos_smem, sem_pos, i, n_tok)

    def tok(t, carry):
        for k in range(TOP_K):
            p = pos_smem[k * TOK_STEP + j * sub + t]
            pltpu.make_async_copy(ys_hbm.at[p], g_scr.at[k, t], sem_rows).start()
        return carry

    lax.fori_loop(0, sub, tok, 0, unroll=4)
    for k in range(TOP_K):
        pltpu.make_async_copy(ys_hbm.at[pl.ds(0, sub)], g_scr.at[k], sem_rows).wait()

    acc_lo = sh_ref[:, :half]
    acc_hi = sh_ref[:, half:]
    for k in range(TOP_K):
        lo, hi = _unpack_rows(g_scr[k])
        w = gw_ref[:, k:k + 1]
        acc_lo = acc_lo + w * lo
        acc_hi = acc_hi + w * hi
    gate = mod_ref[0, G_F:G_F + 1, :]
    y_lo = x_ref[:, :half] + gate[:, :half] * acc_lo
    y_hi = x_ref[:, half:] + gate[:, half:] * acc_hi
    if final_norm:
        ms = (jnp.sum(y_lo * y_lo, axis=-1, keepdims=True) + jnp.sum(y_hi * y_hi, axis=-1, keepdims=True)) / d
        r = lax.rsqrt(ms + 1e-6)
        y_lo = y_lo * r * fg_ref[:, :half]
        y_hi = y_hi * r * fg_ref[:, half:]
    o_ref[:, :half] = y_lo
    o_ref[:, half:] = y_hi


def _combine(ys, pos_flat, gw, x2, shared, mod, final_g, seq, final_norm):
    t, d = x2.shape
    sub = min(TOK_SUB, t)
    nj = TOK_STEP // sub
    row = lambda i, j: (i * nj + j, 0)
    return pl.pallas_call(
        functools.partial(_combine_body, n_tok=t, final_norm=final_norm),
        grid=(t // TOK_STEP, nj),
        in_specs=[pl.BlockSpec(memory_space=pl.ANY),
                  pl.BlockSpec(memory_space=pl.ANY),
                  pl.BlockSpec((sub, TOP_K), row),
                  pl.BlockSpec((sub, d), row),
                  pl.BlockSpec((sub, d), row),
                  pl.BlockSpec((1, 6, d), lambda i, j: ((i * nj + j) * sub // seq, 0, 0)),
                  pl.BlockSpec((1, d), lambda i, j: (0, 0))],
        out_specs=pl.BlockSpec((sub, d), row),
        out_shape=jax.ShapeDtypeStruct((t, d), F32),
        scratch_shapes=[pltpu.SMEM((TOP_K * TOK_STEP,), jnp.int32),
                        pltpu.VMEM((TOP_K, sub, d // 2), jnp.int32),
                        pltpu.SemaphoreType.DMA, pltpu.SemaphoreType.DMA],
        compiler_params=_cp(("arbitrary", "arbitrary")),
        name="moe_combine",
    )(pos_flat, ys, gw, x2, shared, mod, final_g.reshape(1, d))


def _dispatch_plan(eidx_t, rank_t, counts, tm):
    n_assign = eidx_t.size
    padded = (counts + tm - 1) // tm * tm
    ends = jnp.cumsum(padded)
    pstart = ends - padded
    pos_t = rank_t
    for e in range(N_EXPERTS):
        pos_t = pos_t + jnp.where(eidx_t == e, pstart[e], 0)
    n_blocks = -(-n_assign // tm) + N_EXPERTS
    blk_start = jnp.arange(n_blocks, dtype=jnp.int32) * tm
    block_exp = jnp.minimum(jnp.sum((ends[None, :] <= blk_start[:, None]).astype(jnp.int32), axis=1),
                            N_EXPERTS - 1)
    n_used = (ends[-1] // tm).astype(jnp.int32).reshape(1)
    return pos_t.reshape(-1), ends, padded, block_exp, n_used, n_blocks


def _moe(x2, g, mod, w_router, e_bias, wg, wu, wd, layer, swg, swu, swd, final_g, seq, tm, tme, final_norm):
    h, logits_t, shared = _moe_pre(x2, g, mod, w_router, swg.astype(BF16), swu.astype(BF16),
                                   swd.astype(BF16), seq, tm)
    eidx_t, rank_t, gw_t, cnt = _route(logits_t, e_bias, tm)
    counts = cnt[:, 0].astype(jnp.int32)
    pos_flat, ends, padded, block_exp, n_used, n_blocks = _dispatch_plan(eidx_t, rank_t, counts, tme)
    x_sorted = _dispatch(h, pos_flat, ends, padded, n_blocks * tme, tme)
    y_sorted = _experts(x_sorted, block_exp, n_used, wg, wu, wd, layer, tme)
    return _combine(y_sorted, pos_flat, gw_t.T, x2, shared, mod, final_g, seq, final_norm)


def kernel(x, c, positions, mod_w, mod_b, norm_mix, norm_ffn, hyb_w_in, hyb_w_out, lru_conv_w, lru_conv_b, lru_wa, lru_ba, lru_wx, lru_bx, lru_lambda, cfm_w1, cfm_b1, cfm_dw, cfm_dwb, cfm_ln_g, cfm_ln_b, cfm_w2, cfm_b2, moe_router, moe_bias, moe_wg, moe_wu, moe_wd, sh_wg, sh_wu, sh_wd, final_norm):
    batch, seq, d = x.shape
    t = batch * seq
    n_sel = min(INDEX_TOPK, seq // 4)
    tm = min(512, seq)
    tq = min(256, seq)
    tme = MOE_ROWS

    mod_all = _modulation(c, mod_w, mod_b).reshape(mod_w.shape[0], batch, 6, d)
    tab_a = _rope_tables(positions, ROT_ATTN, HEAD_DIM)
    tab_i = _rope_tables(positions, ROT_IDX, IDX_DIM)
    x2 = x.reshape(t, d)
    zero_bias = jnp.zeros((d,), F32)

    mod = mod_all[0]
    w_in = hyb_w_in[0]
    w_main = w_in[:, :MAIN_COLS].astype(BF16)
    w_idx = jnp.pad(w_in[:, MAIN_COLS:], ((0, 0), (0, IDX_PAD - IDX_COLS))).astype(BF16)
    proj = _nm_matmul(x2, norm_mix[0], mod, w_main, seq, tm, MAIN_COLS, BF16, "hyb_in_main")
    idx = _nm_matmul(x2, norm_mix[0], mod, w_idx, seq, tm, IDX_PAD, F32, "hyb_in_idx")
    y_lru = _rg_lru(proj, lru_conv_w[0], lru_conv_b[0], lru_wa[0], lru_ba[0], lru_wx[0], lru_bx[0],
                    lru_lambda[0], batch, seq)
    y_att = _dsa(proj, idx, tab_a, tab_i, batch, seq, n_sel, tq)
    x2 = _proj_residual([y_lru, y_att], hyb_w_out[0].astype(BF16), zero_bias, x2, mod, G_M, seq, tm,
                        "hyb_out")
    x2 = _moe(x2, norm_ffn[0], mod, moe_router[0], moe_bias[0], moe_wg, moe_wu, moe_wd, 0,
              sh_wg[0], sh_wu[0], sh_wd[0], final_norm, seq, tm, tme, False)

    mod = mod_all[1]
    u = _nm_glu(x2, norm_mix[1], mod, cfm_w1[0].astype(BF16), cfm_b1[0], seq, tm, d)
    z = _cfm_conv(u, cfm_dw[0], cfm_dwb[0], cfm_ln_g[0], cfm_ln_b[0], seq, tm)
    x2 = _proj_residual([z], cfm_w2[0].astype(BF16), cfm_b2[0], x2, mod, G_M, seq, tm, "cfm_out")
    out = _moe(x2, norm_ffn[1], mod, moe_router[1], moe_bias[1], moe_wg, moe_wu, moe_wd, 1,
               sh_wg[1], sh_wu[1], sh_wd[1], final_norm, seq, tm, tme, True)
    return out.reshape(batch, seq, d)
```

```python
import functools

import jax
import jax.numpy as jnp
from jax import lax
from jax.experimental import pallas as pl
from jax.experimental.pallas import tpu as pltpu

F32 = jnp.float32
BF16 = jnp.bfloat16

LRU_WIDTH = 1024
LRU_BW = 128
LRU_CONV = 4
LRU_C = 8.0
N_HEADS = 8
N_KV_HEADS = 2
HEAD_DIM = 128
ATT_WIDTH = N_HEADS * HEAD_DIM
KV_WIDTH = N_KV_HEADS * HEAD_DIM
IDX_HEADS = 8
IDX_DIM = 64
INDEX_TOPK = 256
ROPE_THETA = 500000.0
ROT_ATTN = HEAD_DIM // 4
ROT_IDX = IDX_DIM // 4
CFM_CONV = 31
N_EXPERTS = 64
TOP_K = 8
N_GROUPS = 8
TOPK_GROUPS = 4
ROUTED_SCALE = 2.5

MAIN_COLS = 2 * LRU_WIDTH + ATT_WIDTH + 2 * KV_WIDTH
IDX_COLS = IDX_HEADS * IDX_DIM + IDX_DIM + IDX_HEADS
IDX_PAD = 640

MOE_ROWS = 512
LANES = 128
VMEM_LIMIT = 56 * 1024 * 1024
INT_MIN = -2147483648
NEG_BIG = -1e30

SH_M, SC_M, G_M, SH_F, SC_F, G_F = range(6)


def _cp(sem):
    return pltpu.CompilerParams(dimension_semantics=sem, vmem_limit_bytes=VMEM_LIMIT)


def _weight_spec(block, index_map, n_col_blocks):
    if n_col_blocks == 1:
        return pl.BlockSpec(block, index_map, pipeline_mode=pl.Buffered(1))
    return pl.BlockSpec(block, index_map)


def _norm_mod(x, g, mod_ref, sh_row, sc_row):
    ms = jnp.mean(x * x, axis=-1, keepdims=True)
    y = x * lax.rsqrt(ms + 1e-6) * g
    return y * (1.0 + mod_ref[0, sc_row:sc_row + 1, :]) + mod_ref[0, sh_row:sh_row + 1, :]


def _mod_body(c_ref, w_ref, b_ref, o_ref):
    c = c_ref[...]
    cond = c * jax.nn.sigmoid(c)
    o_ref[0] = jnp.dot(cond, w_ref[0], preferred_element_type=F32,
                       precision=lax.Precision.HIGHEST) + b_ref[0]


def _modulation(c, mod_w, mod_b):
    depth, d, n = mod_w.shape
    b = c.shape[0]
    tn = 512
    return pl.pallas_call(
        _mod_body,
        grid=(depth, n // tn),
        in_specs=[pl.BlockSpec((b, d), lambda l, j: (0, 0)),
                  pl.BlockSpec((1, d, tn), lambda l, j: (l, 0, j)),
                  pl.BlockSpec((1, 1, tn), lambda l, j: (l, 0, j))],
        out_specs=pl.BlockSpec((1, b, tn), lambda l, j: (l, 0, j)),
        out_shape=jax.ShapeDtypeStruct((depth, b, n), F32),
        compiler_params=_cp(("arbitrary", "arbitrary")),
        name="modulation",
    )(c, mod_w, mod_b.reshape(depth, 1, n))


def _nm_matmul_body(x_ref, g_ref, mod_ref, w_ref, o_ref, h_scr):
    @pl.when(pl.program_id(1) == 0)
    def _():
        h_scr[...] = _norm_mod(x_ref[...], g_ref[...], mod_ref, SH_M, SC_M).astype(BF16)

    o_ref[...] = jnp.dot(h_scr[...], w_ref[...], preferred_element_type=F32).astype(o_ref.dtype)


def _nm_matmul(x2, g, mod, w, seq, tm, tn, out_dtype, name):
    t, d = x2.shape
    n = w.shape[1]
    return pl.pallas_call(
        _nm_matmul_body,
        grid=(t // tm, n // tn),
        in_specs=[pl.BlockSpec((tm, d), lambda i, j: (i, 0)),
                  pl.BlockSpec((1, d), lambda i, j: (0, 0)),
                  pl.BlockSpec((1, 6, d), lambda i, j: (i * tm // seq, 0, 0)),
                  _weight_spec((d, tn), lambda i, j: (0, j), n // tn)],
        out_specs=pl.BlockSpec((tm, tn), lambda i, j: (i, j)),
        out_shape=jax.ShapeDtypeStruct((t, n), out_dtype),
        scratch_shapes=[pltpu.VMEM((tm, d), BF16)],
        compiler_params=_cp(("arbitrary", "arbitrary")),
        name=name,
    )(x2, g.reshape(1, d), mod, w)


def _nm_glu_body(x_ref, g_ref, mod_ref, wa_ref, wg_ref, ba_ref, bg_ref, o_ref, h_scr):
    @pl.when(pl.program_id(1) == 0)
    def _():
        h_scr[...] = _norm_mod(x_ref[...], g_ref[...], mod_ref, SH_M, SC_M).astype(BF16)

    h = h_scr[...]
    a = jnp.dot(h, wa_ref[...], preferred_element_type=F32) + ba_ref[...]
    gt = jnp.dot(h, wg_ref[...], preferred_element_type=F32) + bg_ref[...]
    o_ref[...] = (a * jax.nn.sigmoid(gt)).astype(o_ref.dtype)


def _nm_glu(x2, g, mod, w1, b1, seq, tm, tn):
    t, d = x2.shape
    n = w1.shape[1] // 2
    nj = n // tn
    return pl.pallas_call(
        _nm_glu_body,
        grid=(t // tm, nj),
        in_specs=[pl.BlockSpec((tm, d), lambda i, j: (i, 0)),
                  pl.BlockSpec((1, d), lambda i, j: (0, 0)),
                  pl.BlockSpec((1, 6, d), lambda i, j: (i * tm // seq, 0, 0)),
                  _weight_spec((d, tn), lambda i, j: (0, j), nj),
                  _weight_spec((d, tn), lambda i, j: (0, j + nj), nj),
                  pl.BlockSpec((1, tn), lambda i, j: (0, j)),
                  pl.BlockSpec((1, tn), lambda i, j: (0, j + nj))],
        out_specs=pl.BlockSpec((tm, tn), lambda i, j: (i, j)),
        out_shape=jax.ShapeDtypeStruct((t, n), BF16),
        scratch_shapes=[pltpu.VMEM((tm, d), BF16)],
        compiler_params=_cp(("arbitrary", "arbitrary")),
        name="cfm_in_glu",
    )(x2, g.reshape(1, d), mod, w1, w1, b1.reshape(1, 2 * n), b1.reshape(1, 2 * n))


def _proj_res_body(*refs, n_parts, gate_row):
    part_refs = refs[:n_parts]
    w_ref, b_ref, x_ref, mod_ref, o_ref = refs[n_parts:]
    acc = None
    off = 0
    for p in part_refs:
        k = p.shape[1]
        y = jnp.dot(p[...], w_ref[off:off + k, :], preferred_element_type=F32)
        acc = y if acc is None else acc + y
        off += k
    y = acc + b_ref[...]
    o_ref[...] = x_ref[...] + mod_ref[0, gate_row:gate_row + 1, :] * y


def _proj_residual(parts, w, bias, x2, mod, gate_row, seq, tm, name):
    t, d = x2.shape
    n = w.shape[1]
    in_specs = [pl.BlockSpec((tm, p.shape[1]), lambda i: (i, 0)) for p in parts]
    in_specs += [pl.BlockSpec(w.shape, lambda i: (0, 0)),
                 pl.BlockSpec((1, n), lambda i: (0, 0)),
                 pl.BlockSpec((tm, d), lambda i: (i, 0)),
                 pl.BlockSpec((1, 6, d), lambda i: (i * tm // seq, 0, 0))]
    return pl.pallas_call(
        functools.partial(_proj_res_body, n_parts=len(parts), gate_row=gate_row),
        grid=(t // tm,),
        in_specs=in_specs,
        out_specs=pl.BlockSpec((tm, n), lambda i: (i, 0)),
        out_shape=jax.ShapeDtypeStruct((t, n), F32),
        compiler_params=_cp(("arbitrary",)),
        name=name,
    )(*parts, w, bias.reshape(1, n), x2, mod)


def _shift_rows(x, d, fill, row):
    return jnp.where(row >= d, pltpu.roll(x, d, 0), fill)


def _lru_body(lx_ref, lg_ref, cw_ref, cb_ref, wa_ref, ba_ref, wx_ref, bx_ref, lam_ref, o_ref):
    s, cw = lx_ref.shape
    x = lx_ref[...].astype(F32)
    row = lax.broadcasted_iota(jnp.int32, (s, cw), 0)
    xc = x * cw_ref[LRU_CONV - 1:LRU_CONV, :] + cb_ref[...]
    for d in range(1, LRU_CONV):
        xc = xc + _shift_rows(x, d, 0.0, row) * cw_ref[LRU_CONV - 1 - d:LRU_CONV - d, :]
    xb = xc.astype(BF16)
    ga = jax.nn.sigmoid(jnp.dot(xb, wa_ref[0].astype(BF16), preferred_element_type=F32) + ba_ref[...])
    gx = jax.nn.sigmoid(jnp.dot(xb, wx_ref[0].astype(BF16), preferred_element_type=F32) + bx_ref[...])
    z = -lam_ref[...]
    softplus = jnp.maximum(z, 0.0) + jnp.log(1.0 + jnp.exp(-jnp.abs(z)))
    log_a = (-LRU_C) * ga * softplus
    a = jnp.exp(log_a)
    mult = jnp.sqrt(1.0 - a * a)
    mult = jnp.where(row == 0, 1.0, mult)
    b = gx * xc * mult
    d = 1
    while d < s:
        a_sh = _shift_rows(a, d, 1.0, row)
        b_sh = _shift_rows(b, d, 0.0, row)
        b = a * b_sh + b
        a = a * a_sh
        d *= 2
    lg = lg_ref[...].astype(F32)
    gelu = 0.5 * lg * (1.0 + jnp.tanh(0.7978845608028654 * (lg + 0.044715 * lg * lg * lg)))
    o_ref[...] = (b * gelu).astype(o_ref.dtype)


def _rg_lru(proj, conv_w, conv_b, wa, ba, wx, bx, lam, batch, seq):
    cw = LRU_BW
    nc = LRU_WIDTH // cw
    vec = lambda v: v.reshape(1, LRU_WIDTH)
    vspec = pl.BlockSpec((1, cw), lambda b, c: (0, c))
    return pl.pallas_call(
        _lru_body,
        grid=(batch, nc),
        in_specs=[pl.BlockSpec((seq, cw), lambda b, c: (b, c)),
                  pl.BlockSpec((seq, cw), lambda b, c: (b, c + nc)),
                  pl.BlockSpec((LRU_CONV, cw), lambda b, c: (0, c)),
                  vspec,
                  pl.BlockSpec((1, cw, cw), lambda b, c: (c, 0, 0)),
                  vspec,
                  pl.BlockSpec((1, cw, cw), lambda b, c: (c, 0, 0)),
                  vspec, vspec],
        out_specs=pl.BlockSpec((seq, cw), lambda b, c: (b, c)),
        out_shape=jax.ShapeDtypeStruct((batch * seq, LRU_WIDTH), BF16),
        compiler_params=_cp(("arbitrary", "arbitrary")),
        name="rg_lru",
    )(proj, proj, conv_w, vec(conv_b), wa, vec(ba), wx, vec(bx), vec(lam))


def _rope(x, c, s1, s2, half):
    w = x.shape[1]
    return x * c + pltpu.roll(x, w - half, 1) * s1 + pltpu.roll(x, half, 1) * s2


def _sort_key(x):
    bits = pltpu.bitcast(x, jnp.int32)
    return bits ^ (jnp.right_shift(bits, 31) & 0x7FFFFFFF)


def _dsa_tile(sk, q_ref, v_ref, iq_ref, ta_q_ref, ti_q_ref, o_ref, kr_scr, ikr_scr, key_scr, kc_scr, bias_scr,
              n_sel):
    qi = pl.program_id(1)
    tq = q_ref.shape[0]
    per_group = N_HEADS // N_KV_HEADS

    ci, si1, si2 = ti_q_ref[0], ti_q_ref[1], ti_q_ref[2]
    w_scale = (IDX_HEADS ** -0.5) * (IDX_DIM ** -0.5)
    iw = iq_ref[:, IDX_HEADS * IDX_DIM + IDX_DIM:IDX_HEADS * IDX_DIM + LANES] * w_scale
    ikr = ikr_scr[:sk, :]
    score = jnp.zeros((tq, sk), F32)
    for hp in range(IDX_HEADS // 2):
        pair = _rope(iq_ref[:, hp * LANES:(hp + 1) * LANES], ci, si1, si2, ROT_IDX // 2).astype(BF16)
        for sub in range(2):
            h = 2 * hp + sub
            qh = pair[:, sub * IDX_DIM:(sub + 1) * IDX_DIM]
            dots = lax.dot_general(qh, ikr, (((1,), (1,)), ((), ())), preferred_element_type=F32)
            score = score + jnp.maximum(dots, 0.0) * iw[:, h:h + 1]

    t_row = qi * tq + lax.broadcasted_iota(jnp.int32, (tq, sk), 0)
    col = lax.broadcasted_iota(jnp.int32, (tq, sk), 1)
    causal = col <= t_row
    keys0 = jnp.where(causal, _sort_key(score), INT_MIN)
    key_scr[:, :sk] = keys0
    for c in range(sk // tq):
        kc_scr[c] = keys0[:, c * tq:(c + 1) * tq]

    kf = float(n_sel)

    def count_ge(cand):
        def chunk(c, acc):
            hit = (kc_scr[c] >= cand).astype(F32)
            for l in range(tq // LANES):
                acc = acc + hit[:, l * LANES:(l + 1) * LANES]
            return acc

        acc = lax.fori_loop(0, qi + 1, chunk, jnp.zeros((tq, LANES), F32))
        return jnp.sum(acc, axis=1, keepdims=True)

    tau0 = jnp.where(count_ge(jnp.zeros((tq, 1), jnp.int32)) >= kf, 0, INT_MIN).astype(jnp.int32)

    def bis(i, tau):
        cand = tau + jnp.left_shift(jnp.int32(1), 30 - i)
        return jnp.where(count_ge(cand) >= kf, cand, tau)

    tau = lax.fori_loop(0, 31, bis, tau0)

    keys = key_scr[:, :sk]
    n_gt = jnp.sum((keys > tau).astype(F32), axis=1, keepdims=True)
    n_ge = jnp.sum((keys >= tau).astype(F32), axis=1, keepdims=True)
    need = kf - n_gt
    tie = keys == tau
    excess = jnp.logical_and(n_ge > kf, tau > INT_MIN)
    any_excess = jnp.max(excess.astype(F32)) > 0.0

    bias_scr[:, :sk] = jnp.where(jnp.logical_and(keys >= tau, causal), 0.0, NEG_BIG)

    @pl.when(any_excess)
    def _():
        nbits = sk.bit_length()

        def jb(i, jcur):
            cand = jcur + jnp.left_shift(jnp.int32(1), nbits - 1 - i)
            cnt = jnp.sum(jnp.logical_and(tie, col < cand).astype(F32), axis=1, keepdims=True)
            return jnp.where(cnt <= need, cand, jcur)

        jlim = lax.fori_loop(0, nbits, jb, jnp.zeros((tq, 1), jnp.int32))
        jlim = jnp.where(excess, jlim, sk)
        sel = jnp.logical_or(keys > tau, jnp.logical_and(tie, col < jlim))
        bias_scr[:, :sk] = jnp.where(jnp.logical_and(sel, causal), 0.0, NEG_BIG)

    ca, sa1, sa2 = ta_q_ref[0], ta_q_ref[1], ta_q_ref[2]
    scale = HEAD_DIM ** -0.5
    for h in range(N_HEADS):
        g = h // per_group
        qh = _rope(q_ref[:, h * HEAD_DIM:(h + 1) * HEAD_DIM].astype(F32), ca, sa1, sa2, ROT_ATTN // 2)
        qh = qh.astype(BF16)
        kg = kr_scr[:sk, g * HEAD_DIM:(g + 1) * HEAD_DIM]
        logits = lax.dot_general(qh, kg, (((1,), (1,)), ((), ())), preferred_element_type=F32)
        logits = logits * scale + bias_scr[:, :sk]
        m = jnp.max(logits, axis=1, keepdims=True)
        p = jnp.exp(logits - m)
        den = jnp.sum(p, axis=1, keepdims=True)
        vg = v_ref[:sk, g * HEAD_DIM:(g + 1) * HEAD_DIM]
        o = jnp.dot(p.astype(BF16), vg, preferred_element_type=F32) / den
        o_ref[:, h * HEAD_DIM:(h + 1) * HEAD_DIM] = o.astype(o_ref.dtype)


def _dsa_body(q_ref, k_ref, v_ref, iq_ref, ik_ref, ta_q_ref, ta_k_ref, ti_q_ref, ti_k_ref,
              o_ref, kr_scr, ikr_scr, key_scr, kc_scr, bias_scr, *, n_sel):
    qi = pl.program_id(1)
    s = k_ref.shape[0]

    @pl.when(qi == 0)
    def _():
        ca, sa1, sa2 = ta_k_ref[0], ta_k_ref[1], ta_k_ref[2]
        for g in range(N_KV_HEADS):
            kg = k_ref[:, g * HEAD_DIM:(g + 1) * HEAD_DIM].astype(F32)
            kr_scr[:, g * HEAD_DIM:(g + 1) * HEAD_DIM] = _rope(kg, ca, sa1, sa2, ROT_ATTN // 2).astype(BF16)
        ikp = _rope(ik_ref[...], ti_k_ref[0], ti_k_ref[1], ti_k_ref[2], ROT_IDX // 2)
        ikr_scr[...] = ikp[:, :IDX_DIM].astype(BF16)

    _dsa_tile(s, q_ref, v_ref, iq_ref, ta_q_ref, ti_q_ref, o_ref, kr_scr, ikr_scr, key_scr, kc_scr, bias_scr,
              n_sel)


def _dsa(proj, idx, tab_a, tab_i, batch, seq, n_sel, tq):
    nq = seq // tq
    q_col = 2 * LRU_WIDTH // ATT_WIDTH
    k_col = (2 * LRU_WIDTH + ATT_WIDTH) // KV_WIDTH
    ik_col = IDX_HEADS * IDX_DIM // LANES
    return pl.pallas_call(
        functools.partial(_dsa_body, n_sel=n_sel),
        grid=(batch, nq),
        in_specs=[pl.BlockSpec((tq, ATT_WIDTH), lambda b, i: (b * nq + i, q_col)),
                  pl.BlockSpec((seq, KV_WIDTH), lambda b, i: (b, k_col)),
                  pl.BlockSpec((seq, KV_WIDTH), lambda b, i: (b, k_col + 1)),
                  pl.BlockSpec((tq, IDX_PAD), lambda b, i: (b * nq + i, 0)),
                  pl.BlockSpec((seq, LANES), lambda b, i: (b, ik_col)),
                  pl.BlockSpec((3, tq, LANES), lambda b, i: (0, b * nq + i, 0)),
                  pl.BlockSpec((3, seq, LANES), lambda b, i: (0, b, 0)),
                  pl.BlockSpec((3, tq, LANES), lambda b, i: (0, b * nq + i, 0)),
                  pl.BlockSpec((3, seq, LANES), lambda b, i: (0, b, 0))],
        out_specs=pl.BlockSpec((tq, ATT_WIDTH), lambda b, i: (b * nq + i, 0)),
        out_shape=jax.ShapeDtypeStruct((batch * seq, ATT_WIDTH), BF16),
        scratch_shapes=[pltpu.VMEM((seq, KV_WIDTH), BF16),
                        pltpu.VMEM((seq, IDX_DIM), BF16),
                        pltpu.VMEM((tq, seq), jnp.int32),
                        pltpu.VMEM((seq // tq, tq, tq), jnp.int32),
                        pltpu.VMEM((tq, seq), F32)],
        compiler_params=_cp(("arbitrary", "arbitrary")),
        name="dsa_attention",
    )(proj, proj, proj, idx, idx, tab_a, tab_a, tab_i, tab_i)


def _rope_tables(positions, rot_dim, period):
    half = rot_dim // 2
    inv = jnp.power(jnp.float32(ROPE_THETA), -jnp.arange(0, rot_dim, 2, dtype=F32) / rot_dim)
    ang = positions.astype(F32).reshape(-1, 1) * inv
    cos, sin = jnp.cos(ang), jnp.sin(ang)
    t = cos.shape[0]
    ones = jnp.ones((t, period - rot_dim), F32)
    zeros = jnp.zeros((t, period - rot_dim), F32)
    zh = jnp.zeros((t, half), F32)
    c = jnp.concatenate([cos, cos, ones], axis=1)
    s1 = jnp.concatenate([-sin, zh, zeros], axis=1)
    s2 = jnp.concatenate([zh, sin, zeros], axis=1)
    reps = LANES // period
    return jnp.stack([jnp.tile(c, (1, reps)), jnp.tile(s1, (1, reps)), jnp.tile(s2, (1, reps))])


HALO = 32


def _cfm_conv_body(u_ref, halo_ref, dw_ref, dwb_ref, g_ref, b_ref, o_ref, ext_scr, acc_scr, *, seq, rc, cc):
    tm, c = u_ref.shape
    i = pl.program_id(0)
    at_start = (i * tm) % seq == 0
    halo = jnp.where(at_start, 0.0, halo_ref[...].astype(F32))
    ext_scr[0:HALO, :] = halo
    ext_scr[HALO:, :] = u_ref[...].astype(F32)
    base = HALO - (CFM_CONV - 1)

    def chunk(r, carry):
        r0 = pl.multiple_of(r * rc, rc)
        for c0 in range(0, c, cc):
            win = ext_scr[pl.ds(r0, rc + HALO), c0:c0 + cc]
            acc = jnp.zeros((rc, cc), F32) + dwb_ref[:, c0:c0 + cc]
            for sub in range(8):
                rolled = win if sub == 0 else pltpu.roll(win, rc + HALO - sub, 0)
                for j in range(CFM_CONV):
                    off = base + j
                    if off % 8 == sub:
                        a0 = off - sub
                        acc = acc + rolled[a0:a0 + rc, :] * dw_ref[j:j + 1, c0:c0 + cc]
            acc_scr[pl.ds(r0, rc), c0:c0 + cc] = acc
        return carry

    lax.fori_loop(0, tm // rc, chunk, 0)
    y = acc_scr[...]
    mu = jnp.mean(y, axis=-1, keepdims=True)
    yc = y - mu
    var = jnp.mean(yc * yc, axis=-1, keepdims=True)
    z = yc * lax.rsqrt(var + 1e-5) * g_ref[...] + b_ref[...]
    o_ref[...] = (z * jax.nn.sigmoid(z)).astype(o_ref.dtype)


def _cfm_conv(u, dw, dwb, ln_g, ln_b, seq, tm):
    t, c = u.shape
    hb = tm // HALO
    vec = lambda v: v.reshape(1, c)
    vspec = pl.BlockSpec((1, c), lambda i: (0, 0))
    return pl.pallas_call(
        functools.partial(_cfm_conv_body, seq=seq, rc=32, cc=256),
        grid=(t // tm,),
        in_specs=[pl.BlockSpec((tm, c), lambda i: (i, 0)),
                  pl.BlockSpec((HALO, c), lambda i: (jnp.maximum(i * hb - 1, 0), 0)),
                  pl.BlockSpec((CFM_CONV, c), lambda i: (0, 0)),
                  vspec, vspec, vspec],
        out_specs=pl.BlockSpec((tm, c), lambda i: (i, 0)),
        out_shape=jax.ShapeDtypeStruct((t, c), BF16),
        scratch_shapes=[pltpu.VMEM((tm + HALO, c), F32), pltpu.VMEM((tm, c), F32)],
        compiler_params=_cp(("arbitrary",)),
        name="cfm_conv_ln",
    )(u, u, dw, vec(dwb), vec(ln_g), vec(ln_b))


HI_MASK = -65536


def _pack_rows(y):
    c = y.shape[1] // 2
    lo = pltpu.bitcast(y[:, :c].astype(BF16).astype(F32), jnp.int32)
    hi = pltpu.bitcast(y[:, c:].astype(BF16).astype(F32), jnp.int32)
    return (hi & HI_MASK) | (jnp.right_shift(lo, 16) & 0xFFFF)


def _unpack_rows(w):
    lo = pltpu.bitcast(jnp.left_shift(w, 16), F32)
    hi = pltpu.bitcast(w & HI_MASK, F32)
    return lo, hi


def _moe_pre_body(x_ref, g_ref, mod_ref, wr_ref, swg_ref, swu_ref, swd_ref, h_ref, lg_ref, sh_ref):
    h = _norm_mod(x_ref[...], g_ref[...], mod_ref, SH_F, SC_F)
    hb = h.astype(BF16)
    h_ref[...] = _pack_rows(h)
    e = wr_ref.shape[1]
    ff = swu_ref.shape[1]
    r = jnp.dot(hb, swg_ref[...], preferred_element_type=F32)
    h_lo = (h - hb.astype(F32)).astype(BF16)
    lg = r[:, ff:ff + e] + r[:, ff + e:ff + 2 * e] + jnp.dot(h_lo, wr_ref[...], preferred_element_type=F32)
    lg_ref[...] = lg.T
    gt = r[:, :ff]
    up = jnp.dot(hb, swu_ref[...], preferred_element_type=F32)
    mid = (gt * jax.nn.sigmoid(gt) * up).astype(BF16)
    sh_ref[...] = jnp.dot(mid, swd_ref[...], preferred_element_type=F32)


def _moe_pre(x2, g, mod, w_router, swg, swu, swd, seq, tm):
    t, d = x2.shape
    e = w_router.shape[1]
    full = lambda a: pl.BlockSpec(a.shape, lambda i: (0, 0))
    return pl.pallas_call(
        _moe_pre_body,
        grid=(t // tm,),
        in_specs=[pl.BlockSpec((tm, d), lambda i: (i, 0)),
                  pl.BlockSpec((1, d), lambda i: (0, 0)),
                  pl.BlockSpec((1, 6, d), lambda i: (i * tm // seq, 0, 0)),
                  full(w_router), full(swg), full(swu), full(swd)],
        out_specs=[pl.BlockSpec((tm, d // 2), lambda i: (i, 0)),
                   pl.BlockSpec((e, tm), lambda i: (0, i)),
                   pl.BlockSpec((tm, d), lambda i: (i, 0))],
        out_shape=[jax.ShapeDtypeStruct((t, d // 2), jnp.int32),
                   jax.ShapeDtypeStruct((e, t), F32),
                   jax.ShapeDtypeStruct((t, d), F32)],
        compiler_params=_cp(("arbitrary",)),
        name="moe_pre",
    )(x2, g.reshape(1, d), mod, w_router, swg, swu, swd)


def _first_max(v, ids, sentinel):
    m = jnp.max(v, axis=0, keepdims=True)
    first = jnp.min(jnp.where(v == m, ids, sentinel), axis=0, keepdims=True)
    return m, first


def _route_body(lg_ref, bias_ref, eidx_ref, rank_ref, gw_ref, cnt_ref):
    e, tm = lg_ref.shape
    gsz = e // N_GROUPS
    neg = -jnp.inf

    @pl.when(pl.program_id(0) == 0)
    def _():
        cnt_ref[...] = jnp.zeros_like(cnt_ref)

    scores = jax.nn.sigmoid(lg_ref[...])
    choice = scores + bias_ref[:, 0:1]
    row = lax.broadcasted_iota(jnp.int32, (e, tm), 0)
    sub = lax.broadcasted_iota(jnp.int32, (gsz, tm), 0)

    parts = []
    for g in range(N_GROUPS):
        vg = choice[g * gsz:(g + 1) * gsz, :]
        m1, f1 = _first_max(vg, sub, gsz)
        m2 = jnp.max(jnp.where(sub == f1, neg, vg), axis=0, keepdims=True)
        parts.append(jnp.broadcast_to(m1 + m2, (gsz, tm)))
    gscore = jnp.concatenate(parts, axis=0)

    gid = jnp.right_shift(row, gsz.bit_length() - 1)
    v = jnp.full((e, tm), neg, F32)
    for _ in range(TOPK_GROUPS):
        _, fg = _first_max(gscore, gid, N_GROUPS)
        hit = gid == fg
        v = jnp.where(hit, choice, v)
        gscore = jnp.where(hit, neg, gscore)

    picks = []
    self32 = jnp.zeros((e, tm), F32)
    for _ in range(TOP_K):
        _, fe = _first_max(v, row, e)
        hit = row == fe
        picks.append(fe)
        self32 = jnp.where(hit, 1.0, self32)
        v = jnp.where(hit, neg, v)

    picked = self32 * scores
    gwd = picked / jnp.sum(picked, axis=0, keepdims=True) * ROUTED_SCALE

    before = (lax.broadcasted_iota(jnp.int32, (tm, tm), 0)
              < lax.broadcasted_iota(jnp.int32, (tm, tm), 1)).astype(BF16)
    rank = jnp.dot(self32.astype(BF16), before, preferred_element_type=F32) + cnt_ref[:, 0:1]
    cnt_ref[...] = cnt_ref[...] + jnp.sum(self32, axis=1, keepdims=True)

    for k in range(TOP_K):
        hit = row == picks[k]
        eidx_ref[k:k + 1, :] = picks[k]
        rank_ref[k:k + 1, :] = jnp.sum(jnp.where(hit, rank, 0.0), axis=0, keepdims=True).astype(jnp.int32)
        gw_ref[k:k + 1, :] = jnp.sum(jnp.where(hit, gwd, 0.0), axis=0, keepdims=True)


def _route(logits_t, e_bias, tm):
    e, t = logits_t.shape
    kspec = pl.BlockSpec((TOP_K, tm), lambda i: (0, i))
    return pl.pallas_call(
        _route_body,
        grid=(t // tm,),
        in_specs=[pl.BlockSpec((e, tm), lambda i: (0, i)),
                  pl.BlockSpec((e, LANES), lambda i: (0, 0))],
        out_specs=[kspec, kspec, kspec, pl.BlockSpec((e, LANES), lambda i: (0, 0))],
        out_shape=[jax.ShapeDtypeStruct((TOP_K, t), jnp.int32),
                   jax.ShapeDtypeStruct((TOP_K, t), jnp.int32),
                   jax.ShapeDtypeStruct((TOP_K, t), F32),
                   jax.ShapeDtypeStruct((e, LANES), F32)],
        compiler_params=_cp(("arbitrary",)),
        name="moe_route",
    )(logits_t, jnp.broadcast_to(e_bias.astype(F32).reshape(e, 1), (e, LANES)))


def _experts_body(be_ref, nu_ref, x_ref, wg_ref, wu_ref, wd_ref, o_ref, wg_scr, wu_scr, wd_scr):
    i = pl.program_id(0)
    half = x_ref.shape[1]

    @pl.when(i >= nu_ref[0])
    def _():
        o_ref[...] = jnp.zeros_like(o_ref)

    @pl.when(i < nu_ref[0])
    def _():
        prev = be_ref[jnp.maximum(i - 1, 0)]
        changed = jnp.logical_or(i == 0, be_ref[i] != prev)

        @pl.when(changed)
        def _():
            wg_scr[...] = wg_ref[0, 0].astype(BF16)
            wu_scr[...] = wu_ref[0, 0].astype(BF16)
            wd_scr[...] = wd_ref[0, 0].astype(BF16)

        lo, hi = _unpack_rows(x_ref[...])
        lo = lo.astype(BF16)
        hi = hi.astype(BF16)
        gt = (jnp.dot(lo, wg_scr[:half, :], preferred_element_type=F32)
              + jnp.dot(hi, wg_scr[half:, :], preferred_element_type=F32))
        up = (jnp.dot(lo, wu_scr[:half, :], preferred_element_type=F32)
              + jnp.dot(hi, wu_scr[half:, :], preferred_element_type=F32))
        mid = (gt * jax.nn.sigmoid(gt) * up).astype(BF16)
        o_ref[...] = _pack_rows(jnp.dot(mid, wd_scr[...], preferred_element_type=F32))


def _experts(x_sorted, block_exp, n_used, wg, wu, wd, layer, tm):
    n_rows, half = x_sorted.shape
    d, ff = wg.shape[2], wg.shape[3]
    n_blocks = n_rows // tm
    row_map = lambda i, be, nu: (jnp.minimum(i, nu[0] - 1), 0)
    grid_spec = pltpu.PrefetchScalarGridSpec(
        num_scalar_prefetch=2,
        grid=(n_blocks,),
        in_specs=[pl.BlockSpec((tm, half), row_map),
                  pl.BlockSpec((1, 1, d, ff), lambda i, be, nu: (layer, be[i], 0, 0)),
                  pl.BlockSpec((1, 1, d, ff), lambda i, be, nu: (layer, be[i], 0, 0)),
                  pl.BlockSpec((1, 1, ff, d), lambda i, be, nu: (layer, be[i], 0, 0))],
        out_specs=pl.BlockSpec((tm, half), lambda i, be, nu: (i, 0)),
        scratch_shapes=[pltpu.VMEM((d, ff), BF16), pltpu.VMEM((d, ff), BF16), pltpu.VMEM((ff, d), BF16)],
    )
    return pl.pallas_call(
        _experts_body,
        grid_spec=grid_spec,
        out_shape=jax.ShapeDtypeStruct((n_rows, half), jnp.int32),
        compiler_params=_cp(("arbitrary",)),
        name="moe_experts",
    )(block_exp, n_used, x_sorted, wg, wu, wd)


TOK_STEP = 1024
TOK_SUB = 256


def _stage_pos(pos_hbm, pos_smem, sem, step, n_tok):
    copies = [pltpu.make_async_copy(pos_hbm.at[pl.ds(k * n_tok + step * TOK_STEP, TOK_STEP)],
                                    pos_smem.at[pl.ds(k * TOK_STEP, TOK_STEP)], sem)
              for k in range(TOP_K)]
    for cp in copies:
        cp.start()
    for cp in copies:
        cp.wait()


def _dispatch_body(ends_ref, padded_ref, pos_hbm, h_ref, xs_hbm, pos_smem, zero_scr, sem_pos, sem_fill, sem_rows,
                   *, n_tok, tme):
    i, j = pl.program_id(0), pl.program_id(1)
    sub = h_ref.shape[0]

    @pl.when(jnp.logical_and(i == 0, j == 0))
    def _():
        zero_scr[...] = jnp.zeros_like(zero_scr)

        def fill(e):
            start = pl.multiple_of(ends_ref[e] - tme, tme)
            return pltpu.make_async_copy(zero_scr, xs_hbm.at[pl.ds(start, tme)], sem_fill)

        for e in range(N_EXPERTS):
            @pl.when(padded_ref[e] > 0)
            def _():
                fill(e).start()
        for e in range(N_EXPERTS):
            @pl.when(padded_ref[e] > 0)
            def _():
                fill(e).wait()

        def tail(b):
            return pltpu.make_async_copy(zero_scr, xs_hbm.at[pl.ds(pl.multiple_of(b * tme, tme), tme)], sem_fill)

        n_used = ends_ref[N_EXPERTS - 1] // tme
        n_blocks = xs_hbm.shape[0] // tme
        lax.fori_loop(n_used, n_blocks, lambda b, c: (tail(b).start(), c)[1], 0)
        lax.fori_loop(n_used, n_blocks, lambda b, c: (tail(b).wait(), c)[1], 0)

    @pl.when(j == 0)
    def _():
        _stage_pos(pos_hbm, pos_smem, sem_pos, i, n_tok)

    def tok8(g, carry):
        t0 = pl.multiple_of(g * 8, 8)
        for u in range(8):
            for k in range(TOP_K):
                p = pos_smem[k * TOK_STEP + j * sub + t0 + u]
                pltpu.make_async_copy(h_ref.at[t0 + u], xs_hbm.at[p], sem_rows).start()
        return carry

    lax.fori_loop(0, sub // 8, tok8, 0)
    for k in range(TOP_K):
        pltpu.make_async_copy(h_ref, xs_hbm.at[pl.ds(0, sub)], sem_rows).wait()


def _dispatch(h, pos_flat, ends, padded, n_rows, tme):
    t, half = h.shape
    sub = min(TOK_SUB, t)
    grid_spec = pltpu.PrefetchScalarGridSpec(
        num_scalar_prefetch=2,
        grid=(t // TOK_STEP, TOK_STEP // sub),
        in_specs=[pl.BlockSpec(memory_space=pl.ANY),
                  pl.BlockSpec((sub, half), lambda i, j, en, pd: (i * (TOK_STEP // sub) + j, 0))],
        out_specs=pl.BlockSpec(memory_space=pl.ANY),
        scratch_shapes=[pltpu.SMEM((TOP_K * TOK_STEP,), jnp.int32),
                        pltpu.VMEM((tme, half), jnp.int32),
                        pltpu.SemaphoreType.DMA, pltpu.SemaphoreType.DMA, pltpu.SemaphoreType.DMA],
    )
    return pl.pallas_call(
        functools.partial(_dispatch_body, n_tok=t, tme=tme),
        grid_spec=grid_spec,
        out_shape=jax.ShapeDtypeStruct((n_rows, half), jnp.int32),
        compiler_params=_cp(("arbitrary", "arbitrary")),
        name="moe_dispatch",
    )(ends, padded, pos_flat, h)


def _combine_body(pos_hbm, ys_hbm, gw_ref, x_ref, sh_ref, mod_ref, fg_ref, o_ref, pos_smem, g_scr,
                  sem_pos, sem_rows, *, n_tok, final_norm):
    i, j = pl.program_id(0), pl.program_id(1)
    sub, d = x_ref.shape
    half = d // 2

    @pl.when(j == 0)
    def _():
        _stage_pos(pos_hbm, pos_smem, sem_pos, i, n_tok)

    def tok8(g, carry):
        t0 = pl.multiple_of(g * 8, 8)
        for u in range(8):
            for k in range(TOP_K):
                p = pos_smem[k * TOK_STEP + j * sub + t0 + u]
                pltpu.make_async_copy(ys_hbm.at[p], g_scr.at[k, t0 + u], sem_rows).start()
        return carry

    lax.fori_loop(0, sub // 8, tok8, 0)
    for k in range(TOP_K):
        pltpu.make_async_copy(ys_hbm.at[pl.ds(0, sub)], g_scr.at[k], sem_rows).wait()

    acc_lo = sh_ref[:, :half]
    acc_hi = sh_ref[:, half:]
    for k in range(TOP_K):
        lo, hi = _unpack_rows(g_scr[k])
        w = gw_ref[:, k:k + 1]
        acc_lo = acc_lo + w * lo
        acc_hi = acc_hi + w * hi
    gate = mod_ref[0, G_F:G_F + 1, :]
    y_lo = x_ref[:, :half] + gate[:, :half] * acc_lo
    y_hi = x_ref[:, half:] + gate[:, half:] * acc_hi
    if final_norm:
        ms = (jnp.sum(y_lo * y_lo, axis=-1, keepdims=True) + jnp.sum(y_hi * y_hi, axis=-1, keepdims=True)) / d
        r = lax.rsqrt(ms + 1e-6)
        y_lo = y_lo * r * fg_ref[:, :half]
        y_hi = y_hi * r * fg_ref[:, half:]
    o_ref[:, :half] = y_lo
    o_ref[:, half:] = y_hi


def _combine(ys, pos_flat, gw, x2, shared, mod, final_g, seq, final_norm):
    t, d = x2.shape
    sub = min(TOK_SUB, t)
    nj = TOK_STEP // sub
    row = lambda i, j: (i * nj + j, 0)
    return pl.pallas_call(
        functools.partial(_combine_body, n_tok=t, final_norm=final_norm),
        grid=(t // TOK_STEP, nj),
        in_specs=[pl.BlockSpec(memory_space=pl.ANY),
                  pl.BlockSpec(memory_space=pl.ANY),
                  pl.BlockSpec((sub, TOP_K), row),
                  pl.BlockSpec((sub, d), row),
                  pl.BlockSpec((sub, d), row),
                  pl.BlockSpec((1, 6, d), lambda i, j: ((i * nj + j) * sub // seq, 0, 0)),
                  pl.BlockSpec((1, d), lambda i, j: (0, 0))],
        out_specs=pl.BlockSpec((sub, d), row),
        out_shape=jax.ShapeDtypeStruct((t, d), F32),
        scratch_shapes=[pltpu.SMEM((TOP_K * TOK_STEP,), jnp.int32),
                        pltpu.VMEM((TOP_K, sub, d // 2), jnp.int32),
                        pltpu.SemaphoreType.DMA, pltpu.SemaphoreType.DMA],
        compiler_params=_cp(("arbitrary", "arbitrary")),
        name="moe_combine",
    )(pos_flat, ys, gw, x2, shared, mod, final_g.reshape(1, d))


def _dispatch_plan(eidx_t, rank_t, counts, tm):
    n_assign = eidx_t.size
    padded = (counts + tm - 1) // tm * tm
    ends = jnp.cumsum(padded)
    pstart = ends - padded
    pos_t = rank_t
    for e in range(N_EXPERTS):
        pos_t = pos_t + jnp.where(eidx_t == e, pstart[e], 0)
    n_blocks = -(-n_assign // tm) + N_EXPERTS
    blk_start = jnp.arange(n_blocks, dtype=jnp.int32) * tm
    block_exp = jnp.minimum(jnp.sum((ends[None, :] <= blk_start[:, None]).astype(jnp.int32), axis=1),
                            N_EXPERTS - 1)
    n_used = (ends[-1] // tm).astype(jnp.int32).reshape(1)
    return pos_t.reshape(-1), ends, padded, block_exp, n_used, n_blocks


def _moe(x2, g, mod, w_router, e_bias, wg, wu, wd, layer, swg, swu, swd, final_g, seq, tm, tme, final_norm):
    wr_hi = w_router.astype(BF16)
    wr_lo = (w_router - wr_hi.astype(F32)).astype(BF16)
    gate_and_router = jnp.concatenate([swg.astype(BF16), wr_hi, wr_lo], axis=1)
    h, logits_t, shared = _moe_pre(x2, g, mod, wr_hi, gate_and_router, swu.astype(BF16),
                                   swd.astype(BF16), seq, tm)
    eidx_t, rank_t, gw_t, cnt = _route(logits_t, e_bias, tm)
    counts = cnt[:, 0].astype(jnp.int32)
    pos_flat, ends, padded, block_exp, n_used, n_blocks = _dispatch_plan(eidx_t, rank_t, counts, tme)
    x_sorted = _dispatch(h, pos_flat, ends, padded, n_blocks * tme, tme)
    y_sorted = _experts(x_sorted, block_exp, n_used, wg, wu, wd, layer, tme)
    return _combine(y_sorted, pos_flat, gw_t.T, x2, shared, mod, final_g, seq, final_norm)


def kernel(x, c, positions, mod_w, mod_b, norm_mix, norm_ffn, hyb_w_in, hyb_w_out, lru_conv_w, lru_conv_b, lru_wa, lru_ba, lru_wx, lru_bx, lru_lambda, cfm_w1, cfm_b1, cfm_dw, cfm_dwb, cfm_ln_g, cfm_ln_b, cfm_w2, cfm_b2, moe_router, moe_bias, moe_wg, moe_wu, moe_wd, sh_wg, sh_wu, sh_wd, final_norm):
    batch, seq, d = x.shape
    t = batch * seq
    n_sel = min(INDEX_TOPK, seq // 4)
    tm = min(512, seq)
    tq = min(256, seq)
    tme = MOE_ROWS

    mod_all = _modulation(c, mod_w, mod_b).reshape(mod_w.shape[0], batch, 6, d)
    tab_a = _rope_tables(positions, ROT_ATTN, HEAD_DIM)
    tab_i = _rope_tables(positions, ROT_IDX, IDX_DIM)
    x2 = x.reshape(t, d)
    zero_bias = jnp.zeros((d,), F32)

    mod = mod_all[0]
    w_in = hyb_w_in[0]
    w_main = w_in[:, :MAIN_COLS].astype(BF16)
    w_idx = jnp.pad(w_in[:, MAIN_COLS:], ((0, 0), (0, IDX_PAD - IDX_COLS))).astype(BF16)
    proj = _nm_matmul(x2, norm_mix[0], mod, w_main, seq, tm, MAIN_COLS, BF16, "hyb_in_main")
    idx = _nm_matmul(x2, norm_mix[0], mod, w_idx, seq, tm, IDX_PAD, F32, "hyb_in_idx")
    y_lru = _rg_lru(proj, lru_conv_w[0], lru_conv_b[0], lru_wa[0], lru_ba[0], lru_wx[0], lru_bx[0],
                    lru_lambda[0], batch, seq)
    y_att = _dsa(proj, idx, tab_a, tab_i, batch, seq, n_sel, tq)
    x2 = _proj_residual([y_lru, y_att], hyb_w_out[0].astype(BF16), zero_bias, x2, mod, G_M, seq, tm,
                        "hyb_out")
    x2 = _moe(x2, norm_ffn[0], mod, moe_router[0], moe_bias[0], moe_wg, moe_wu, moe_wd, 0,
              sh_wg[0], sh_wu[0], sh_wd[0], final_norm, seq, tm, tme, False)

    mod = mod_all[1]
    u = _nm_glu(x2, norm_mix[1], mod, cfm_w1[0].astype(BF16), cfm_b1[0], seq, tm, d)
    z = _cfm_conv(u, cfm_dw[0], cfm_dwb[0], cfm_ln_g[0], cfm_ln_b[0], seq, tm)
    x2 = _proj_residual([z], cfm_w2[0].astype(BF16), cfm_b2[0], x2, mod, G_M, seq, tm, "cfm_out")
    out = _moe(x2, norm_ffn[1], mod, moe_router[1], moe_bias[1], moe_wg, moe_wu, moe_wd, 1,
               sh_wg[1], sh_wu[1], sh_wd[1], final_norm, seq, tm, tme, True)
    return out.reshape(batch, seq, d)
```

```python
import functools

import jax
import jax.numpy as jnp
from jax import lax
from jax.experimental import pallas as pl
from jax.experimental.pallas import tpu as pltpu

F32 = jnp.float32
BF16 = jnp.bfloat16

LRU_WIDTH = 1024
LRU_BW = 128
LRU_CONV = 4
LRU_C = 8.0
N_HEADS = 8
N_KV_HEADS = 2
HEAD_DIM = 128
ATT_WIDTH = N_HEADS * HEAD_DIM
KV_WIDTH = N_KV_HEADS * HEAD_DIM
IDX_HEADS = 8
IDX_DIM = 64
INDEX_TOPK = 256
ROPE_THETA = 500000.0
ROT_ATTN = HEAD_DIM // 4
ROT_IDX = IDX_DIM // 4
CFM_CONV = 31
N_EXPERTS = 64
TOP_K = 8
N_GROUPS = 8
TOPK_GROUPS = 4
ROUTED_SCALE = 2.5

MAIN_COLS = 2 * LRU_WIDTH + ATT_WIDTH + 2 * KV_WIDTH
IDX_COLS = IDX_HEADS * IDX_DIM + IDX_DIM + IDX_HEADS
IDX_PAD = 640

MOE_ROWS = 512
LANES = 128
VMEM_LIMIT = 56 * 1024 * 1024
INT_MIN = -2147483648
NEG_BIG = -1e30

SH_M, SC_M, G_M, SH_F, SC_F, G_F = range(6)


def _cp(sem):
    return pltpu.CompilerParams(dimension_semantics=sem, vmem_limit_bytes=VMEM_LIMIT)


def _weight_spec(block, index_map, n_col_blocks):
    if n_col_blocks == 1:
        return pl.BlockSpec(block, index_map, pipeline_mode=pl.Buffered(1))
    return pl.BlockSpec(block, index_map)


def _norm_mod(x, g, mod_ref, sh_row, sc_row):
    ms = jnp.mean(x * x, axis=-1, keepdims=True)
    y = x * lax.rsqrt(ms + 1e-6) * g
    return y * (1.0 + mod_ref[0, sc_row:sc_row + 1, :]) + mod_ref[0, sh_row:sh_row + 1, :]


def _mod_body(c_ref, w_ref, b_ref, o_ref):
    c = c_ref[...]
    cond = c * jax.nn.sigmoid(c)
    o_ref[0] = jnp.dot(cond, w_ref[0], preferred_element_type=F32,
                       precision=lax.Precision.HIGHEST) + b_ref[0]


def _modulation(c, mod_w, mod_b):
    depth, d, n = mod_w.shape
    b = c.shape[0]
    tn = 512
    return pl.pallas_call(
        _mod_body,
        grid=(depth, n // tn),
        in_specs=[pl.BlockSpec((b, d), lambda l, j: (0, 0)),
                  pl.BlockSpec((1, d, tn), lambda l, j: (l, 0, j)),
                  pl.BlockSpec((1, 1, tn), lambda l, j: (l, 0, j))],
        out_specs=pl.BlockSpec((1, b, tn), lambda l, j: (l, 0, j)),
        out_shape=jax.ShapeDtypeStruct((depth, b, n), F32),
        compiler_params=_cp(("arbitrary", "arbitrary")),
        name="modulation",
    )(c, mod_w, mod_b.reshape(depth, 1, n))


def _nm_matmul_body(x_ref, g_ref, mod_ref, w_ref, o_ref, h_scr):
    @pl.when(pl.program_id(1) == 0)
    def _():
        h_scr[...] = _norm_mod(x_ref[...], g_ref[...], mod_ref, SH_M, SC_M).astype(BF16)

    o_ref[...] = jnp.dot(h_scr[...], w_ref[...], preferred_element_type=F32).astype(o_ref.dtype)


def _nm_matmul(x2, g, mod, w, seq, tm, tn, out_dtype, name):
    t, d = x2.shape
    n = w.shape[1]
    return pl.pallas_call(
        _nm_matmul_body,
        grid=(t // tm, n // tn),
        in_specs=[pl.BlockSpec((tm, d), lambda i, j: (i, 0)),
                  pl.BlockSpec((1, d), lambda i, j: (0, 0)),
                  pl.BlockSpec((1, 6, d), lambda i, j: (i * tm // seq, 0, 0)),
                  _weight_spec((d, tn), lambda i, j: (0, j), n // tn)],
        out_specs=pl.BlockSpec((tm, tn), lambda i, j: (i, j)),
        out_shape=jax.ShapeDtypeStruct((t, n), out_dtype),
        scratch_shapes=[pltpu.VMEM((tm, d), BF16)],
        compiler_params=_cp(("arbitrary", "arbitrary")),
        name=name,
    )(x2, g.reshape(1, d), mod, w)


def _nm_glu_body(x_ref, g_ref, mod_ref, wa_ref, wg_ref, ba_ref, bg_ref, o_ref, h_scr):
    @pl.when(pl.program_id(1) == 0)
    def _():
        h_scr[...] = _norm_mod(x_ref[...], g_ref[...], mod_ref, SH_M, SC_M).astype(BF16)

    h = h_scr[...]
    a = jnp.dot(h, wa_ref[...], preferred_element_type=F32) + ba_ref[...]
    gt = jnp.dot(h, wg_ref[...], preferred_element_type=F32) + bg_ref[...]
    o_ref[...] = (a * jax.nn.sigmoid(gt)).astype(o_ref.dtype)


def _nm_glu(x2, g, mod, w1, b1, seq, tm, tn):
    t, d = x2.shape
    n = w1.shape[1] // 2
    nj = n // tn
    return pl.pallas_call(
        _nm_glu_body,
        grid=(t // tm, nj),
        in_specs=[pl.BlockSpec((tm, d), lambda i, j: (i, 0)),
                  pl.BlockSpec((1, d), lambda i, j: (0, 0)),
                  pl.BlockSpec((1, 6, d), lambda i, j: (i * tm // seq, 0, 0)),
                  _weight_spec((d, tn), lambda i, j: (0, j), nj),
                  _weight_spec((d, tn), lambda i, j: (0, j + nj), nj),
                  pl.BlockSpec((1, tn), lambda i, j: (0, j)),
                  pl.BlockSpec((1, tn), lambda i, j: (0, j + nj))],
        out_specs=pl.BlockSpec((tm, tn), lambda i, j: (i, j)),
        out_shape=jax.ShapeDtypeStruct((t, n), BF16),
        scratch_shapes=[pltpu.VMEM((tm, d), BF16)],
        compiler_params=_cp(("arbitrary", "arbitrary")),
        name="cfm_in_glu",
    )(x2, g.reshape(1, d), mod, w1, w1, b1.reshape(1, 2 * n), b1.reshape(1, 2 * n))


def _proj_res_body(*refs, n_parts, gate_row):
    part_refs = refs[:n_parts]
    w_ref, b_ref, x_ref, mod_ref, o_ref = refs[n_parts:]
    acc = None
    off = 0
    for p in part_refs:
        k = p.shape[1]
        y = jnp.dot(p[...], w_ref[off:off + k, :], preferred_element_type=F32)
        acc = y if acc is None else acc + y
        off += k
    y = acc + b_ref[...]
    o_ref[...] = x_ref[...] + mod_ref[0, gate_row:gate_row + 1, :] * y


def _proj_residual(parts, w, bias, x2, mod, gate_row, seq, tm, name):
    t, d = x2.shape
    n = w.shape[1]
    in_specs = [pl.BlockSpec((tm, p.shape[1]), lambda i: (i, 0)) for p in parts]
    in_specs += [pl.BlockSpec(w.shape, lambda i: (0, 0)),
                 pl.BlockSpec((1, n), lambda i: (0, 0)),
                 pl.BlockSpec((tm, d), lambda i: (i, 0)),
                 pl.BlockSpec((1, 6, d), lambda i: (i * tm // seq, 0, 0))]
    return pl.pallas_call(
        functools.partial(_proj_res_body, n_parts=len(parts), gate_row=gate_row),
        grid=(t // tm,),
        in_specs=in_specs,
        out_specs=pl.BlockSpec((tm, n), lambda i: (i, 0)),
        out_shape=jax.ShapeDtypeStruct((t, n), F32),
        compiler_params=_cp(("arbitrary",)),
        name=name,
    )(*parts, w, bias.reshape(1, n), x2, mod)


def _shift_rows(x, d, fill, row):
    return jnp.where(row >= d, pltpu.roll(x, d, 0), fill)


def _lru_body(lx_ref, lg_ref, cw_ref, cb_ref, wa_ref, ba_ref, wx_ref, bx_ref, lam_ref, o_ref):
    s, cw = lx_ref.shape
    x = lx_ref[...].astype(F32)
    row = lax.broadcasted_iota(jnp.int32, (s, cw), 0)
    xc = x * cw_ref[LRU_CONV - 1:LRU_CONV, :] + cb_ref[...]
    for d in range(1, LRU_CONV):
        xc = xc + _shift_rows(x, d, 0.0, row) * cw_ref[LRU_CONV - 1 - d:LRU_CONV - d, :]
    xb = xc.astype(BF16)
    ga = jax.nn.sigmoid(jnp.dot(xb, wa_ref[0].astype(BF16), preferred_element_type=F32) + ba_ref[...])
    gx = jax.nn.sigmoid(jnp.dot(xb, wx_ref[0].astype(BF16), preferred_element_type=F32) + bx_ref[...])
    z = -lam_ref[...]
    softplus = jnp.maximum(z, 0.0) + jnp.log(1.0 + jnp.exp(-jnp.abs(z)))
    log_a = (-LRU_C) * ga * softplus
    a = jnp.exp(log_a)
    mult = jnp.sqrt(1.0 - a * a)
    mult = jnp.where(row == 0, 1.0, mult)
    b = gx * xc * mult
    d = 1
    while d < s:
        a_sh = _shift_rows(a, d, 1.0, row)
        b_sh = _shift_rows(b, d, 0.0, row)
        b = a * b_sh + b
        a = a * a_sh
        d *= 2
    lg = lg_ref[...].astype(F32)
    gelu = 0.5 * lg * (1.0 + jnp.tanh(0.7978845608028654 * (lg + 0.044715 * lg * lg * lg)))
    o_ref[...] = (b * gelu).astype(o_ref.dtype)


def _rg_lru(proj, conv_w, conv_b, wa, ba, wx, bx, lam, batch, seq):
    cw = LRU_BW
    nc = LRU_WIDTH // cw
    vec = lambda v: v.reshape(1, LRU_WIDTH)
    vspec = pl.BlockSpec((1, cw), lambda b, c: (0, c))
    return pl.pallas_call(
        _lru_body,
        grid=(batch, nc),
        in_specs=[pl.BlockSpec((seq, cw), lambda b, c: (b, c)),
                  pl.BlockSpec((seq, cw), lambda b, c: (b, c + nc)),
                  pl.BlockSpec((LRU_CONV, cw), lambda b, c: (0, c)),
                  vspec,
                  pl.BlockSpec((1, cw, cw), lambda b, c: (c, 0, 0)),
                  vspec,
                  pl.BlockSpec((1, cw, cw), lambda b, c: (c, 0, 0)),
                  vspec, vspec],
        out_specs=pl.BlockSpec((seq, cw), lambda b, c: (b, c)),
        out_shape=jax.ShapeDtypeStruct((batch * seq, LRU_WIDTH), BF16),
        compiler_params=_cp(("arbitrary", "arbitrary")),
        name="rg_lru",
    )(proj, proj, conv_w, vec(conv_b), wa, vec(ba), wx, vec(bx), vec(lam))


def _rope(x, c, s1, s2, half):
    w = x.shape[1]
    return x * c + pltpu.roll(x, w - half, 1) * s1 + pltpu.roll(x, half, 1) * s2


def _sort_key(x):
    bits = pltpu.bitcast(x, jnp.int32)
    return bits ^ (jnp.right_shift(bits, 31) & 0x7FFFFFFF)


def _dsa_tile(sk, q_ref, v_ref, iq_ref, ta_q_ref, ti_q_ref, o_ref, kr_scr, ikr_scr, key_scr, bias_scr, n_sel):
    qi = pl.program_id(1)
    tq = q_ref.shape[0]
    per_group = N_HEADS // N_KV_HEADS

    ci, si1, si2 = ti_q_ref[0], ti_q_ref[1], ti_q_ref[2]
    w_scale = (IDX_HEADS ** -0.5) * (IDX_DIM ** -0.5)
    iw = iq_ref[:, IDX_HEADS * IDX_DIM + IDX_DIM:IDX_HEADS * IDX_DIM + LANES] * w_scale
    ikr = ikr_scr[:sk, :]
    score = jnp.zeros((tq, sk), F32)
    for hp in range(IDX_HEADS // 2):
        pair = _rope(iq_ref[:, hp * LANES:(hp + 1) * LANES], ci, si1, si2, ROT_IDX // 2).astype(BF16)
        for sub in range(2):
            h = 2 * hp + sub
            qh = pair[:, sub * IDX_DIM:(sub + 1) * IDX_DIM]
            dots = lax.dot_general(qh, ikr, (((1,), (1,)), ((), ())), preferred_element_type=F32)
            score = score + jnp.maximum(dots, 0.0) * iw[:, h:h + 1]

    t_row = qi * tq + lax.broadcasted_iota(jnp.int32, (tq, sk), 0)
    col = lax.broadcasted_iota(jnp.int32, (tq, sk), 1)
    causal = col <= t_row
    key_scr[:, :sk] = jnp.where(causal, _sort_key(score), INT_MIN)

    kf = float(n_sel)

    def count_ge(cand):
        return jnp.sum((key_scr[:, :sk] >= cand).astype(F32), axis=1, keepdims=True)

    tau0 = jnp.where(count_ge(jnp.zeros((tq, 1), jnp.int32)) >= kf, 0, INT_MIN).astype(jnp.int32)

    def bis(i, tau):
        cand = tau + jnp.left_shift(jnp.int32(1), 30 - i)
        return jnp.where(count_ge(cand) >= kf, cand, tau)

    tau = lax.fori_loop(0, 31, bis, tau0)

    keys = key_scr[:, :sk]
    n_gt = jnp.sum((keys > tau).astype(F32), axis=1, keepdims=True)
    n_ge = jnp.sum((keys >= tau).astype(F32), axis=1, keepdims=True)
    need = kf - n_gt
    tie = keys == tau
    excess = jnp.logical_and(n_ge > kf, tau > INT_MIN)
    any_excess = jnp.max(excess.astype(F32)) > 0.0

    bias_scr[:, :sk] = jnp.where(jnp.logical_and(keys >= tau, causal), 0.0, NEG_BIG)

    @pl.when(any_excess)
    def _():
        nbits = sk.bit_length()

        def jb(i, jcur):
            cand = jcur + jnp.left_shift(jnp.int32(1), nbits - 1 - i)
            cnt = jnp.sum(jnp.logical_and(tie, col < cand).astype(F32), axis=1, keepdims=True)
            return jnp.where(cnt <= need, cand, jcur)

        jlim = lax.fori_loop(0, nbits, jb, jnp.zeros((tq, 1), jnp.int32))
        jlim = jnp.where(excess, jlim, sk)
        sel = jnp.logical_or(keys > tau, jnp.logical_and(tie, col < jlim))
        bias_scr[:, :sk] = jnp.where(jnp.logical_and(sel, causal), 0.0, NEG_BIG)

    ca, sa1, sa2 = ta_q_ref[0], ta_q_ref[1], ta_q_ref[2]
    scale = HEAD_DIM ** -0.5
    for h in range(N_HEADS):
        g = h // per_group
        qh = _rope(q_ref[:, h * HEAD_DIM:(h + 1) * HEAD_DIM].astype(F32), ca, sa1, sa2, ROT_ATTN // 2)
        qh = qh.astype(BF16)
        kg = kr_scr[:sk, g * HEAD_DIM:(g + 1) * HEAD_DIM]
        logits = lax.dot_general(qh, kg, (((1,), (1,)), ((), ())), preferred_element_type=F32)
        logits = logits * scale + bias_scr[:, :sk]
        m = jnp.max(logits, axis=1, keepdims=True)
        p = jnp.exp(logits - m)
        den = jnp.sum(p, axis=1, keepdims=True)
        vg = v_ref[:sk, g * HEAD_DIM:(g + 1) * HEAD_DIM]
        o = jnp.dot(p.astype(BF16), vg, preferred_element_type=F32) / den
        o_ref[:, h * HEAD_DIM:(h + 1) * HEAD_DIM] = o.astype(o_ref.dtype)


def _dsa_body(q_ref, k_ref, v_ref, iq_ref, ik_ref, ta_q_ref, ta_k_ref, ti_q_ref, ti_k_ref,
              o_ref, kr_scr, ikr_scr, key_scr, bias_scr, *, n_sel):
    qi = pl.program_id(1)
    s = k_ref.shape[0]

    @pl.when(qi == 0)
    def _():
        ca, sa1, sa2 = ta_k_ref[0], ta_k_ref[1], ta_k_ref[2]
        for g in range(N_KV_HEADS):
            kg = k_ref[:, g * HEAD_DIM:(g + 1) * HEAD_DIM].astype(F32)
            kr_scr[:, g * HEAD_DIM:(g + 1) * HEAD_DIM] = _rope(kg, ca, sa1, sa2, ROT_ATTN // 2).astype(BF16)
        ikp = _rope(ik_ref[...], ti_k_ref[0], ti_k_ref[1], ti_k_ref[2], ROT_IDX // 2)
        ikr_scr[...] = ikp[:, :IDX_DIM].astype(BF16)

    _dsa_tile(s, q_ref, v_ref, iq_ref, ta_q_ref, ti_q_ref, o_ref, kr_scr, ikr_scr, key_scr, bias_scr, n_sel)


def _dsa(proj, idx, tab_a, tab_i, batch, seq, n_sel, tq):
    nq = seq // tq
    q_col = 2 * LRU_WIDTH // ATT_WIDTH
    k_col = (2 * LRU_WIDTH + ATT_WIDTH) // KV_WIDTH
    ik_col = IDX_HEADS * IDX_DIM // LANES
    return pl.pallas_call(
        functools.partial(_dsa_body, n_sel=n_sel),
        grid=(batch, nq),
        in_specs=[pl.BlockSpec((tq, ATT_WIDTH), lambda b, i: (b * nq + i, q_col)),
                  pl.BlockSpec((seq, KV_WIDTH), lambda b, i: (b, k_col)),
                  pl.BlockSpec((seq, KV_WIDTH), lambda b, i: (b, k_col + 1)),
                  pl.BlockSpec((tq, IDX_PAD), lambda b, i: (b * nq + i, 0)),
                  pl.BlockSpec((seq, LANES), lambda b, i: (b, ik_col)),
                  pl.BlockSpec((3, tq, LANES), lambda b, i: (0, b * nq + i, 0)),
                  pl.BlockSpec((3, seq, LANES), lambda b, i: (0, b, 0)),
                  pl.BlockSpec((3, tq, LANES), lambda b, i: (0, b * nq + i, 0)),
                  pl.BlockSpec((3, seq, LANES), lambda b, i: (0, b, 0))],
        out_specs=pl.BlockSpec((tq, ATT_WIDTH), lambda b, i: (b * nq + i, 0)),
        out_shape=jax.ShapeDtypeStruct((batch * seq, ATT_WIDTH), BF16),
        scratch_shapes=[pltpu.VMEM((seq, KV_WIDTH), BF16),
                        pltpu.VMEM((seq, IDX_DIM), BF16),
                        pltpu.VMEM((tq, seq), jnp.int32),
                        pltpu.VMEM((tq, seq), F32)],
        compiler_params=_cp(("arbitrary", "arbitrary")),
        name="dsa_attention",
    )(proj, proj, proj, idx, idx, tab_a, tab_a, tab_i, tab_i)


def _rope_tables(positions, rot_dim, period):
    half = rot_dim // 2
    inv = jnp.power(jnp.float32(ROPE_THETA), -jnp.arange(0, rot_dim, 2, dtype=F32) / rot_dim)
    ang = positions.astype(F32).reshape(-1, 1) * inv
    cos, sin = jnp.cos(ang), jnp.sin(ang)
    t = cos.shape[0]
    ones = jnp.ones((t, period - rot_dim), F32)
    zeros = jnp.zeros((t, period - rot_dim), F32)
    zh = jnp.zeros((t, half), F32)
    c = jnp.concatenate([cos, cos, ones], axis=1)
    s1 = jnp.concatenate([-sin, zh, zeros], axis=1)
    s2 = jnp.concatenate([zh, sin, zeros], axis=1)
    reps = LANES // period
    return jnp.stack([jnp.tile(c, (1, reps)), jnp.tile(s1, (1, reps)), jnp.tile(s2, (1, reps))])


HALO = 32


def _cfm_conv_body(u_ref, halo_ref, dw_ref, dwb_ref, g_ref, b_ref, o_ref, ext_scr, acc_scr, *, seq, rc, cc):
    tm, c = u_ref.shape
    i = pl.program_id(0)
    at_start = (i * tm) % seq == 0
    halo = jnp.where(at_start, 0.0, halo_ref[...].astype(F32))
    ext_scr[0:HALO, :] = halo
    ext_scr[HALO:, :] = u_ref[...].astype(F32)
    base = HALO - (CFM_CONV - 1)

    def chunk(r, carry):
        r0 = pl.multiple_of(r * rc, rc)
        for c0 in range(0, c, cc):
            win = ext_scr[pl.ds(r0, rc + HALO), c0:c0 + cc]
            acc = jnp.zeros((rc, cc), F32) + dwb_ref[:, c0:c0 + cc]
            for sub in range(8):
                rolled = win if sub == 0 else pltpu.roll(win, rc + HALO - sub, 0)
                for j in range(CFM_CONV):
                    off = base + j
                    if off % 8 == sub:
                        a0 = off - sub
                        acc = acc + rolled[a0:a0 + rc, :] * dw_ref[j:j + 1, c0:c0 + cc]
            acc_scr[pl.ds(r0, rc), c0:c0 + cc] = acc
        return carry

    lax.fori_loop(0, tm // rc, chunk, 0)
    y = acc_scr[...]
    mu = jnp.mean(y, axis=-1, keepdims=True)
    yc = y - mu
    var = jnp.mean(yc * yc, axis=-1, keepdims=True)
    z = yc * lax.rsqrt(var + 1e-5) * g_ref[...] + b_ref[...]
    o_ref[...] = (z * jax.nn.sigmoid(z)).astype(o_ref.dtype)


def _cfm_conv(u, dw, dwb, ln_g, ln_b, seq, tm):
    t, c = u.shape
    hb = tm // HALO
    vec = lambda v: v.reshape(1, c)
    vspec = pl.BlockSpec((1, c), lambda i: (0, 0))
    return pl.pallas_call(
        functools.partial(_cfm_conv_body, seq=seq, rc=32, cc=256),
        grid=(t // tm,),
        in_specs=[pl.BlockSpec((tm, c), lambda i: (i, 0)),
                  pl.BlockSpec((HALO, c), lambda i: (jnp.maximum(i * hb - 1, 0), 0)),
                  pl.BlockSpec((CFM_CONV, c), lambda i: (0, 0)),
                  vspec, vspec, vspec],
        out_specs=pl.BlockSpec((tm, c), lambda i: (i, 0)),
        out_shape=jax.ShapeDtypeStruct((t, c), BF16),
        scratch_shapes=[pltpu.VMEM((tm + HALO, c), F32), pltpu.VMEM((tm, c), F32)],
        compiler_params=_cp(("arbitrary",)),
        name="cfm_conv_ln",
    )(u, u, dw, vec(dwb), vec(ln_g), vec(ln_b))


HI_MASK = -65536


def _pack_rows(y):
    c = y.shape[1] // 2
    lo = pltpu.bitcast(y[:, :c].astype(BF16).astype(F32), jnp.int32)
    hi = pltpu.bitcast(y[:, c:].astype(BF16).astype(F32), jnp.int32)
    return (hi & HI_MASK) | (jnp.right_shift(lo, 16) & 0xFFFF)


def _unpack_rows(w):
    lo = pltpu.bitcast(jnp.left_shift(w, 16), F32)
    hi = pltpu.bitcast(w & HI_MASK, F32)
    return lo, hi


def _moe_pre_body(x_ref, g_ref, mod_ref, wr_ref, swg_ref, swu_ref, swd_ref, h_ref, lg_ref, sh_ref):
    h = _norm_mod(x_ref[...], g_ref[...], mod_ref, SH_F, SC_F)
    hb = h.astype(BF16)
    h_ref[...] = _pack_rows(h)
    e = wr_ref.shape[1]
    ff = swu_ref.shape[1]
    r = jnp.dot(hb, swg_ref[...], preferred_element_type=F32)
    h_lo = (h - hb.astype(F32)).astype(BF16)
    lg = r[:, ff:ff + e] + r[:, ff + e:ff + 2 * e] + jnp.dot(h_lo, wr_ref[...], preferred_element_type=F32)
    lg_ref[...] = lg.T
    gt = r[:, :ff]
    up = jnp.dot(hb, swu_ref[...], preferred_element_type=F32)
    mid = (gt * jax.nn.sigmoid(gt) * up).astype(BF16)
    sh_ref[...] = jnp.dot(mid, swd_ref[...], preferred_element_type=F32)


def _moe_pre(x2, g, mod, w_router, swg, swu, swd, seq, tm):
    t, d = x2.shape
    e = w_router.shape[1]
    full = lambda a: pl.BlockSpec(a.shape, lambda i: (0, 0))
    return pl.pallas_call(
        _moe_pre_body,
        grid=(t // tm,),
        in_specs=[pl.BlockSpec((tm, d), lambda i: (i, 0)),
                  pl.BlockSpec((1, d), lambda i: (0, 0)),
                  pl.BlockSpec((1, 6, d), lambda i: (i * tm // seq, 0, 0)),
                  full(w_router), full(swg), full(swu), full(swd)],
        out_specs=[pl.BlockSpec((tm, d // 2), lambda i: (i, 0)),
                   pl.BlockSpec((e, tm), lambda i: (0, i)),
                   pl.BlockSpec((tm, d), lambda i: (i, 0))],
        out_shape=[jax.ShapeDtypeStruct((t, d // 2), jnp.int32),
                   jax.ShapeDtypeStruct((e, t), F32),
                   jax.ShapeDtypeStruct((t, d), F32)],
        compiler_params=_cp(("arbitrary",)),
        name="moe_pre",
    )(x2, g.reshape(1, d), mod, w_router, swg, swu, swd)


def _first_max(v, ids, sentinel):
    m = jnp.max(v, axis=0, keepdims=True)
    first = jnp.min(jnp.where(v == m, ids, sentinel), axis=0, keepdims=True)
    return m, first


def _route_body(lg_ref, bias_ref, eidx_ref, rank_ref, gw_ref, cnt_ref):
    e, tm = lg_ref.shape
    gsz = e // N_GROUPS
    neg = -jnp.inf

    @pl.when(pl.program_id(0) == 0)
    def _():
        cnt_ref[...] = jnp.zeros_like(cnt_ref)

    scores = jax.nn.sigmoid(lg_ref[...])
    choice = scores + bias_ref[:, 0:1]
    row = lax.broadcasted_iota(jnp.int32, (e, tm), 0)
    sub = lax.broadcasted_iota(jnp.int32, (gsz, tm), 0)

    parts = []
    for g in range(N_GROUPS):
        vg = choice[g * gsz:(g + 1) * gsz, :]
        m1, f1 = _first_max(vg, sub, gsz)
        m2 = jnp.max(jnp.where(sub == f1, neg, vg), axis=0, keepdims=True)
        parts.append(jnp.broadcast_to(m1 + m2, (gsz, tm)))
    gscore = jnp.concatenate(parts, axis=0)

    gid = jnp.right_shift(row, gsz.bit_length() - 1)
    v = jnp.full((e, tm), neg, F32)
    for _ in range(TOPK_GROUPS):
        _, fg = _first_max(gscore, gid, N_GROUPS)
        hit = gid == fg
        v = jnp.where(hit, choice, v)
        gscore = jnp.where(hit, neg, gscore)

    picks = []
    self32 = jnp.zeros((e, tm), F32)
    for _ in range(TOP_K):
        _, fe = _first_max(v, row, e)
        hit = row == fe
        picks.append(fe)
        self32 = jnp.where(hit, 1.0, self32)
        v = jnp.where(hit, neg, v)

    picked = self32 * scores
    gwd = picked / jnp.sum(picked, axis=0, keepdims=True) * ROUTED_SCALE

    before = (lax.broadcasted_iota(jnp.int32, (tm, tm), 0)
              < lax.broadcasted_iota(jnp.int32, (tm, tm), 1)).astype(BF16)
    rank = jnp.dot(self32.astype(BF16), before, preferred_element_type=F32) + cnt_ref[:, 0:1]
    cnt_ref[...] = cnt_ref[...] + jnp.sum(self32, axis=1, keepdims=True)

    for k in range(TOP_K):
        hit = row == picks[k]
        eidx_ref[k:k + 1, :] = picks[k]
        rank_ref[k:k + 1, :] = jnp.sum(jnp.where(hit, rank, 0.0), axis=0, keepdims=True).astype(jnp.int32)
        gw_ref[k:k + 1, :] = jnp.sum(jnp.where(hit, gwd, 0.0), axis=0, keepdims=True)


def _route(logits_t, e_bias, tm):
    e, t = logits_t.shape
    kspec = pl.BlockSpec((TOP_K, tm), lambda i: (0, i))
    return pl.pallas_call(
        _route_body,
        grid=(t // tm,),
        in_specs=[pl.BlockSpec((e, tm), lambda i: (0, i)),
                  pl.BlockSpec((e, LANES), lambda i: (0, 0))],
        out_specs=[kspec, kspec, kspec, pl.BlockSpec((e, LANES), lambda i: (0, 0))],
        out_shape=[jax.ShapeDtypeStruct((TOP_K, t), jnp.int32),
                   jax.ShapeDtypeStruct((TOP_K, t), jnp.int32),
                   jax.ShapeDtypeStruct((TOP_K, t), F32),
                   jax.ShapeDtypeStruct((e, LANES), F32)],
        compiler_params=_cp(("arbitrary",)),
        name="moe_route",
    )(logits_t, jnp.broadcast_to(e_bias.astype(F32).reshape(e, 1), (e, LANES)))


def _experts_body(be_ref, nu_ref, x_ref, wg_ref, wu_ref, wd_ref, o_ref, wg_scr, wu_scr, wd_scr):
    i = pl.program_id(0)
    half = x_ref.shape[1]

    @pl.when(i >= nu_ref[0])
    def _():
        o_ref[...] = jnp.zeros_like(o_ref)

    @pl.when(i < nu_ref[0])
    def _():
        prev = be_ref[jnp.maximum(i - 1, 0)]
        changed = jnp.logical_or(i == 0, be_ref[i] != prev)

        @pl.when(changed)
        def _():
            wg_scr[...] = wg_ref[0, 0].astype(BF16)
            wu_scr[...] = wu_ref[0, 0].astype(BF16)
            wd_scr[...] = wd_ref[0, 0].astype(BF16)

        lo, hi = _unpack_rows(x_ref[...])
        lo = lo.astype(BF16)
        hi = hi.astype(BF16)
        gt = (jnp.dot(lo, wg_scr[:half, :], preferred_element_type=F32)
              + jnp.dot(hi, wg_scr[half:, :], preferred_element_type=F32))
        up = (jnp.dot(lo, wu_scr[:half, :], preferred_element_type=F32)
              + jnp.dot(hi, wu_scr[half:, :], preferred_element_type=F32))
        mid = (gt * jax.nn.sigmoid(gt) * up).astype(BF16)
        o_ref[...] = _pack_rows(jnp.dot(mid, wd_scr[...], preferred_element_type=F32))


def _experts(x_sorted, block_exp, n_used, wg, wu, wd, layer, tm):
    n_rows, half = x_sorted.shape
    d, ff = wg.shape[2], wg.shape[3]
    n_blocks = n_rows // tm
    row_map = lambda i, be, nu: (jnp.minimum(i, nu[0] - 1), 0)
    grid_spec = pltpu.PrefetchScalarGridSpec(
        num_scalar_prefetch=2,
        grid=(n_blocks,),
        in_specs=[pl.BlockSpec((tm, half), row_map),
                  pl.BlockSpec((1, 1, d, ff), lambda i, be, nu: (layer, be[i], 0, 0)),
                  pl.BlockSpec((1, 1, d, ff), lambda i, be, nu: (layer, be[i], 0, 0)),
                  pl.BlockSpec((1, 1, ff, d), lambda i, be, nu: (layer, be[i], 0, 0))],
        out_specs=pl.BlockSpec((tm, half), lambda i, be, nu: (i, 0)),
        scratch_shapes=[pltpu.VMEM((d, ff), BF16), pltpu.VMEM((d, ff), BF16), pltpu.VMEM((ff, d), BF16)],
    )
    return pl.pallas_call(
        _experts_body,
        grid_spec=grid_spec,
        out_shape=jax.ShapeDtypeStruct((n_rows, half), jnp.int32),
        compiler_params=_cp(("arbitrary",)),
        name="moe_experts",
    )(block_exp, n_used, x_sorted, wg, wu, wd)


TOK_STEP = 1024
TOK_SUB = 256


def _stage_pos(pos_hbm, pos_smem, sem, step, n_tok):
    copies = [pltpu.make_async_copy(pos_hbm.at[pl.ds(k * n_tok + step * TOK_STEP, TOK_STEP)],
                                    pos_smem.at[pl.ds(k * TOK_STEP, TOK_STEP)], sem)
              for k in range(TOP_K)]
    for cp in copies:
        cp.start()
    for cp in copies:
        cp.wait()


def _dispatch_body(ends_ref, padded_ref, pos_hbm, h_ref, xs_hbm, pos_smem, zero_scr, sem_pos, sem_fill, sem_rows,
                   *, n_tok, tme):
    i, j = pl.program_id(0), pl.program_id(1)
    sub = h_ref.shape[0]

    @pl.when(jnp.logical_and(i == 0, j == 0))
    def _():
        zero_scr[...] = jnp.zeros_like(zero_scr)

        def fill(e):
            start = pl.multiple_of(ends_ref[e] - tme, tme)
            return pltpu.make_async_copy(zero_scr, xs_hbm.at[pl.ds(start, tme)], sem_fill)

        for e in range(N_EXPERTS):
            @pl.when(padded_ref[e] > 0)
            def _():
                fill(e).start()
        for e in range(N_EXPERTS):
            @pl.when(padded_ref[e] > 0)
            def _():
                fill(e).wait()

        def tail(b):
            return pltpu.make_async_copy(zero_scr, xs_hbm.at[pl.ds(pl.multiple_of(b * tme, tme), tme)], sem_fill)

        n_used = ends_ref[N_EXPERTS - 1] // tme
        n_blocks = xs_hbm.shape[0] // tme
        lax.fori_loop(n_used, n_blocks, lambda b, c: (tail(b).start(), c)[1], 0)
        lax.fori_loop(n_used, n_blocks, lambda b, c: (tail(b).wait(), c)[1], 0)

    @pl.when(j == 0)
    def _():
        _stage_pos(pos_hbm, pos_smem, sem_pos, i, n_tok)

    def tok8(g, carry):
        t0 = pl.multiple_of(g * 8, 8)
        for u in range(8):
            for k in range(TOP_K):
                p = pos_smem[k * TOK_STEP + j * sub + t0 + u]
                pltpu.make_async_copy(h_ref.at[t0 + u], xs_hbm.at[p], sem_rows).start(priority=k % 2)
        return carry

    lax.fori_loop(0, sub // 8, tok8, 0)
    for k in range(TOP_K):
        pltpu.make_async_copy(h_ref, xs_hbm.at[pl.ds(0, sub)], sem_rows).wait()


def _dispatch(h, pos_flat, ends, padded, n_rows, tme):
    t, half = h.shape
    sub = min(TOK_SUB, t)
    grid_spec = pltpu.PrefetchScalarGridSpec(
        num_scalar_prefetch=2,
        grid=(t // TOK_STEP, TOK_STEP // sub),
        in_specs=[pl.BlockSpec(memory_space=pl.ANY),
                  pl.BlockSpec((sub, half), lambda i, j, en, pd: (i * (TOK_STEP // sub) + j, 0))],
        out_specs=pl.BlockSpec(memory_space=pl.ANY),
        scratch_shapes=[pltpu.SMEM((TOP_K * TOK_STEP,), jnp.int32),
                        pltpu.VMEM((tme, half), jnp.int32),
                        pltpu.SemaphoreType.DMA, pltpu.SemaphoreType.DMA, pltpu.SemaphoreType.DMA],
    )
    return pl.pallas_call(
        functools.partial(_dispatch_body, n_tok=t, tme=tme),
        grid_spec=grid_spec,
        out_shape=jax.ShapeDtypeStruct((n_rows, half), jnp.int32),
        compiler_params=_cp(("arbitrary", "arbitrary")),
        name="moe_dispatch",
    )(ends, padded, pos_flat, h)


def _combine_body(pos_hbm, ys_hbm, gw_ref, x_ref, sh_ref, mod_ref, fg_ref, o_ref, pos_smem, g_scr,
                  sem_pos, sem_rows, *, n_tok, final_norm):
    i, j = pl.program_id(0), pl.program_id(1)
    sub, d = x_ref.shape
    half = d // 2

    @pl.when(j == 0)
    def _():
        _stage_pos(pos_hbm, pos_smem, sem_pos, i, n_tok)

    def tok8(g, carry):
        t0 = pl.multiple_of(g * 8, 8)
        for u in range(8):
            for k in range(TOP_K):
                p = pos_smem[k * TOK_STEP + j * sub + t0 + u]
                pltpu.make_async_copy(ys_hbm.at[p], g_scr.at[k, t0 + u], sem_rows).start(priority=k % 2)
        return carry

    lax.fori_loop(0, sub // 8, tok8, 0)
    for k in range(TOP_K):
        pltpu.make_async_copy(ys_hbm.at[pl.ds(0, sub)], g_scr.at[k], sem_rows).wait()

    acc_lo = sh_ref[:, :half]
    acc_hi = sh_ref[:, half:]
    for k in range(TOP_K):
        lo, hi = _unpack_rows(g_scr[k])
        w = gw_ref[:, k:k + 1]
        acc_lo = acc_lo + w * lo
        acc_hi = acc_hi + w * hi
    gate = mod_ref[0, G_F:G_F + 1, :]
    y_lo = x_ref[:, :half] + gate[:, :half] * acc_lo
    y_hi = x_ref[:, half:] + gate[:, half:] * acc_hi
    if final_norm:
        ms = (jnp.sum(y_lo * y_lo, axis=-1, keepdims=True) + jnp.sum(y_hi * y_hi, axis=-1, keepdims=True)) / d
        r = lax.rsqrt(ms + 1e-6)
        y_lo = y_lo * r * fg_ref[:, :half]
        y_hi = y_hi * r * fg_ref[:, half:]
    o_ref[:, :half] = y_lo
    o_ref[:, half:] = y_hi


def _combine(ys, pos_flat, gw, x2, shared, mod, final_g, seq, final_norm):
    t, d = x2.shape
    sub = min(TOK_SUB, t)
    nj = TOK_STEP // sub
    row = lambda i, j: (i * nj + j, 0)
    return pl.pallas_call(
        functools.partial(_combine_body, n_tok=t, final_norm=final_norm),
        grid=(t // TOK_STEP, nj),
        in_specs=[pl.BlockSpec(memory_space=pl.ANY),
                  pl.BlockSpec(memory_space=pl.ANY),
                  pl.BlockSpec((sub, TOP_K), row),
                  pl.BlockSpec((sub, d), row),
                  pl.BlockSpec((sub, d), row),
                  pl.BlockSpec((1, 6, d), lambda i, j: ((i * nj + j) * sub // seq, 0, 0)),
                  pl.BlockSpec((1, d), lambda i, j: (0, 0))],
        out_specs=pl.BlockSpec((sub, d), row),
        out_shape=jax.ShapeDtypeStruct((t, d), F32),
        scratch_shapes=[pltpu.SMEM((TOP_K * TOK_STEP,), jnp.int32),
                        pltpu.VMEM((TOP_K, sub, d // 2), jnp.int32),
                        pltpu.SemaphoreType.DMA, pltpu.SemaphoreType.DMA],
        compiler_params=_cp(("arbitrary", "arbitrary")),
        name="moe_combine",
    )(pos_flat, ys, gw, x2, shared, mod, final_g.reshape(1, d))


def _dispatch_plan(eidx_t, rank_t, counts, tm):
    n_assign = eidx_t.size
    padded = (counts + tm - 1) // tm * tm
    ends = jnp.cumsum(padded)
    pstart = ends - padded
    pos_t = rank_t
    for e in range(N_EXPERTS):
        pos_t = pos_t + jnp.where(eidx_t == e, pstart[e], 0)
    n_blocks = -(-n_assign // tm) + N_EXPERTS
    blk_start = jnp.arange(n_blocks, dtype=jnp.int32) * tm
    block_exp = jnp.minimum(jnp.sum((ends[None, :] <= blk_start[:, None]).astype(jnp.int32), axis=1),
                            N_EXPERTS - 1)
    n_used = (ends[-1] // tm).astype(jnp.int32).reshape(1)
    return pos_t.reshape(-1), ends, padded, block_exp, n_used, n_blocks


def _moe(x2, g, mod, w_router, e_bias, wg, wu, wd, layer, swg, swu, swd, final_g, seq, tm, tme, final_norm):
    wr_hi = w_router.astype(BF16)
    wr_lo = (w_router - wr_hi.astype(F32)).astype(BF16)
    gate_and_router = jnp.concatenate([swg.astype(BF16), wr_hi, wr_lo], axis=1)
    h, logits_t, shared = _moe_pre(x2, g, mod, wr_hi, gate_and_router, swu.astype(BF16),
                                   swd.astype(BF16), seq, tm)
    eidx_t, rank_t, gw_t, cnt = _route(logits_t, e_bias, tm)
    counts = cnt[:, 0].astype(jnp.int32)
    pos_flat, ends, padded, block_exp, n_used, n_blocks = _dispatch_plan(eidx_t, rank_t, counts, tme)
    x_sorted = _dispatch(h, pos_flat, ends, padded, n_blocks * tme, tme)
    y_sorted = _experts(x_sorted, block_exp, n_used, wg, wu, wd, layer, tme)
    return _combine(y_sorted, pos_flat, gw_t.T, x2, shared, mod, final_g, seq, final_norm)


def kernel(x, c, positions, mod_w, mod_b, norm_mix, norm_ffn, hyb_w_in, hyb_w_out, lru_conv_w, lru_conv_b, lru_wa, lru_ba, lru_wx, lru_bx, lru_lambda, cfm_w1, cfm_b1, cfm_dw, cfm_dwb, cfm_ln_g, cfm_ln_b, cfm_w2, cfm_b2, moe_router, moe_bias, moe_wg, moe_wu, moe_wd, sh_wg, sh_wu, sh_wd, final_norm):
    batch, seq, d = x.shape
    t = batch * seq
    n_sel = min(INDEX_TOPK, seq // 4)
    tm = min(512, seq)
    tq = min(256, seq)
    tme = MOE_ROWS

    mod_all = _modulation(c, mod_w, mod_b).reshape(mod_w.shape[0], batch, 6, d)
    tab_a = _rope_tables(positions, ROT_ATTN, HEAD_DIM)
    tab_i = _rope_tables(positions, ROT_IDX, IDX_DIM)
    x2 = x.reshape(t, d)
    zero_bias = jnp.zeros((d,), F32)

    mod = mod_all[0]
    w_in = hyb_w_in[0]
    w_main = w_in[:, :MAIN_COLS].astype(BF16)
    w_idx = jnp.pad(w_in[:, MAIN_COLS:], ((0, 0), (0, IDX_PAD - IDX_COLS))).astype(BF16)
    proj = _nm_matmul(x2, norm_mix[0], mod, w_main, seq, tm, MAIN_COLS, BF16, "hyb_in_main")
    idx = _nm_matmul(x2, norm_mix[0], mod, w_idx, seq, tm, IDX_PAD, F32, "hyb_in_idx")
    y_lru = _rg_lru(proj, lru_conv_w[0], lru_conv_b[0], lru_wa[0], lru_ba[0], lru_wx[0], lru_bx[0],
                    lru_lambda[0], batch, seq)
    y_att = _dsa(proj, idx, tab_a, tab_i, batch, seq, n_sel, tq)
    x2 = _proj_residual([y_lru, y_att], hyb_w_out[0].astype(BF16), zero_bias, x2, mod, G_M, seq, tm,
                        "hyb_out")
    x2 = _moe(x2, norm_ffn[0], mod, moe_router[0], moe_bias[0], moe_wg, moe_wu, moe_wd, 0,
              sh_wg[0], sh_wu[0], sh_wd[0], final_norm, seq, tm, tme, False)

    mod = mod_all[1]
    u = _nm_glu(x2, norm_mix[1], mod, cfm_w1[0].astype(BF16), cfm_b1[0], seq, tm, d)
    z = _cfm_conv(u, cfm_dw[0], cfm_dwb[0], cfm_ln_g[0], cfm_ln_b[0], seq, tm)
    x2 = _proj_residual([z], cfm_w2[0].astype(BF16), cfm_b2[0], x2, mod, G_M, seq, tm, "cfm_out")
    out = _moe(x2, norm_ffn[1], mod, moe_router[1], moe_bias[1], moe_wg, moe_wu, moe_wd, 1,
               sh_wg[1], sh_wu[1], sh_wd[1], final_norm, seq, tm, tme, True)
    return out.reshape(batch, seq, d)
```

```python
import functools

import jax
import jax.numpy as jnp
from jax import lax
from jax.experimental import pallas as pl
from jax.experimental.pallas import tpu as pltpu

F32 = jnp.float32
BF16 = jnp.bfloat16

LRU_WIDTH = 1024
LRU_BW = 128
LRU_CONV = 4
LRU_C = 8.0
N_HEADS = 8
N_KV_HEADS = 2
HEAD_DIM = 128
ATT_WIDTH = N_HEADS * HEAD_DIM
KV_WIDTH = N_KV_HEADS * HEAD_DIM
IDX_HEADS = 8
IDX_DIM = 64
INDEX_TOPK = 256
ROPE_THETA = 500000.0
ROT_ATTN = HEAD_DIM // 4
ROT_IDX = IDX_DIM // 4
CFM_CONV = 31
N_EXPERTS = 64
TOP_K = 8
N_GROUPS = 8
TOPK_GROUPS = 4
ROUTED_SCALE = 2.5

MAIN_COLS = 2 * LRU_WIDTH + ATT_WIDTH + 2 * KV_WIDTH
IDX_COLS = IDX_HEADS * IDX_DIM + IDX_DIM + IDX_HEADS
IDX_PAD = 640

MOE_ROWS = 512
LANES = 128
VMEM_LIMIT = 56 * 1024 * 1024
INT_MIN = -2147483648
NEG_BIG = -1e30

SH_M, SC_M, G_M, SH_F, SC_F, G_F = range(6)


def _cp(sem):
    return pltpu.CompilerParams(dimension_semantics=sem, vmem_limit_bytes=VMEM_LIMIT)


def _weight_spec(block, index_map, n_col_blocks):
    if n_col_blocks == 1:
        return pl.BlockSpec(block, index_map, pipeline_mode=pl.Buffered(1))
    return pl.BlockSpec(block, index_map)


def _norm_mod(x, g, mod_ref, sh_row, sc_row):
    ms = jnp.mean(x * x, axis=-1, keepdims=True)
    y = x * lax.rsqrt(ms + 1e-6) * g
    return y * (1.0 + mod_ref[0, sc_row:sc_row + 1, :]) + mod_ref[0, sh_row:sh_row + 1, :]


def _mod_body(c_ref, w_ref, b_ref, o_ref):
    c = c_ref[...]
    cond = c * jax.nn.sigmoid(c)
    o_ref[0] = jnp.dot(cond, w_ref[0], preferred_element_type=F32,
                       precision=lax.Precision.HIGHEST) + b_ref[0]


def _modulation(c, mod_w, mod_b):
    depth, d, n = mod_w.shape
    b = c.shape[0]
    tn = 512
    return pl.pallas_call(
        _mod_body,
        grid=(depth, n // tn),
        in_specs=[pl.BlockSpec((b, d), lambda l, j: (0, 0)),
                  pl.BlockSpec((1, d, tn), lambda l, j: (l, 0, j)),
                  pl.BlockSpec((1, 1, tn), lambda l, j: (l, 0, j))],
        out_specs=pl.BlockSpec((1, b, tn), lambda l, j: (l, 0, j)),
        out_shape=jax.ShapeDtypeStruct((depth, b, n), F32),
        compiler_params=_cp(("arbitrary", "arbitrary")),
        name="modulation",
    )(c, mod_w, mod_b.reshape(depth, 1, n))


def _nm_matmul_body(x_ref, g_ref, mod_ref, w_ref, o_ref, h_scr):
    @pl.when(pl.program_id(1) == 0)
    def _():
        h_scr[...] = _norm_mod(x_ref[...], g_ref[...], mod_ref, SH_M, SC_M).astype(BF16)

    o_ref[...] = jnp.dot(h_scr[...], w_ref[...], preferred_element_type=F32).astype(o_ref.dtype)


def _nm_matmul(x2, g, mod, w, seq, tm, tn, out_dtype, name):
    t, d = x2.shape
    n = w.shape[1]
    return pl.pallas_call(
        _nm_matmul_body,
        grid=(t // tm, n // tn),
        in_specs=[pl.BlockSpec((tm, d), lambda i, j: (i, 0)),
                  pl.BlockSpec((1, d), lambda i, j: (0, 0)),
                  pl.BlockSpec((1, 6, d), lambda i, j: (i * tm // seq, 0, 0)),
                  _weight_spec((d, tn), lambda i, j: (0, j), n // tn)],
        out_specs=pl.BlockSpec((tm, tn), lambda i, j: (i, j)),
        out_shape=jax.ShapeDtypeStruct((t, n), out_dtype),
        scratch_shapes=[pltpu.VMEM((tm, d), BF16)],
        compiler_params=_cp(("arbitrary", "arbitrary")),
        name=name,
    )(x2, g.reshape(1, d), mod, w)


def _nm_glu_body(x_ref, g_ref, mod_ref, wa_ref, wg_ref, ba_ref, bg_ref, o_ref, h_scr):
    @pl.when(pl.program_id(1) == 0)
    def _():
        h_scr[...] = _norm_mod(x_ref[...], g_ref[...], mod_ref, SH_M, SC_M).astype(BF16)

    h = h_scr[...]
    a = jnp.dot(h, wa_ref[...], preferred_element_type=F32) + ba_ref[...]
    gt = jnp.dot(h, wg_ref[...], preferred_element_type=F32) + bg_ref[...]
    o_ref[...] = (a * jax.nn.sigmoid(gt)).astype(o_ref.dtype)


def _nm_glu(x2, g, mod, w1, b1, seq, tm, tn):
    t, d = x2.shape
    n = w1.shape[1] // 2
    nj = n // tn
    return pl.pallas_call(
        _nm_glu_body,
        grid=(t // tm, nj),
        in_specs=[pl.BlockSpec((tm, d), lambda i, j: (i, 0)),
                  pl.BlockSpec((1, d), lambda i, j: (0, 0)),
                  pl.BlockSpec((1, 6, d), lambda i, j: (i * tm // seq, 0, 0)),
                  _weight_spec((d, tn), lambda i, j: (0, j), nj),
                  _weight_spec((d, tn), lambda i, j: (0, j + nj), nj),
                  pl.BlockSpec((1, tn), lambda i, j: (0, j)),
                  pl.BlockSpec((1, tn), lambda i, j: (0, j + nj))],
        out_specs=pl.BlockSpec((tm, tn), lambda i, j: (i, j)),
        out_shape=jax.ShapeDtypeStruct((t, n), BF16),
        scratch_shapes=[pltpu.VMEM((tm, d), BF16)],
        compiler_params=_cp(("arbitrary", "arbitrary")),
        name="cfm_in_glu",
    )(x2, g.reshape(1, d), mod, w1, w1, b1.reshape(1, 2 * n), b1.reshape(1, 2 * n))


def _proj_res_body(*refs, n_parts, gate_row):
    part_refs = refs[:n_parts]
    w_ref, b_ref, x_ref, mod_ref, o_ref = refs[n_parts:]
    acc = None
    off = 0
    for p in part_refs:
        k = p.shape[1]
        y = jnp.dot(p[...], w_ref[off:off + k, :], preferred_element_type=F32)
        acc = y if acc is None else acc + y
        off += k
    y = acc + b_ref[...]
    o_ref[...] = x_ref[...] + mod_ref[0, gate_row:gate_row + 1, :] * y


def _proj_residual(parts, w, bias, x2, mod, gate_row, seq, tm, name):
    t, d = x2.shape
    n = w.shape[1]
    in_specs = [pl.BlockSpec((tm, p.shape[1]), lambda i: (i, 0)) for p in parts]
    in_specs += [pl.BlockSpec(w.shape, lambda i: (0, 0)),
                 pl.BlockSpec((1, n), lambda i: (0, 0)),
                 pl.BlockSpec((tm, d), lambda i: (i, 0)),
                 pl.BlockSpec((1, 6, d), lambda i: (i * tm // seq, 0, 0))]
    return pl.pallas_call(
        functools.partial(_proj_res_body, n_parts=len(parts), gate_row=gate_row),
        grid=(t // tm,),
        in_specs=in_specs,
        out_specs=pl.BlockSpec((tm, n), lambda i: (i, 0)),
        out_shape=jax.ShapeDtypeStruct((t, n), F32),
        compiler_params=_cp(("arbitrary",)),
        name=name,
    )(*parts, w, bias.reshape(1, n), x2, mod)


def _shift_rows(x, d, fill, row):
    return jnp.where(row >= d, pltpu.roll(x, d, 0), fill)


def _lru_body(lx_ref, lg_ref, cw_ref, cb_ref, wa_ref, ba_ref, wx_ref, bx_ref, lam_ref, o_ref):
    s, cw = lx_ref.shape
    x = lx_ref[...].astype(F32)
    row = lax.broadcasted_iota(jnp.int32, (s, cw), 0)
    xc = x * cw_ref[LRU_CONV - 1:LRU_CONV, :] + cb_ref[...]
    for d in range(1, LRU_CONV):
        xc = xc + _shift_rows(x, d, 0.0, row) * cw_ref[LRU_CONV - 1 - d:LRU_CONV - d, :]
    xb = xc.astype(BF16)
    ga = jax.nn.sigmoid(jnp.dot(xb, wa_ref[0].astype(BF16), preferred_element_type=F32) + ba_ref[...])
    gx = jax.nn.sigmoid(jnp.dot(xb, wx_ref[0].astype(BF16), preferred_element_type=F32) + bx_ref[...])
    z = -lam_ref[...]
    softplus = jnp.maximum(z, 0.0) + jnp.log(1.0 + jnp.exp(-jnp.abs(z)))
    log_a = (-LRU_C) * ga * softplus
    a = jnp.exp(log_a)
    mult = jnp.sqrt(1.0 - a * a)
    mult = jnp.where(row == 0, 1.0, mult)
    b = gx * xc * mult
    d = 1
    while d < s:
        a_sh = _shift_rows(a, d, 1.0, row)
        b_sh = _shift_rows(b, d, 0.0, row)
        b = a * b_sh + b
        a = a * a_sh
        d *= 2
    lg = lg_ref[...].astype(F32)
    gelu = 0.5 * lg * (1.0 + jnp.tanh(0.7978845608028654 * (lg + 0.044715 * lg * lg * lg)))
    o_ref[...] = (b * gelu).astype(o_ref.dtype)


def _rg_lru(proj, conv_w, conv_b, wa, ba, wx, bx, lam, batch, seq):
    cw = LRU_BW
    nc = LRU_WIDTH // cw
    vec = lambda v: v.reshape(1, LRU_WIDTH)
    vspec = pl.BlockSpec((1, cw), lambda b, c: (0, c))
    return pl.pallas_call(
        _lru_body,
        grid=(batch, nc),
        in_specs=[pl.BlockSpec((seq, cw), lambda b, c: (b, c)),
                  pl.BlockSpec((seq, cw), lambda b, c: (b, c + nc)),
                  pl.BlockSpec((LRU_CONV, cw), lambda b, c: (0, c)),
                  vspec,
                  pl.BlockSpec((1, cw, cw), lambda b, c: (c, 0, 0)),
                  vspec,
                  pl.BlockSpec((1, cw, cw), lambda b, c: (c, 0, 0)),
                  vspec, vspec],
        out_specs=pl.BlockSpec((seq, cw), lambda b, c: (b, c)),
        out_shape=jax.ShapeDtypeStruct((batch * seq, LRU_WIDTH), BF16),
        compiler_params=_cp(("arbitrary", "arbitrary")),
        name="rg_lru",
    )(proj, proj, conv_w, vec(conv_b), wa, vec(ba), wx, vec(bx), vec(lam))


def _rope(x, c, s1, s2, half):
    w = x.shape[1]
    return x * c + pltpu.roll(x, w - half, 1) * s1 + pltpu.roll(x, half, 1) * s2


def _sort_key(x):
    bits = pltpu.bitcast(x, jnp.int32)
    return bits ^ (jnp.right_shift(bits, 31) & 0x7FFFFFFF)


def _dsa_tile(sk, q_ref, va_scr, iq_ref, ta_q_ref, ti_q_ref, o_ref, kr_scr, ikr_scr, key_scr, bias_scr, n_sel):
    qi = pl.program_id(1)
    tq = q_ref.shape[0]
    per_group = N_HEADS // N_KV_HEADS

    ci, si1, si2 = ti_q_ref[0], ti_q_ref[1], ti_q_ref[2]
    w_scale = (IDX_HEADS ** -0.5) * (IDX_DIM ** -0.5)
    iw = iq_ref[:, IDX_HEADS * IDX_DIM + IDX_DIM:IDX_HEADS * IDX_DIM + LANES] * w_scale
    ikr = ikr_scr[:sk, :]
    score = jnp.zeros((tq, sk), F32)
    for hp in range(IDX_HEADS // 2):
        pair = _rope(iq_ref[:, hp * LANES:(hp + 1) * LANES], ci, si1, si2, ROT_IDX // 2).astype(BF16)
        for sub in range(2):
            h = 2 * hp + sub
            qh = pair[:, sub * IDX_DIM:(sub + 1) * IDX_DIM]
            dots = lax.dot_general(qh, ikr, (((1,), (1,)), ((), ())), preferred_element_type=F32)
            score = score + jnp.maximum(dots, 0.0) * iw[:, h:h + 1]

    t_row = qi * tq + lax.broadcasted_iota(jnp.int32, (tq, sk), 0)
    col = lax.broadcasted_iota(jnp.int32, (tq, sk), 1)
    causal = col <= t_row
    key_scr[:, :sk] = jnp.where(causal, _sort_key(score), INT_MIN)

    kf = float(n_sel)

    def count_ge(cand):
        return jnp.sum((key_scr[:, :sk] >= cand).astype(F32), axis=1, keepdims=True)

    tau0 = jnp.where(count_ge(jnp.zeros((tq, 1), jnp.int32)) >= kf, 0, INT_MIN).astype(jnp.int32)

    def bis(i, tau):
        cand = tau + jnp.left_shift(jnp.int32(1), 30 - i)
        return jnp.where(count_ge(cand) >= kf, cand, tau)

    tau = lax.fori_loop(0, 31, bis, tau0)

    keys = key_scr[:, :sk]
    n_gt = jnp.sum((keys > tau).astype(F32), axis=1, keepdims=True)
    n_ge = jnp.sum((keys >= tau).astype(F32), axis=1, keepdims=True)
    need = kf - n_gt
    tie = keys == tau
    excess = jnp.logical_and(n_ge > kf, tau > INT_MIN)
    any_excess = jnp.max(excess.astype(F32)) > 0.0

    bias_scr[:, :sk] = jnp.where(jnp.logical_and(keys >= tau, causal), 0.0, NEG_BIG)

    @pl.when(any_excess)
    def _():
        nbits = sk.bit_length()

        def jb(i, jcur):
            cand = jcur + jnp.left_shift(jnp.int32(1), nbits - 1 - i)
            cnt = jnp.sum(jnp.logical_and(tie, col < cand).astype(F32), axis=1, keepdims=True)
            return jnp.where(cnt <= need, cand, jcur)

        jlim = lax.fori_loop(0, nbits, jb, jnp.zeros((tq, 1), jnp.int32))
        jlim = jnp.where(excess, jlim, sk)
        sel = jnp.logical_or(keys > tau, jnp.logical_and(tie, col < jlim))
        bias_scr[:, :sk] = jnp.where(jnp.logical_and(sel, causal), 0.0, NEG_BIG)

    ca, sa1, sa2 = ta_q_ref[0], ta_q_ref[1], ta_q_ref[2]
    scale = HEAD_DIM ** -0.5
    for h in range(N_HEADS):
        g = h // per_group
        qh = _rope(q_ref[:, h * HEAD_DIM:(h + 1) * HEAD_DIM].astype(F32), ca, sa1, sa2, ROT_ATTN // 2)
        qh = (qh * scale).astype(BF16)
        kg = kr_scr[:sk, g * HEAD_DIM:(g + 1) * HEAD_DIM]
        logits = lax.dot_general(qh, kg, (((1,), (1,)), ((), ())), preferred_element_type=F32)
        logits = logits + bias_scr[:, :sk]
        m = jnp.max(logits, axis=1, keepdims=True)
        p = jnp.exp(logits - m).astype(BF16)
        pv = jnp.dot(p, va_scr[:sk, 2 * g * HEAD_DIM:(2 * g + 2) * HEAD_DIM], preferred_element_type=F32)
        o = pv[:, :HEAD_DIM] / pv[:, HEAD_DIM:HEAD_DIM + 1]
        o_ref[:, h * HEAD_DIM:(h + 1) * HEAD_DIM] = o.astype(o_ref.dtype)


def _dsa_body(q_ref, k_ref, v_ref, iq_ref, ik_ref, ta_q_ref, ta_k_ref, ti_q_ref, ti_k_ref,
              o_ref, kr_scr, va_scr, ikr_scr, key_scr, bias_scr, *, n_sel):
    qi = pl.program_id(1)
    s = k_ref.shape[0]

    @pl.when(qi == 0)
    def _():
        ca, sa1, sa2 = ta_k_ref[0], ta_k_ref[1], ta_k_ref[2]
        for g in range(N_KV_HEADS):
            kg = k_ref[:, g * HEAD_DIM:(g + 1) * HEAD_DIM].astype(F32)
            kr_scr[:, g * HEAD_DIM:(g + 1) * HEAD_DIM] = _rope(kg, ca, sa1, sa2, ROT_ATTN // 2).astype(BF16)
            va_scr[:, 2 * g * HEAD_DIM:(2 * g + 1) * HEAD_DIM] = v_ref[:, g * HEAD_DIM:(g + 1) * HEAD_DIM]
            va_scr[:, (2 * g + 1) * HEAD_DIM:(2 * g + 2) * HEAD_DIM] = jnp.ones((s, HEAD_DIM), BF16)
        ikp = _rope(ik_ref[...], ti_k_ref[0], ti_k_ref[1], ti_k_ref[2], ROT_IDX // 2)
        ikr_scr[...] = ikp[:, :IDX_DIM].astype(BF16)

    _dsa_tile(s, q_ref, va_scr, iq_ref, ta_q_ref, ti_q_ref, o_ref, kr_scr, ikr_scr, key_scr, bias_scr, n_sel)


def _dsa(proj, idx, tab_a, tab_i, batch, seq, n_sel, tq):
    nq = seq // tq
    q_col = 2 * LRU_WIDTH // ATT_WIDTH
    k_col = (2 * LRU_WIDTH + ATT_WIDTH) // KV_WIDTH
    ik_col = IDX_HEADS * IDX_DIM // LANES
    return pl.pallas_call(
        functools.partial(_dsa_body, n_sel=n_sel),
        grid=(batch, nq),
        in_specs=[pl.BlockSpec((tq, ATT_WIDTH), lambda b, i: (b * nq + i, q_col)),
                  pl.BlockSpec((seq, KV_WIDTH), lambda b, i: (b, k_col)),
                  pl.BlockSpec((seq, KV_WIDTH), lambda b, i: (b, k_col + 1)),
                  pl.BlockSpec((tq, IDX_PAD), lambda b, i: (b * nq + i, 0)),
                  pl.BlockSpec((seq, LANES), lambda b, i: (b, ik_col)),
                  pl.BlockSpec((3, tq, LANES), lambda b, i: (0, b * nq + i, 0)),
                  pl.BlockSpec((3, seq, LANES), lambda b, i: (0, b, 0)),
                  pl.BlockSpec((3, tq, LANES), lambda b, i: (0, b * nq + i, 0)),
                  pl.BlockSpec((3, seq, LANES), lambda b, i: (0, b, 0))],
        out_specs=pl.BlockSpec((tq, ATT_WIDTH), lambda b, i: (b * nq + i, 0)),
        out_shape=jax.ShapeDtypeStruct((batch * seq, ATT_WIDTH), BF16),
        scratch_shapes=[pltpu.VMEM((seq, KV_WIDTH), BF16),
                        pltpu.VMEM((seq, 2 * KV_WIDTH), BF16),
                        pltpu.VMEM((seq, IDX_DIM), BF16),
                        pltpu.VMEM((tq, seq), jnp.int32),
                        pltpu.VMEM((tq, seq), F32)],
        compiler_params=_cp(("arbitrary", "arbitrary")),
        name="dsa_attention",
    )(proj, proj, proj, idx, idx, tab_a, tab_a, tab_i, tab_i)


def _rope_tables(positions, rot_dim, period):
    half = rot_dim // 2
    inv = jnp.power(jnp.float32(ROPE_THETA), -jnp.arange(0, rot_dim, 2, dtype=F32) / rot_dim)
    ang = positions.astype(F32).reshape(-1, 1) * inv
    cos, sin = jnp.cos(ang), jnp.sin(ang)
    t = cos.shape[0]
    ones = jnp.ones((t, period - rot_dim), F32)
    zeros = jnp.zeros((t, period - rot_dim), F32)
    zh = jnp.zeros((t, half), F32)
    c = jnp.concatenate([cos, cos, ones], axis=1)
    s1 = jnp.concatenate([-sin, zh, zeros], axis=1)
    s2 = jnp.concatenate([zh, sin, zeros], axis=1)
    reps = LANES // period
    return jnp.stack([jnp.tile(c, (1, reps)), jnp.tile(s1, (1, reps)), jnp.tile(s2, (1, reps))])


HALO = 32


def _cfm_conv_body(u_ref, halo_ref, dw_ref, dwb_ref, g_ref, b_ref, o_ref, ext_scr, shift_scr, acc_scr,
                   *, seq, rc, cc):
    tm, c = u_ref.shape
    i = pl.program_id(0)
    at_start = (i * tm) % seq == 0
    ext_scr[0:HALO, :] = jnp.where(at_start, jnp.zeros_like(halo_ref), halo_ref[...])
    ext_scr[HALO:, :] = u_ref[...]
    base = HALO - (CFM_CONV - 1)
    n = rc + HALO

    m = lax.broadcasted_iota(jnp.int32, (8 * n, n), 0)
    col = lax.broadcasted_iota(jnp.int32, (8 * n, n), 1)
    log_n = n.bit_length() - 1
    shift_scr[...] = (col == (m & (n - 1)) + jnp.right_shift(m, log_n)).astype(BF16)

    def chunk(r, carry):
        r0 = pl.multiple_of(r * rc, rc)
        for c0 in range(0, c, cc):
            win = ext_scr[pl.ds(r0, n), c0:c0 + cc]
            shifted = jnp.dot(shift_scr[...], win, preferred_element_type=F32)
            acc = jnp.zeros((rc, cc), F32) + dwb_ref[:, c0:c0 + cc]
            for j in range(CFM_CONV):
                off = base + j
                sub, a0 = off % 8, off - off % 8
                acc = acc + shifted[sub * n + a0:sub * n + a0 + rc, :] * dw_ref[j:j + 1, c0:c0 + cc]
            acc_scr[pl.ds(r0, rc), c0:c0 + cc] = acc
        return carry

    lax.fori_loop(0, tm // rc, chunk, 0)
    y = acc_scr[...]
    mu = jnp.mean(y, axis=-1, keepdims=True)
    yc = y - mu
    var = jnp.mean(yc * yc, axis=-1, keepdims=True)
    z = yc * lax.rsqrt(var + 1e-5) * g_ref[...] + b_ref[...]
    o_ref[...] = (z * jax.nn.sigmoid(z)).astype(o_ref.dtype)


def _cfm_conv(u, dw, dwb, ln_g, ln_b, seq, tm):
    t, c = u.shape
    hb = tm // HALO
    vec = lambda v: v.reshape(1, c)
    vspec = pl.BlockSpec((1, c), lambda i: (0, 0))
    rc = 32
    return pl.pallas_call(
        functools.partial(_cfm_conv_body, seq=seq, rc=rc, cc=256),
        grid=(t // tm,),
        in_specs=[pl.BlockSpec((tm, c), lambda i: (i, 0)),
                  pl.BlockSpec((HALO, c), lambda i: (jnp.maximum(i * hb - 1, 0), 0)),
                  pl.BlockSpec((CFM_CONV, c), lambda i: (0, 0)),
                  vspec, vspec, vspec],
        out_specs=pl.BlockSpec((tm, c), lambda i: (i, 0)),
        out_shape=jax.ShapeDtypeStruct((t, c), BF16),
        scratch_shapes=[pltpu.VMEM((tm + HALO, c), BF16), pltpu.VMEM((8 * (rc + HALO), rc + HALO), BF16),
                        pltpu.VMEM((tm, c), F32)],
        compiler_params=_cp(("arbitrary",)),
        name="cfm_conv_ln",
    )(u, u, dw, vec(dwb), vec(ln_g), vec(ln_b))


HI_MASK = -65536


def _pack_rows(y):
    c = y.shape[1] // 2
    lo = pltpu.bitcast(y[:, :c].astype(BF16).astype(F32), jnp.int32)
    hi = pltpu.bitcast(y[:, c:].astype(BF16).astype(F32), jnp.int32)
    return (hi & HI_MASK) | (jnp.right_shift(lo, 16) & 0xFFFF)


def _unpack_rows(w):
    lo = pltpu.bitcast(jnp.left_shift(w, 16), F32)
    hi = pltpu.bitcast(w & HI_MASK, F32)
    return lo, hi


def _moe_pre_body(x_ref, g_ref, mod_ref, wr_ref, swg_ref, swu_ref, swd_ref, h_ref, lg_ref, sh_ref):
    h = _norm_mod(x_ref[...], g_ref[...], mod_ref, SH_F, SC_F)
    hb = h.astype(BF16)
    h_ref[...] = _pack_rows(h)
    e = wr_ref.shape[1]
    ff = swu_ref.shape[1]
    r = jnp.dot(hb, swg_ref[...], preferred_element_type=F32)
    h_lo = (h - hb.astype(F32)).astype(BF16)
    lg = r[:, ff:ff + e] + r[:, ff + e:ff + 2 * e] + jnp.dot(h_lo, wr_ref[...], preferred_element_type=F32)
    lg_ref[...] = lg.T
    gt = r[:, :ff]
    up = jnp.dot(hb, swu_ref[...], preferred_element_type=F32)
    mid = (gt * jax.nn.sigmoid(gt) * up).astype(BF16)
    sh_ref[...] = jnp.dot(mid, swd_ref[...], preferred_element_type=F32)


def _moe_pre(x2, g, mod, w_router, swg, swu, swd, seq, tm):
    t, d = x2.shape
    e = w_router.shape[1]
    full = lambda a: pl.BlockSpec(a.shape, lambda i: (0, 0))
    return pl.pallas_call(
        _moe_pre_body,
        grid=(t // tm,),
        in_specs=[pl.BlockSpec((tm, d), lambda i: (i, 0)),
                  pl.BlockSpec((1, d), lambda i: (0, 0)),
                  pl.BlockSpec((1, 6, d), lambda i: (i * tm // seq, 0, 0)),
                  full(w_router), full(swg), full(swu), full(swd)],
        out_specs=[pl.BlockSpec((tm, d // 2), lambda i: (i, 0)),
                   pl.BlockSpec((e, tm), lambda i: (0, i)),
                   pl.BlockSpec((tm, d), lambda i: (i, 0))],
        out_shape=[jax.ShapeDtypeStruct((t, d // 2), jnp.int32),
                   jax.ShapeDtypeStruct((e, t), F32),
                   jax.ShapeDtypeStruct((t, d), F32)],
        compiler_params=_cp(("arbitrary",)),
        name="moe_pre",
    )(x2, g.reshape(1, d), mod, w_router, swg, swu, swd)


def _first_max(v, ids, sentinel):
    m = jnp.max(v, axis=0, keepdims=True)
    first = jnp.min(jnp.where(v == m, ids, sentinel), axis=0, keepdims=True)
    return m, first


def _route_body(lg_ref, bias_ref, eidx_ref, rank_ref, gw_ref, cnt_ref):
    e, tm = lg_ref.shape
    gsz = e // N_GROUPS
    neg = -jnp.inf

    @pl.when(pl.program_id(0) == 0)
    def _():
        cnt_ref[...] = jnp.zeros_like(cnt_ref)

    scores = jax.nn.sigmoid(lg_ref[...])
    choice = scores + bias_ref[:, 0:1]
    row = lax.broadcasted_iota(jnp.int32, (e, tm), 0)
    sub = lax.broadcasted_iota(jnp.int32, (gsz, tm), 0)

    parts = []
    for g in range(N_GROUPS):
        vg = choice[g * gsz:(g + 1) * gsz, :]
        m1, f1 = _first_max(vg, sub, gsz)
        m2 = jnp.max(jnp.where(sub == f1, neg, vg), axis=0, keepdims=True)
        parts.append(jnp.broadcast_to(m1 + m2, (gsz, tm)))
    gscore = jnp.concatenate(parts, axis=0)

    gid = jnp.right_shift(row, gsz.bit_length() - 1)
    v = jnp.full((e, tm), neg, F32)
    for _ in range(TOPK_GROUPS):
        _, fg = _first_max(gscore, gid, N_GROUPS)
        hit = gid == fg
        v = jnp.where(hit, choice, v)
        gscore = jnp.where(hit, neg, gscore)

    picks = []
    self32 = jnp.zeros((e, tm), F32)
    for _ in range(TOP_K):
        _, fe = _first_max(v, row, e)
        hit = row == fe
        picks.append(fe)
        self32 = jnp.where(hit, 1.0, self32)
        v = jnp.where(hit, neg, v)

    picked = self32 * scores
    gwd = picked / jnp.sum(picked, axis=0, keepdims=True) * ROUTED_SCALE

    before = (lax.broadcasted_iota(jnp.int32, (tm, tm), 0)
              < lax.broadcasted_iota(jnp.int32, (tm, tm), 1)).astype(BF16)
    rank = jnp.dot(self32.astype(BF16), before, preferred_element_type=F32) + cnt_ref[:, 0:1]
    cnt_ref[...] = cnt_ref[...] + jnp.sum(self32, axis=1, keepdims=True)

    for k in range(TOP_K):
        hit = row == picks[k]
        eidx_ref[k:k + 1, :] = picks[k]
        rank_ref[k:k + 1, :] = jnp.sum(jnp.where(hit, rank, 0.0), axis=0, keepdims=True).astype(jnp.int32)
        gw_ref[k:k + 1, :] = jnp.sum(jnp.where(hit, gwd, 0.0), axis=0, keepdims=True)


def _route(logits_t, e_bias, tm):
    e, t = logits_t.shape
    kspec = pl.BlockSpec((TOP_K, tm), lambda i: (0, i))
    return pl.pallas_call(
        _route_body,
        grid=(t // tm,),
        in_specs=[pl.BlockSpec((e, tm), lambda i: (0, i)),
                  pl.BlockSpec((e, LANES), lambda i: (0, 0))],
        out_specs=[kspec, kspec, kspec, pl.BlockSpec((e, LANES), lambda i: (0, 0))],
        out_shape=[jax.ShapeDtypeStruct((TOP_K, t), jnp.int32),
                   jax.ShapeDtypeStruct((TOP_K, t), jnp.int32),
                   jax.ShapeDtypeStruct((TOP_K, t), F32),
                   jax.ShapeDtypeStruct((e, LANES), F32)],
        compiler_params=_cp(("arbitrary",)),
        name="moe_route",
    )(logits_t, jnp.broadcast_to(e_bias.astype(F32).reshape(e, 1), (e, LANES)))


def _experts_body(be_ref, nu_ref, x_ref, wg_ref, wu_ref, wd_ref, o_ref, wg_scr, wu_scr, wd_scr):
    i = pl.program_id(0)
    half = x_ref.shape[1]

    @pl.when(i >= nu_ref[0])
    def _():
        o_ref[...] = jnp.zeros_like(o_ref)

    @pl.when(i < nu_ref[0])
    def _():
        prev = be_ref[jnp.maximum(i - 1, 0)]
        changed = jnp.logical_or(i == 0, be_ref[i] != prev)

        @pl.when(changed)
        def _():
            wg_scr[...] = wg_ref[0, 0].astype(BF16)
            wu_scr[...] = wu_ref[0, 0].astype(BF16)
            wd_scr[...] = wd_ref[0, 0].astype(BF16)

        lo, hi = _unpack_rows(x_ref[...])
        lo = lo.astype(BF16)
        hi = hi.astype(BF16)
        gt = (jnp.dot(lo, wg_scr[:half, :], preferred_element_type=F32)
              + jnp.dot(hi, wg_scr[half:, :], preferred_element_type=F32))
        up = (jnp.dot(lo, wu_scr[:half, :], preferred_element_type=F32)
              + jnp.dot(hi, wu_scr[half:, :], preferred_element_type=F32))
        mid = (gt * jax.nn.sigmoid(gt) * up).astype(BF16)
        o_ref[...] = _pack_rows(jnp.dot(mid, wd_scr[...], preferred_element_type=F32))


def _experts(x_sorted, block_exp, n_used, wg, wu, wd, layer, tm):
    n_rows, half = x_sorted.shape
    d, ff = wg.shape[2], wg.shape[3]
    n_blocks = n_rows // tm
    row_map = lambda i, be, nu: (jnp.minimum(i, nu[0] - 1), 0)
    grid_spec = pltpu.PrefetchScalarGridSpec(
        num_scalar_prefetch=2,
        grid=(n_blocks,),
        in_specs=[pl.BlockSpec((tm, half), row_map),
                  pl.BlockSpec((1, 1, d, ff), lambda i, be, nu: (layer, be[i], 0, 0)),
                  pl.BlockSpec((1, 1, d, ff), lambda i, be, nu: (layer, be[i], 0, 0)),
                  pl.BlockSpec((1, 1, ff, d), lambda i, be, nu: (layer, be[i], 0, 0))],
        out_specs=pl.BlockSpec((tm, half), lambda i, be, nu: (i, 0)),
        scratch_shapes=[pltpu.VMEM((d, ff), BF16), pltpu.VMEM((d, ff), BF16), pltpu.VMEM((ff, d), BF16)],
    )
    return pl.pallas_call(
        _experts_body,
        grid_spec=grid_spec,
        out_shape=jax.ShapeDtypeStruct((n_rows, half), jnp.int32),
        compiler_params=_cp(("arbitrary",)),
        name="moe_experts",
    )(block_exp, n_used, x_sorted, wg, wu, wd)


TOK_STEP = 1024
TOK_SUB = 256


def _stage_pos(pos_hbm, pos_smem, sem, step, n_tok):
    copies = [pltpu.make_async_copy(pos_hbm.at[pl.ds(k * n_tok + step * TOK_STEP, TOK_STEP)],
                                    pos_smem.at[pl.ds(k * TOK_STEP, TOK_STEP)], sem)
              for k in range(TOP_K)]
    for cp in copies:
        cp.start()
    for cp in copies:
        cp.wait()


def _dispatch_body(ends_ref, padded_ref, pos_hbm, h_ref, xs_hbm, pos_smem, zero_scr, sem_pos, sem_fill, sem_rows,
                   *, n_tok, tme):
    i, j = pl.program_id(0), pl.program_id(1)
    sub = h_ref.shape[0]

    @pl.when(jnp.logical_and(i == 0, j == 0))
    def _():
        zero_scr[...] = jnp.zeros_like(zero_scr)

        def fill(e):
            start = pl.multiple_of(ends_ref[e] - tme, tme)
            return pltpu.make_async_copy(zero_scr, xs_hbm.at[pl.ds(start, tme)], sem_fill)

        for e in range(N_EXPERTS):
            @pl.when(padded_ref[e] > 0)
            def _():
                fill(e).start()
        for e in range(N_EXPERTS):
            @pl.when(padded_ref[e] > 0)
            def _():
                fill(e).wait()

        def tail(b):
            return pltpu.make_async_copy(zero_scr, xs_hbm.at[pl.ds(pl.multiple_of(b * tme, tme), tme)], sem_fill)

        n_used = ends_ref[N_EXPERTS - 1] // tme
        n_blocks = xs_hbm.shape[0] // tme
        lax.fori_loop(n_used, n_blocks, lambda b, c: (tail(b).start(), c)[1], 0)
        lax.fori_loop(n_used, n_blocks, lambda b, c: (tail(b).wait(), c)[1], 0)

    @pl.when(j == 0)
    def _():
        _stage_pos(pos_hbm, pos_smem, sem_pos, i, n_tok)

    def tok8(g, carry):
        t0 = pl.multiple_of(g * 8, 8)
        for u in range(8):
            for k in range(TOP_K):
                p = pos_smem[k * TOK_STEP + j * sub + t0 + u]
                pltpu.make_async_copy(h_ref.at[t0 + u], xs_hbm.at[p], sem_rows).start(priority=k % 2)
        return carry

    lax.fori_loop(0, sub // 8, tok8, 0)
    for k in range(TOP_K):
        pltpu.make_async_copy(h_ref, xs_hbm.at[pl.ds(0, sub)], sem_rows).wait()


def _dispatch(h, pos_flat, ends, padded, n_rows, tme):
    t, half = h.shape
    sub = min(TOK_SUB, t)
    grid_spec = pltpu.PrefetchScalarGridSpec(
        num_scalar_prefetch=2,
        grid=(t // TOK_STEP, TOK_STEP // sub),
        in_specs=[pl.BlockSpec(memory_space=pl.ANY),
                  pl.BlockSpec((sub, half), lambda i, j, en, pd: (i * (TOK_STEP // sub) + j, 0))],
        out_specs=pl.BlockSpec(memory_space=pl.ANY),
        scratch_shapes=[pltpu.SMEM((TOP_K * TOK_STEP,), jnp.int32),
                        pltpu.VMEM((tme, half), jnp.int32),
                        pltpu.SemaphoreType.DMA, pltpu.SemaphoreType.DMA, pltpu.SemaphoreType.DMA],
    )
    return pl.pallas_call(
        functools.partial(_dispatch_body, n_tok=t, tme=tme),
        grid_spec=grid_spec,
        out_shape=jax.ShapeDtypeStruct((n_rows, half), jnp.int32),
        compiler_params=_cp(("arbitrary", "arbitrary")),
        name="moe_dispatch",
    )(ends, padded, pos_flat, h)


def _combine_body(pos_hbm, ys_hbm, gw_ref, x_ref, sh_ref, mod_ref, fg_ref, o_ref, pos_smem, g_scr,
                  sem_pos, sem_rows, *, n_tok, final_norm):
    i, j = pl.program_id(0), pl.program_id(1)
    sub, d = x_ref.shape
    half = d // 2

    @pl.when(j == 0)
    def _():
        _stage_pos(pos_hbm, pos_smem, sem_pos, i, n_tok)

    def tok8(g, carry):
        t0 = pl.multiple_of(g * 8, 8)
        for u in range(8):
            for k in range(TOP_K):
                p = pos_smem[k * TOK_STEP + j * sub + t0 + u]
                pltpu.make_async_copy(ys_hbm.at[p], g_scr.at[k, t0 + u], sem_rows).start(priority=k % 2)
        return carry

    lax.fori_loop(0, sub // 8, tok8, 0)
    for k in range(TOP_K):
        pltpu.make_async_copy(ys_hbm.at[pl.ds(0, sub)], g_scr.at[k], sem_rows).wait()

    acc_lo = sh_ref[:, :half]
    acc_hi = sh_ref[:, half:]
    for k in range(TOP_K):
        lo, hi = _unpack_rows(g_scr[k])
        w = gw_ref[:, k:k + 1]
        acc_lo = acc_lo + w * lo
        acc_hi = acc_hi + w * hi
    gate = mod_ref[0, G_F:G_F + 1, :]
    y_lo = x_ref[:, :half] + gate[:, :half] * acc_lo
    y_hi = x_ref[:, half:] + gate[:, half:] * acc_hi
    if final_norm:
        ms = (jnp.sum(y_lo * y_lo, axis=-1, keepdims=True) + jnp.sum(y_hi * y_hi, axis=-1, keepdims=True)) / d
        r = lax.rsqrt(ms + 1e-6)
        y_lo = y_lo * r * fg_ref[:, :half]
        y_hi = y_hi * r * fg_ref[:, half:]
    o_ref[:, :half] = y_lo
    o_ref[:, half:] = y_hi


def _combine(ys, pos_flat, gw, x2, shared, mod, final_g, seq, final_norm):
    t, d = x2.shape
    sub = min(TOK_SUB, t)
    nj = TOK_STEP // sub
    row = lambda i, j: (i * nj + j, 0)
    return pl.pallas_call(
        functools.partial(_combine_body, n_tok=t, final_norm=final_norm),
        grid=(t // TOK_STEP, nj),
        in_specs=[pl.BlockSpec(memory_space=pl.ANY),
                  pl.BlockSpec(memory_space=pl.ANY),
                  pl.BlockSpec((sub, TOP_K), row),
                  pl.BlockSpec((sub, d), row),
                  pl.BlockSpec((sub, d), row),
                  pl.BlockSpec((1, 6, d), lambda i, j: ((i * nj + j) * sub // seq, 0, 0)),
                  pl.BlockSpec((1, d), lambda i, j: (0, 0))],
        out_specs=pl.BlockSpec((sub, d), row),
        out_shape=jax.ShapeDtypeStruct((t, d), F32),
        scratch_shapes=[pltpu.SMEM((TOP_K * TOK_STEP,), jnp.int32),
                        pltpu.VMEM((TOP_K, sub, d // 2), jnp.int32),
                        pltpu.SemaphoreType.DMA, pltpu.SemaphoreType.DMA],
        compiler_params=_cp(("arbitrary", "arbitrary")),
        name="moe_combine",
    )(pos_flat, ys, gw, x2, shared, mod, final_g.reshape(1, d))


def _dispatch_plan(eidx_t, rank_t, counts, tm):
    n_assign = eidx_t.size
    padded = (counts + tm - 1) // tm * tm
    ends = jnp.cumsum(padded)
    pstart = ends - padded
    pos_t = rank_t
    for e in range(N_EXPERTS):
        pos_t = pos_t + jnp.where(eidx_t == e, pstart[e], 0)
    n_blocks = -(-n_assign // tm) + N_EXPERTS
    blk_start = jnp.arange(n_blocks, dtype=jnp.int32) * tm
    block_exp = jnp.minimum(jnp.sum((ends[None, :] <= blk_start[:, None]).astype(jnp.int32), axis=1),
                            N_EXPERTS - 1)
    n_used = (ends[-1] // tm).astype(jnp.int32).reshape(1)
    return pos_t.reshape(-1), ends, padded, block_exp, n_used, n_blocks


def _moe(x2, g, mod, w_router, e_bias, wg, wu, wd, layer, swg, swu, swd, final_g, seq, tm, tme, final_norm):
    wr_hi = w_router.astype(BF16)
    wr_lo = (w_router - wr_hi.astype(F32)).astype(BF16)
    gate_and_router = jnp.concatenate([swg.astype(BF16), wr_hi, wr_lo], axis=1)
    h, logits_t, shared = _moe_pre(x2, g, mod, wr_hi, gate_and_router, swu.astype(BF16),
                                   swd.astype(BF16), seq, tm)
    eidx_t, rank_t, gw_t, cnt = _route(logits_t, e_bias, tm)
    counts = cnt[:, 0].astype(jnp.int32)
    pos_flat, ends, padded, block_exp, n_used, n_blocks = _dispatch_plan(eidx_t, rank_t, counts, tme)
    x_sorted = _dispatch(h, pos_flat, ends, padded, n_blocks * tme, tme)
    y_sorted = _experts(x_sorted, block_exp, n_used, wg, wu, wd, layer, tme)
    return _combine(y_sorted, pos_flat, gw_t.T, x2, shared, mod, final_g, seq, final_norm)


def kernel(x, c, positions, mod_w, mod_b, norm_mix, norm_ffn, hyb_w_in, hyb_w_out, lru_conv_w, lru_conv_b, lru_wa, lru_ba, lru_wx, lru_bx, lru_lambda, cfm_w1, cfm_b1, cfm_dw, cfm_dwb, cfm_ln_g, cfm_ln_b, cfm_w2, cfm_b2, moe_router, moe_bias, moe_wg, moe_wu, moe_wd, sh_wg, sh_wu, sh_wd, final_norm):
    batch, seq, d = x.shape
    t = batch * seq
    n_sel = min(INDEX_TOPK, seq // 4)
    tm = min(512, seq)
    tq = min(256, seq)
    tme = MOE_ROWS

    mod_all = _modulation(c, mod_w, mod_b).reshape(mod_w.shape[0], batch, 6, d)
    tab_a = _rope_tables(positions, ROT_ATTN, HEAD_DIM)
    tab_i = _rope_tables(positions, ROT_IDX, IDX_DIM)
    x2 = x.reshape(t, d)
    zero_bias = jnp.zeros((d,), F32)

    mod = mod_all[0]
    w_in = hyb_w_in[0]
    w_main = w_in[:, :MAIN_COLS].astype(BF16)
    w_idx = jnp.pad(w_in[:, MAIN_COLS:], ((0, 0), (0, IDX_PAD - IDX_COLS))).astype(BF16)
    proj = _nm_matmul(x2, norm_mix[0], mod, w_main, seq, tm, MAIN_COLS, BF16, "hyb_in_main")
    idx = _nm_matmul(x2, norm_mix[0], mod, w_idx, seq, tm, IDX_PAD, F32, "hyb_in_idx")
    y_lru = _rg_lru(proj, lru_conv_w[0], lru_conv_b[0], lru_wa[0], lru_ba[0], lru_wx[0], lru_bx[0],
                    lru_lambda[0], batch, seq)
    y_att = _dsa(proj, idx, tab_a, tab_i, batch, seq, n_sel, tq)
    x2 = _proj_residual([y_lru, y_att], hyb_w_out[0].astype(BF16), zero_bias, x2, mod, G_M, seq, tm,
                        "hyb_out")
    x2 = _moe(x2, norm_ffn[0], mod, moe_router[0], moe_bias[0], moe_wg, moe_wu, moe_wd, 0,
              sh_wg[0], sh_wu[0], sh_wd[0], final_norm, seq, tm, tme, False)

    mod = mod_all[1]
    u = _nm_glu(x2, norm_mix[1], mod, cfm_w1[0].astype(BF16), cfm_b1[0], seq, tm, d)
    z = _cfm_conv(u, cfm_dw[0], cfm_dwb[0], cfm_ln_g[0], cfm_ln_b[0], seq, tm)
    x2 = _proj_residual([z], cfm_w2[0].astype(BF16), cfm_b2[0], x2, mod, G_M, seq, tm, "cfm_out")
    out = _moe(x2, norm_ffn[1], mod, moe_router[1], moe_bias[1], moe_wg, moe_wu, moe_wd, 1,
               sh_wg[1], sh_wu[1], sh_wd[1], final_norm, seq, tm, tme, True)
    return out.reshape(batch, seq, d)
```

```python
import functools

import jax
import jax.numpy as jnp
from jax import lax
from jax.experimental import pallas as pl
from jax.experimental.pallas import tpu as pltpu

F32 = jnp.float32
BF16 = jnp.bfloat16

LRU_WIDTH = 1024
LRU_BW = 128
LRU_CONV = 4
LRU_C = 8.0
N_HEADS = 8
N_KV_HEADS = 2
HEAD_DIM = 128
ATT_WIDTH = N_HEADS * HEAD_DIM
KV_WIDTH = N_KV_HEADS * HEAD_DIM
IDX_HEADS = 8
IDX_DIM = 64
INDEX_TOPK = 256
ROPE_THETA = 500000.0
ROT_ATTN = HEAD_DIM // 4
ROT_IDX = IDX_DIM // 4
CFM_CONV = 31
N_EXPERTS = 64
TOP_K = 8
N_GROUPS = 8
TOPK_GROUPS = 4
ROUTED_SCALE = 2.5

MAIN_COLS = 2 * LRU_WIDTH + ATT_WIDTH + 2 * KV_WIDTH
IDX_COLS = IDX_HEADS * IDX_DIM + IDX_DIM + IDX_HEADS
IDX_PAD = 640

MOE_ROWS = 512
LANES = 128
VMEM_LIMIT = 56 * 1024 * 1024
INT_MIN = -2147483648
NEG_BIG = -1e30

SH_M, SC_M, G_M, SH_F, SC_F, G_F = range(6)


def _cp(sem):
    return pltpu.CompilerParams(dimension_semantics=sem, vmem_limit_bytes=VMEM_LIMIT)


def _weight_spec(block, index_map, n_col_blocks):
    if n_col_blocks == 1:
        return pl.BlockSpec(block, index_map, pipeline_mode=pl.Buffered(1))
    return pl.BlockSpec(block, index_map)


def _norm_mod(x, g, mod_ref, sh_row, sc_row):
    ms = jnp.mean(x * x, axis=-1, keepdims=True)
    y = x * lax.rsqrt(ms + 1e-6) * g
    return y * (1.0 + mod_ref[0, sc_row:sc_row + 1, :]) + mod_ref[0, sh_row:sh_row + 1, :]


def _mod_body(c_ref, w_ref, b_ref, o_ref):
    c = c_ref[...]
    cond = c * jax.nn.sigmoid(c)
    o_ref[0] = jnp.dot(cond, w_ref[0], preferred_element_type=F32,
                       precision=lax.Precision.HIGHEST) + b_ref[0]


def _modulation(c, mod_w, mod_b):
    depth, d, n = mod_w.shape
    b = c.shape[0]
    tn = 512
    return pl.pallas_call(
        _mod_body,
        grid=(depth, n // tn),
        in_specs=[pl.BlockSpec((b, d), lambda l, j: (0, 0)),
                  pl.BlockSpec((1, d, tn), lambda l, j: (l, 0, j)),
                  pl.BlockSpec((1, 1, tn), lambda l, j: (l, 0, j))],
        out_specs=pl.BlockSpec((1, b, tn), lambda l, j: (l, 0, j)),
        out_shape=jax.ShapeDtypeStruct((depth, b, n), F32),
        compiler_params=_cp(("arbitrary", "arbitrary")),
        name="modulation",
    )(c, mod_w, mod_b.reshape(depth, 1, n))


def _nm_matmul_body(x_ref, g_ref, mod_ref, w_ref, o_ref, h_scr):
    @pl.when(pl.program_id(1) == 0)
    def _():
        h_scr[...] = _norm_mod(x_ref[...], g_ref[...], mod_ref, SH_M, SC_M).astype(BF16)

    o_ref[...] = jnp.dot(h_scr[...], w_ref[...], preferred_element_type=F32).astype(o_ref.dtype)


def _nm_matmul(x2, g, mod, w, seq, tm, tn, out_dtype, name):
    t, d = x2.shape
    n = w.shape[1]
    return pl.pallas_call(
        _nm_matmul_body,
        grid=(t // tm, n // tn),
        in_specs=[pl.BlockSpec((tm, d), lambda i, j: (i, 0)),
                  pl.BlockSpec((1, d), lambda i, j: (0, 0)),
                  pl.BlockSpec((1, 6, d), lambda i, j: (i * tm // seq, 0, 0)),
                  _weight_spec((d, tn), lambda i, j: (0, j), n // tn)],
        out_specs=pl.BlockSpec((tm, tn), lambda i, j: (i, j)),
        out_shape=jax.ShapeDtypeStruct((t, n), out_dtype),
        scratch_shapes=[pltpu.VMEM((tm, d), BF16)],
        compiler_params=_cp(("arbitrary", "arbitrary")),
        name=name,
    )(x2, g.reshape(1, d), mod, w)


def _nm_glu_body(x_ref, g_ref, mod_ref, wa_ref, wg_ref, ba_ref, bg_ref, o_ref, h_scr):
    @pl.when(pl.program_id(1) == 0)
    def _():
        h_scr[...] = _norm_mod(x_ref[...], g_ref[...], mod_ref, SH_M, SC_M).astype(BF16)

    h = h_scr[...]
    a = jnp.dot(h, wa_ref[...], preferred_element_type=F32) + ba_ref[...]
    gt = jnp.dot(h, wg_ref[...], preferred_element_type=F32) + bg_ref[...]
    o_ref[...] = (a * jax.nn.sigmoid(gt)).astype(o_ref.dtype)


def _nm_glu(x2, g, mod, w1, b1, seq, tm, tn):
    t, d = x2.shape
    n = w1.shape[1] // 2
    nj = n // tn
    return pl.pallas_call(
        _nm_glu_body,
        grid=(t // tm, nj),
        in_specs=[pl.BlockSpec((tm, d), lambda i, j: (i, 0)),
                  pl.BlockSpec((1, d), lambda i, j: (0, 0)),
                  pl.BlockSpec((1, 6, d), lambda i, j: (i * tm // seq, 0, 0)),
                  _weight_spec((d, tn), lambda i, j: (0, j), nj),
                  _weight_spec((d, tn), lambda i, j: (0, j + nj), nj),
                  pl.BlockSpec((1, tn), lambda i, j: (0, j)),
                  pl.BlockSpec((1, tn), lambda i, j: (0, j + nj))],
        out_specs=pl.BlockSpec((tm, tn), lambda i, j: (i, j)),
        out_shape=jax.ShapeDtypeStruct((t, n), BF16),
        scratch_shapes=[pltpu.VMEM((tm, d), BF16)],
        compiler_params=_cp(("arbitrary", "arbitrary")),
        name="cfm_in_glu",
    )(x2, g.reshape(1, d), mod, w1, w1, b1.reshape(1, 2 * n), b1.reshape(1, 2 * n))


def _proj_res_body(*refs, n_parts, gate_row):
    part_refs = refs[:n_parts]
    w_ref, b_ref, x_ref, mod_ref, o_ref = refs[n_parts:]
    acc = None
    off = 0
    for p in part_refs:
        k = p.shape[1]
        y = jnp.dot(p[...], w_ref[off:off + k, :], preferred_element_type=F32)
        acc = y if acc is None else acc + y
        off += k
    y = acc + b_ref[...]
    o_ref[...] = x_ref[...] + mod_ref[0, gate_row:gate_row + 1, :] * y


def _proj_residual(parts, w, bias, x2, mod, gate_row, seq, tm, name):
    t, d = x2.shape
    n = w.shape[1]
    in_specs = [pl.BlockSpec((tm, p.shape[1]), lambda i: (i, 0)) for p in parts]
    in_specs += [pl.BlockSpec(w.shape, lambda i: (0, 0)),
                 pl.BlockSpec((1, n), lambda i: (0, 0)),
                 pl.BlockSpec((tm, d), lambda i: (i, 0)),
                 pl.BlockSpec((1, 6, d), lambda i: (i * tm // seq, 0, 0))]
    return pl.pallas_call(
        functools.partial(_proj_res_body, n_parts=len(parts), gate_row=gate_row),
        grid=(t // tm,),
        in_specs=in_specs,
        out_specs=pl.BlockSpec((tm, n), lambda i: (i, 0)),
        out_shape=jax.ShapeDtypeStruct((t, n), F32),
        compiler_params=_cp(("arbitrary",)),
        name=name,
    )(*parts, w, bias.reshape(1, n), x2, mod)


def _shift_rows(x, d, fill, row):
    return jnp.where(row >= d, pltpu.roll(x, d, 0), fill)


def _lru_body(lx_ref, lg_ref, cw_ref, cb_ref, wa_ref, ba_ref, wx_ref, bx_ref, lam_ref, o_ref):
    s, cw = lx_ref.shape
    x = lx_ref[...].astype(F32)
    row = lax.broadcasted_iota(jnp.int32, (s, cw), 0)
    xc = x * cw_ref[LRU_CONV - 1:LRU_CONV, :] + cb_ref[...]
    for d in range(1, LRU_CONV):
        xc = xc + _shift_rows(x, d, 0.0, row) * cw_ref[LRU_CONV - 1 - d:LRU_CONV - d, :]
    xb = xc.astype(BF16)
    ga = jax.nn.sigmoid(jnp.dot(xb, wa_ref[0].astype(BF16), preferred_element_type=F32) + ba_ref[...])
    gx = jax.nn.sigmoid(jnp.dot(xb, wx_ref[0].astype(BF16), preferred_element_type=F32) + bx_ref[...])
    z = -lam_ref[...]
    softplus = jnp.maximum(z, 0.0) + jnp.log(1.0 + jnp.exp(-jnp.abs(z)))
    log_a = (-LRU_C) * ga * softplus
    a = jnp.exp(log_a)
    mult = jnp.sqrt(1.0 - a * a)
    mult = jnp.where(row == 0, 1.0, mult)
    b = gx * xc * mult
    d = 1
    while d < s:
        a_sh = _shift_rows(a, d, 1.0, row)
        b_sh = _shift_rows(b, d, 0.0, row)
        b = a * b_sh + b
        a = a * a_sh
        d *= 2
    lg = lg_ref[...].astype(F32)
    gelu = 0.5 * lg * (1.0 + jnp.tanh(0.7978845608028654 * (lg + 0.044715 * lg * lg * lg)))
    o_ref[...] = (b * gelu).astype(o_ref.dtype)


def _rg_lru(proj, conv_w, conv_b, wa, ba, wx, bx, lam, batch, seq):
    cw = LRU_BW
    nc = LRU_WIDTH // cw
    vec = lambda v: v.reshape(1, LRU_WIDTH)
    vspec = pl.BlockSpec((1, cw), lambda b, c: (0, c))
    return pl.pallas_call(
        _lru_body,
        grid=(batch, nc),
        in_specs=[pl.BlockSpec((seq, cw), lambda b, c: (b, c)),
                  pl.BlockSpec((seq, cw), lambda b, c: (b, c + nc)),
                  pl.BlockSpec((LRU_CONV, cw), lambda b, c: (0, c)),
                  vspec,
                  pl.BlockSpec((1, cw, cw), lambda b, c: (c, 0, 0)),
                  vspec,
                  pl.BlockSpec((1, cw, cw), lambda b, c: (c, 0, 0)),
                  vspec, vspec],
        out_specs=pl.BlockSpec((seq, cw), lambda b, c: (b, c)),
        out_shape=jax.ShapeDtypeStruct((batch * seq, LRU_WIDTH), BF16),
        compiler_params=_cp(("arbitrary", "arbitrary")),
        name="rg_lru",
    )(proj, proj, conv_w, vec(conv_b), wa, vec(ba), wx, vec(bx), vec(lam))


def _rope(x, c, s1, s2, half):
    w = x.shape[1]
    return x * c + pltpu.roll(x, w - half, 1) * s1 + pltpu.roll(x, half, 1) * s2


def _sort_key(x):
    bits = pltpu.bitcast(x, jnp.int32)
    return bits ^ (jnp.right_shift(bits, 31) & 0x7FFFFFFF)


def _dsa_tile(sk, q_ref, va_scr, iq_ref, ta_q_ref, ti_q_ref, o_ref, kr_scr, ikr_scr, key_scr, bias_scr, n_sel):
    qi = pl.program_id(1)
    tq = q_ref.shape[0]
    per_group = N_HEADS // N_KV_HEADS

    ci, si1, si2 = ti_q_ref[0], ti_q_ref[1], ti_q_ref[2]
    w_scale = (IDX_HEADS ** -0.5) * (IDX_DIM ** -0.5)
    iw = iq_ref[:, IDX_HEADS * IDX_DIM + IDX_DIM:IDX_HEADS * IDX_DIM + LANES] * w_scale
    ikr = ikr_scr[:sk, :]
    score = jnp.zeros((tq, sk), F32)
    for hp in range(IDX_HEADS // 2):
        pair = _rope(iq_ref[:, hp * LANES:(hp + 1) * LANES], ci, si1, si2, ROT_IDX // 2).astype(BF16)
        for sub in range(2):
            h = 2 * hp + sub
            qh = pair[:, sub * IDX_DIM:(sub + 1) * IDX_DIM]
            dots = lax.dot_general(qh, ikr, (((1,), (1,)), ((), ())), preferred_element_type=F32)
            score = score + jnp.maximum(dots, 0.0) * iw[:, h:h + 1]

    t_row = qi * tq + lax.broadcasted_iota(jnp.int32, (tq, sk), 0)
    col = lax.broadcasted_iota(jnp.int32, (tq, sk), 1)
    causal = col <= t_row
    key_scr[:, :sk] = jnp.where(causal, _sort_key(score), INT_MIN)

    kf = float(n_sel)

    def count_ge(cand):
        return jnp.sum((key_scr[:, :sk] >= cand).astype(F32), axis=1, keepdims=True)

    tau0 = jnp.where(count_ge(jnp.zeros((tq, 1), jnp.int32)) >= kf, 0, INT_MIN).astype(jnp.int32)

    def bis(i, tau):
        cand = tau + jnp.left_shift(jnp.int32(1), 30 - i)
        return jnp.where(count_ge(cand) >= kf, cand, tau)

    tau = lax.fori_loop(0, 31, bis, tau0)

    keys = key_scr[:, :sk]
    n_gt = jnp.sum((keys > tau).astype(F32), axis=1, keepdims=True)
    n_ge = jnp.sum((keys >= tau).astype(F32), axis=1, keepdims=True)
    need = kf - n_gt
    tie = keys == tau
    excess = jnp.logical_and(n_ge > kf, tau > INT_MIN)
    any_excess = jnp.max(excess.astype(F32)) > 0.0

    bias_scr[:, :sk] = jnp.where(jnp.logical_and(keys >= tau, causal), 0.0, NEG_BIG)

    @pl.when(any_excess)
    def _():
        nbits = sk.bit_length()

        def jb(i, jcur):
            cand = jcur + jnp.left_shift(jnp.int32(1), nbits - 1 - i)
            cnt = jnp.sum(jnp.logical_and(tie, col < cand).astype(F32), axis=1, keepdims=True)
            return jnp.where(cnt <= need, cand, jcur)

        jlim = lax.fori_loop(0, nbits, jb, jnp.zeros((tq, 1), jnp.int32))
        jlim = jnp.where(excess, jlim, sk)
        sel = jnp.logical_or(keys > tau, jnp.logical_and(tie, col < jlim))
        bias_scr[:, :sk] = jnp.where(jnp.logical_and(sel, causal), 0.0, NEG_BIG)

    ca, sa1, sa2 = ta_q_ref[0], ta_q_ref[1], ta_q_ref[2]
    scale = HEAD_DIM ** -0.5
    for h in range(N_HEADS):
        g = h // per_group
        qh = _rope(q_ref[:, h * HEAD_DIM:(h + 1) * HEAD_DIM].astype(F32), ca, sa1, sa2, ROT_ATTN // 2)
        qh = (qh * scale).astype(BF16)
        kg = kr_scr[:sk, g * HEAD_DIM:(g + 1) * HEAD_DIM]
        logits = lax.dot_general(qh, kg, (((1,), (1,)), ((), ())), preferred_element_type=F32)
        logits = logits + bias_scr[:, :sk]
        m = jnp.max(logits, axis=1, keepdims=True)
        p = jnp.exp(logits - m).astype(BF16)
        pv = jnp.dot(p, va_scr[:sk, 2 * g * HEAD_DIM:(2 * g + 2) * HEAD_DIM], preferred_element_type=F32)
        o = pv[:, :HEAD_DIM] / pv[:, HEAD_DIM:HEAD_DIM + 1]
        o_ref[:, h * HEAD_DIM:(h + 1) * HEAD_DIM] = o.astype(o_ref.dtype)


def _dsa_body(q_ref, k_ref, v_ref, iq_ref, ik_ref, ta_q_ref, ta_k_ref, ti_q_ref, ti_k_ref,
              o_ref, kr_scr, va_scr, ikr_scr, key_scr, bias_scr, *, n_sel):
    qi = pl.program_id(1)
    s = k_ref.shape[0]

    @pl.when(qi == 0)
    def _():
        ca, sa1, sa2 = ta_k_ref[0], ta_k_ref[1], ta_k_ref[2]
        for g in range(N_KV_HEADS):
            kg = k_ref[:, g * HEAD_DIM:(g + 1) * HEAD_DIM].astype(F32)
            kr_scr[:, g * HEAD_DIM:(g + 1) * HEAD_DIM] = _rope(kg, ca, sa1, sa2, ROT_ATTN // 2).astype(BF16)
            va_scr[:, 2 * g * HEAD_DIM:(2 * g + 1) * HEAD_DIM] = v_ref[:, g * HEAD_DIM:(g + 1) * HEAD_DIM]
            va_scr[:, (2 * g + 1) * HEAD_DIM:(2 * g + 2) * HEAD_DIM] = jnp.ones((s, HEAD_DIM), BF16)
        ikp = _rope(ik_ref[...], ti_k_ref[0], ti_k_ref[1], ti_k_ref[2], ROT_IDX // 2)
        ikr_scr[...] = ikp[:, :IDX_DIM].astype(BF16)

    _dsa_tile(s, q_ref, va_scr, iq_ref, ta_q_ref, ti_q_ref, o_ref, kr_scr, ikr_scr, key_scr, bias_scr, n_sel)


def _dsa(proj, idx, tab_a, tab_i, batch, seq, n_sel, tq):
    nq = seq // tq
    q_col = 2 * LRU_WIDTH // ATT_WIDTH
    k_col = (2 * LRU_WIDTH + ATT_WIDTH) // KV_WIDTH
    ik_col = IDX_HEADS * IDX_DIM // LANES
    return pl.pallas_call(
        functools.partial(_dsa_body, n_sel=n_sel),
        grid=(batch, nq),
        in_specs=[pl.BlockSpec((tq, ATT_WIDTH), lambda b, i: (b * nq + i, q_col)),
                  pl.BlockSpec((seq, KV_WIDTH), lambda b, i: (b, k_col)),
                  pl.BlockSpec((seq, KV_WIDTH), lambda b, i: (b, k_col + 1)),
                  pl.BlockSpec((tq, IDX_PAD), lambda b, i: (b * nq + i, 0)),
                  pl.BlockSpec((seq, LANES), lambda b, i: (b, ik_col)),
                  pl.BlockSpec((3, tq, LANES), lambda b, i: (0, b * nq + i, 0)),
                  pl.BlockSpec((3, seq, LANES), lambda b, i: (0, b, 0)),
                  pl.BlockSpec((3, tq, LANES), lambda b, i: (0, b * nq + i, 0)),
                  pl.BlockSpec((3, seq, LANES), lambda b, i: (0, b, 0))],
        out_specs=pl.BlockSpec((tq, ATT_WIDTH), lambda b, i: (b * nq + i, 0)),
        out_shape=jax.ShapeDtypeStruct((batch * seq, ATT_WIDTH), BF16),
        scratch_shapes=[pltpu.VMEM((seq, KV_WIDTH), BF16),
                        pltpu.VMEM((seq, 2 * KV_WIDTH), BF16),
                        pltpu.VMEM((seq, IDX_DIM), BF16),
                        pltpu.VMEM((tq, seq), jnp.int32),
                        pltpu.VMEM((tq, seq), F32)],
        compiler_params=_cp(("arbitrary", "arbitrary")),
        name="dsa_attention",
    )(proj, proj, proj, idx, idx, tab_a, tab_a, tab_i, tab_i)


def _rope_tables(positions, rot_dim, period):
    half = rot_dim // 2
    inv = jnp.power(jnp.float32(ROPE_THETA), -jnp.arange(0, rot_dim, 2, dtype=F32) / rot_dim)
    ang = positions.astype(F32).reshape(-1, 1) * inv
    cos, sin = jnp.cos(ang), jnp.sin(ang)
    t = cos.shape[0]
    ones = jnp.ones((t, period - rot_dim), F32)
    zeros = jnp.zeros((t, period - rot_dim), F32)
    zh = jnp.zeros((t, half), F32)
    c = jnp.concatenate([cos, cos, ones], axis=1)
    s1 = jnp.concatenate([-sin, zh, zeros], axis=1)
    s2 = jnp.concatenate([zh, sin, zeros], axis=1)
    reps = LANES // period
    return jnp.stack([jnp.tile(c, (1, reps)), jnp.tile(s1, (1, reps)), jnp.tile(s2, (1, reps))])


HALO = 32


def _cfm_conv_body(u_ref, halo_ref, dw_ref, dwb_ref, g_ref, b_ref, o_ref, ext_scr, acc_scr, *, seq, rc, cc):
    tm, c = u_ref.shape
    i = pl.program_id(0)
    at_start = (i * tm) % seq == 0
    halo = jnp.where(at_start, 0.0, halo_ref[...].astype(F32))
    ext_scr[0:HALO, :] = halo
    ext_scr[HALO:, :] = u_ref[...].astype(F32)
    base = HALO - (CFM_CONV - 1)

    def chunk(r, carry):
        r0 = pl.multiple_of(r * rc, rc)
        for c0 in range(0, c, cc):
            win = ext_scr[pl.ds(r0, rc + HALO), c0:c0 + cc]
            acc = jnp.zeros((rc, cc), F32) + dwb_ref[:, c0:c0 + cc]
            for sub in range(8):
                rolled = win if sub == 0 else pltpu.roll(win, rc + HALO - sub, 0)
                for j in range(CFM_CONV):
                    off = base + j
                    if off % 8 == sub:
                        a0 = off - sub
                        acc = acc + rolled[a0:a0 + rc, :] * dw_ref[j:j + 1, c0:c0 + cc]
            acc_scr[pl.ds(r0, rc), c0:c0 + cc] = acc
        return carry

    lax.fori_loop(0, tm // rc, chunk, 0)
    y = acc_scr[...]
    mu = jnp.mean(y, axis=-1, keepdims=True)
    yc = y - mu
    var = jnp.mean(yc * yc, axis=-1, keepdims=True)
    z = yc * lax.rsqrt(var + 1e-5) * g_ref[...] + b_ref[...]
    o_ref[...] = (z * jax.nn.sigmoid(z)).astype(o_ref.dtype)


def _cfm_conv(u, dw, dwb, ln_g, ln_b, seq, tm):
    t, c = u.shape
    hb = tm // HALO
    vec = lambda v: v.reshape(1, c)
    vspec = pl.BlockSpec((1, c), lambda i: (0, 0))
    return pl.pallas_call(
        functools.partial(_cfm_conv_body, seq=seq, rc=32, cc=256),
        grid=(t // tm,),
        in_specs=[pl.BlockSpec((tm, c), lambda i: (i, 0)),
                  pl.BlockSpec((HALO, c), lambda i: (jnp.maximum(i * hb - 1, 0), 0)),
                  pl.BlockSpec((CFM_CONV, c), lambda i: (0, 0)),
                  vspec, vspec, vspec],
        out_specs=pl.BlockSpec((tm, c), lambda i: (i, 0)),
        out_shape=jax.ShapeDtypeStruct((t, c), BF16),
        scratch_shapes=[pltpu.VMEM((tm + HALO, c), F32), pltpu.VMEM((tm, c), F32)],
        compiler_params=_cp(("arbitrary",)),
        name="cfm_conv_ln",
    )(u, u, dw, vec(dwb), vec(ln_g), vec(ln_b))


HI_MASK = -65536


def _pack_rows(y):
    c = y.shape[1] // 2
    lo = pltpu.bitcast(y[:, :c].astype(BF16).astype(F32), jnp.int32)
    hi = pltpu.bitcast(y[:, c:].astype(BF16).astype(F32), jnp.int32)
    return (hi & HI_MASK) | (jnp.right_shift(lo, 16) & 0xFFFF)


def _unpack_rows(w):
    lo = pltpu.bitcast(jnp.left_shift(w, 16), F32)
    hi = pltpu.bitcast(w & HI_MASK, F32)
    return lo, hi


def _moe_pre_body(x_ref, g_ref, mod_ref, wr_ref, swg_ref, swu_ref, swd_ref, h_ref, lg_ref, sh_ref):
    h = _norm_mod(x_ref[...], g_ref[...], mod_ref, SH_F, SC_F)
    hb = h.astype(BF16)
    h_ref[...] = _pack_rows(h)
    e = wr_ref.shape[1]
    ff = swu_ref.shape[1]
    r = jnp.dot(hb, swg_ref[...], preferred_element_type=F32)
    h_lo = (h - hb.astype(F32)).astype(BF16)
    lg = r[:, ff:ff + e] + r[:, ff + e:ff + 2 * e] + jnp.dot(h_lo, wr_ref[...], preferred_element_type=F32)
    lg_ref[...] = lg.T
    gt = r[:, :ff]
    up = jnp.dot(hb, swu_ref[...], preferred_element_type=F32)
    mid = (gt * jax.nn.sigmoid(gt) * up).astype(BF16)
    sh_ref[...] = jnp.dot(mid, swd_ref[...], preferred_element_type=F32)


def _moe_pre(x2, g, mod, w_router, swg, swu, swd, seq, tm):
    t, d = x2.shape
    e = w_router.shape[1]
    full = lambda a: pl.BlockSpec(a.shape, lambda i: (0, 0))
    return pl.pallas_call(
        _moe_pre_body,
        grid=(t // tm,),
        in_specs=[pl.BlockSpec((tm, d), lambda i: (i, 0)),
                  pl.BlockSpec((1, d), lambda i: (0, 0)),
                  pl.BlockSpec((1, 6, d), lambda i: (i * tm // seq, 0, 0)),
                  full(w_router), full(swg), full(swu), full(swd)],
        out_specs=[pl.BlockSpec((tm, d // 2), lambda i: (i, 0)),
                   pl.BlockSpec((e, tm), lambda i: (0, i)),
                   pl.BlockSpec((tm, d), lambda i: (i, 0))],
        out_shape=[jax.ShapeDtypeStruct((t, d // 2), jnp.int32),
                   jax.ShapeDtypeStruct((e, t), F32),
                   jax.ShapeDtypeStruct((t, d), F32)],
        compiler_params=_cp(("arbitrary",)),
        name="moe_pre",
    )(x2, g.reshape(1, d), mod, w_router, swg, swu, swd)


def _first_max(v, ids, sentinel):
    m = jnp.max(v, axis=0, keepdims=True)
    first = jnp.min(jnp.where(v == m, ids, sentinel), axis=0, keepdims=True)
    return m, first


def _route_body(lg_ref, bias_ref, eidx_ref, rank_ref, gw_ref, cnt_ref):
    e, tm = lg_ref.shape
    gsz = e // N_GROUPS
    neg = -jnp.inf

    @pl.when(pl.program_id(0) == 0)
    def _():
        cnt_ref[...] = jnp.zeros_like(cnt_ref)

    scores = jax.nn.sigmoid(lg_ref[...])
    choice = scores + bias_ref[:, 0:1]
    row = lax.broadcasted_iota(jnp.int32, (e, tm), 0)
    sub = lax.broadcasted_iota(jnp.int32, (gsz, tm), 0)

    parts = []
    for g in range(N_GROUPS):
        vg = choice[g * gsz:(g + 1) * gsz, :]
        m1, f1 = _first_max(vg, sub, gsz)
        m2 = jnp.max(jnp.where(sub == f1, neg, vg), axis=0, keepdims=True)
        parts.append(jnp.broadcast_to(m1 + m2, (gsz, tm)))
    gscore = jnp.concatenate(parts, axis=0)

    gid = jnp.right_shift(row, gsz.bit_length() - 1)
    v = jnp.full((e, tm), neg, F32)
    for _ in range(TOPK_GROUPS):
        _, fg = _first_max(gscore, gid, N_GROUPS)
        hit = gid == fg
        v = jnp.where(hit, choice, v)
        gscore = jnp.where(hit, neg, gscore)

    picks = []
    self32 = jnp.zeros((e, tm), F32)
    for _ in range(TOP_K):
        _, fe = _first_max(v, row, e)
        hit = row == fe
        picks.append(fe)
        self32 = jnp.where(hit, 1.0, self32)
        v = jnp.where(hit, neg, v)

    picked = self32 * scores
    gwd = picked / jnp.sum(picked, axis=0, keepdims=True) * ROUTED_SCALE

    before = (lax.broadcasted_iota(jnp.int32, (tm, tm), 0)
              < lax.broadcasted_iota(jnp.int32, (tm, tm), 1)).astype(BF16)
    rank = jnp.dot(self32.astype(BF16), before, preferred_element_type=F32) + cnt_ref[:, 0:1]
    cnt_ref[...] = cnt_ref[...] + jnp.sum(self32, axis=1, keepdims=True)

    for k in range(TOP_K):
        hit = row == picks[k]
        eidx_ref[k:k + 1, :] = picks[k]
        rank_ref[k:k + 1, :] = jnp.sum(jnp.where(hit, rank, 0.0), axis=0, keepdims=True).astype(jnp.int32)
        gw_ref[k:k + 1, :] = jnp.sum(jnp.where(hit, gwd, 0.0), axis=0, keepdims=True)


def _route(logits_t, e_bias, tm):
    e, t = logits_t.shape
    kspec = pl.BlockSpec((TOP_K, tm), lambda i: (0, i))
    return pl.pallas_call(
        _route_body,
        grid=(t // tm,),
        in_specs=[pl.BlockSpec((e, tm), lambda i: (0, i)),
                  pl.BlockSpec((e, LANES), lambda i: (0, 0))],
        out_specs=[kspec, kspec, kspec, pl.BlockSpec((e, LANES), lambda i: (0, 0))],
        out_shape=[jax.ShapeDtypeStruct((TOP_K, t), jnp.int32),
                   jax.ShapeDtypeStruct((TOP_K, t), jnp.int32),
                   jax.ShapeDtypeStruct((TOP_K, t), F32),
                   jax.ShapeDtypeStruct((e, LANES), F32)],
        compiler_params=_cp(("arbitrary",)),
        name="moe_route",
    )(logits_t, jnp.broadcast_to(e_bias.astype(F32).reshape(e, 1), (e, LANES)))


def _experts_body(be_ref, nu_ref, x_ref, wg_ref, wu_ref, wd_ref, o_ref, wg_scr, wu_scr, wd_scr):
    i = pl.program_id(0)
    half = x_ref.shape[1]

    @pl.when(i >= nu_ref[0])
    def _():
        o_ref[...] = jnp.zeros_like(o_ref)

    @pl.when(i < nu_ref[0])
    def _():
        prev = be_ref[jnp.maximum(i - 1, 0)]
        changed = jnp.logical_or(i == 0, be_ref[i] != prev)

        @pl.when(changed)
        def _():
            wg_scr[...] = wg_ref[0, 0].astype(BF16)
            wu_scr[...] = wu_ref[0, 0].astype(BF16)
            wd_scr[...] = wd_ref[0, 0].astype(BF16)

        lo, hi = _unpack_rows(x_ref[...])
        lo = lo.astype(BF16)
        hi = hi.astype(BF16)
        gt = (jnp.dot(lo, wg_scr[:half, :], preferred_element_type=F32)
              + jnp.dot(hi, wg_scr[half:, :], preferred_element_type=F32))
        up = (jnp.dot(lo, wu_scr[:half, :], preferred_element_type=F32)
              + jnp.dot(hi, wu_scr[half:, :], preferred_element_type=F32))
        mid = (gt * jax.nn.sigmoid(gt) * up).astype(BF16)
        o_ref[...] = _pack_rows(jnp.dot(mid, wd_scr[...], preferred_element_type=F32))


def _experts(x_sorted, block_exp, n_used, wg, wu, wd, layer, tm):
    n_rows, half = x_sorted.shape
    d, ff = wg.shape[2], wg.shape[3]
    n_blocks = n_rows // tm
    row_map = lambda i, be, nu: (jnp.minimum(i, nu[0] - 1), 0)
    grid_spec = pltpu.PrefetchScalarGridSpec(
        num_scalar_prefetch=2,
        grid=(n_blocks,),
        in_specs=[pl.BlockSpec((tm, half), row_map),
                  pl.BlockSpec((1, 1, d, ff), lambda i, be, nu: (layer, be[i], 0, 0)),
                  pl.BlockSpec((1, 1, d, ff), lambda i, be, nu: (layer, be[i], 0, 0)),
                  pl.BlockSpec((1, 1, ff, d), lambda i, be, nu: (layer, be[i], 0, 0))],
        out_specs=pl.BlockSpec((tm, half), lambda i, be, nu: (i, 0)),
        scratch_shapes=[pltpu.VMEM((d, ff), BF16), pltpu.VMEM((d, ff), BF16), pltpu.VMEM((ff, d), BF16)],
    )
    return pl.pallas_call(
        _experts_body,
        grid_spec=grid_spec,
        out_shape=jax.ShapeDtypeStruct((n_rows, half), jnp.int32),
        compiler_params=_cp(("arbitrary",)),
        name="moe_experts",
    )(block_exp, n_used, x_sorted, wg, wu, wd)


TOK_STEP = 1024
TOK_SUB = 256


def _stage_pos(pos_hbm, pos_smem, sem, step, n_tok):
    copies = [pltpu.make_async_copy(pos_hbm.at[pl.ds(k * n_tok + step * TOK_STEP, TOK_STEP)],
                                    pos_smem.at[pl.ds(k * TOK_STEP, TOK_STEP)], sem)
              for k in range(TOP_K)]
    for cp in copies:
        cp.start()
    for cp in copies:
        cp.wait()


def _dispatch_body(ends_ref, padded_ref, pos_hbm, h_ref, xs_hbm, pos_smem, zero_scr, sem_pos, sem_fill, sem_rows,
                   *, n_tok, tme):
    i, j = pl.program_id(0), pl.program_id(1)
    sub = h_ref.shape[0]

    @pl.when(jnp.logical_and(i == 0, j == 0))
    def _():
        zero_scr[...] = jnp.zeros_like(zero_scr)

        def fill(e):
            start = pl.multiple_of(ends_ref[e] - tme, tme)
            return pltpu.make_async_copy(zero_scr, xs_hbm.at[pl.ds(start, tme)], sem_fill)

        for e in range(N_EXPERTS):
            @pl.when(padded_ref[e] > 0)
            def _():
                fill(e).start()
        for e in range(N_EXPERTS):
            @pl.when(padded_ref[e] > 0)
            def _():
                fill(e).wait()

        def tail(b):
            return pltpu.make_async_copy(zero_scr, xs_hbm.at[pl.ds(pl.multiple_of(b * tme, tme), tme)], sem_fill)

        n_used = ends_ref[N_EXPERTS - 1] // tme
        n_blocks = xs_hbm.shape[0] // tme
        lax.fori_loop(n_used, n_blocks, lambda b, c: (tail(b).start(), c)[1], 0)
        lax.fori_loop(n_used, n_blocks, lambda b, c: (tail(b).wait(), c)[1], 0)

    @pl.when(j == 0)
    def _():
        _stage_pos(pos_hbm, pos_smem, sem_pos, i, n_tok)

    def tok8(g, carry):
        t0 = pl.multiple_of(g * 8, 8)
        for u in range(8):
            for k in range(TOP_K):
                p = pos_smem[k * TOK_STEP + j * sub + t0 + u]
                pltpu.make_async_copy(h_ref.at[t0 + u], xs_hbm.at[p], sem_rows).start(priority=k % 2)
        return carry

    lax.fori_loop(0, sub // 8, tok8, 0)
    for k in range(TOP_K):
        pltpu.make_async_copy(h_ref, xs_hbm.at[pl.ds(0, sub)], sem_rows).wait()


def _dispatch(h, pos_flat, ends, padded, n_rows, tme):
    t, half = h.shape
    sub = min(TOK_SUB, t)
    grid_spec = pltpu.PrefetchScalarGridSpec(
        num_scalar_prefetch=2,
        grid=(t // TOK_STEP, TOK_STEP // sub),
        in_specs=[pl.BlockSpec(memory_space=pl.ANY),
                  pl.BlockSpec((sub, half), lambda i, j, en, pd: (i * (TOK_STEP // sub) + j, 0))],
        out_specs=pl.BlockSpec(memory_space=pl.ANY),
        scratch_shapes=[pltpu.SMEM((TOP_K * TOK_STEP,), jnp.int32),
                        pltpu.VMEM((tme, half), jnp.int32),
                        pltpu.SemaphoreType.DMA, pltpu.SemaphoreType.DMA, pltpu.SemaphoreType.DMA],
    )
    return pl.pallas_call(
        functools.partial(_dispatch_body, n_tok=t, tme=tme),
        grid_spec=grid_spec,
        out_shape=jax.ShapeDtypeStruct((n_rows, half), jnp.int32),
        compiler_params=_cp(("arbitrary", "arbitrary")),
        name="moe_dispatch",
    )(ends, padded, pos_flat, h)


def _combine_body(pos_hbm, ys_hbm, gw_ref, x_ref, sh_ref, mod_ref, fg_ref, o_ref, pos_smem, g_scr,
                  sem_pos, sem_rows, *, n_tok, final_norm):
    i, j = pl.program_id(0), pl.program_id(1)
    sub, d = x_ref.shape
    half = d // 2

    @pl.when(j == 0)
    def _():
        _stage_pos(pos_hbm, pos_smem, sem_pos, i, n_tok)

    def tok8(g, carry):
        t0 = pl.multiple_of(g * 8, 8)
        for u in range(8):
            for k in range(TOP_K):
                p = pos_smem[k * TOK_STEP + j * sub + t0 + u]
                pltpu.make_async_copy(ys_hbm.at[p], g_scr.at[k, t0 + u], sem_rows).start(priority=k % 2)
        return carry

    lax.fori_loop(0, sub // 8, tok8, 0)
    for k in range(TOP_K):
        pltpu.make_async_copy(ys_hbm.at[pl.ds(0, sub)], g_scr.at[k], sem_rows).wait()

    acc_lo = sh_ref[:, :half]
    acc_hi = sh_ref[:, half:]
    for k in range(TOP_K):
        lo, hi = _unpack_rows(g_scr[k])
        w = gw_ref[:, k:k + 1]
        acc_lo = acc_lo + w * lo
        acc_hi = acc_hi + w * hi
    gate = mod_ref[0, G_F:G_F + 1, :]
    y_lo = x_ref[:, :half] + gate[:, :half] * acc_lo
    y_hi = x_ref[:, half:] + gate[:, half:] * acc_hi
    if final_norm:
        ms = (jnp.sum(y_lo * y_lo, axis=-1, keepdims=True) + jnp.sum(y_hi * y_hi, axis=-1, keepdims=True)) / d
        r = lax.rsqrt(ms + 1e-6)
        y_lo = y_lo * r * fg_ref[:, :half]
        y_hi = y_hi * r * fg_ref[:, half:]
    o_ref[:, :half] = y_lo
    o_ref[:, half:] = y_hi


def _combine(ys, pos_flat, gw, x2, shared, mod, final_g, seq, final_norm):
    t, d = x2.shape
    sub = min(TOK_SUB, t)
    nj = TOK_STEP // sub
    row = lambda i, j: (i * nj + j, 0)
    return pl.pallas_call(
        functools.partial(_combine_body, n_tok=t, final_norm=final_norm),
        grid=(t // TOK_STEP, nj),
        in_specs=[pl.BlockSpec(memory_space=pl.ANY),
                  pl.BlockSpec(memory_space=pl.ANY),
                  pl.BlockSpec((sub, TOP_K), row),
                  pl.BlockSpec((sub, d), row),
                  pl.BlockSpec((sub, d), row),
                  pl.BlockSpec((1, 6, d), lambda i, j: ((i * nj + j) * sub // seq, 0, 0)),
                  pl.BlockSpec((1, d), lambda i, j: (0, 0))],
        out_specs=pl.BlockSpec((sub, d), row),
        out_shape=jax.ShapeDtypeStruct((t, d), F32),
        scratch_shapes=[pltpu.SMEM((TOP_K * TOK_STEP,), jnp.int32),
                        pltpu.VMEM((TOP_K, sub, d // 2), jnp.int32),
                        pltpu.SemaphoreType.DMA, pltpu.SemaphoreType.DMA],
        compiler_params=_cp(("arbitrary", "arbitrary")),
        name="moe_combine",
    )(pos_flat, ys, gw, x2, shared, mod, final_g.reshape(1, d))


def _dispatch_plan(eidx_t, rank_t, counts, tm):
    n_assign = eidx_t.size
    padded = (counts + tm - 1) // tm * tm
    ends = jnp.cumsum(padded)
    pstart = ends - padded
    pos_t = rank_t
    for e in range(N_EXPERTS):
        pos_t = pos_t + jnp.where(eidx_t == e, pstart[e], 0)
    n_blocks = -(-n_assign // tm) + N_EXPERTS
    blk_start = jnp.arange(n_blocks, dtype=jnp.int32) * tm
    block_exp = jnp.minimum(jnp.sum((ends[None, :] <= blk_start[:, None]).astype(jnp.int32), axis=1),
                            N_EXPERTS - 1)
    n_used = (ends[-1] // tm).astype(jnp.int32).reshape(1)
    return pos_t.reshape(-1), ends, padded, block_exp, n_used, n_blocks


def _moe(x2, g, mod, w_router, e_bias, wg, wu, wd, layer, swg, swu, swd, final_g, seq, tm, tme, final_norm):
    wr_hi = w_router.astype(BF16)
    wr_lo = (w_router - wr_hi.astype(F32)).astype(BF16)
    gate_and_router = jnp.concatenate([swg.astype(BF16), wr_hi, wr_lo], axis=1)
    h, logits_t, shared = _moe_pre(x2, g, mod, wr_hi, gate_and_router, swu.astype(BF16),
                                   swd.astype(BF16), seq, tm)
    eidx_t, rank_t, gw_t, cnt = _route(logits_t, e_bias, tm)
    counts = cnt[:, 0].astype(jnp.int32)
    pos_flat, ends, padded, block_exp, n_used, n_blocks = _dispatch_plan(eidx_t, rank_t, counts, tme)
    x_sorted = _dispatch(h, pos_flat, ends, padded, n_blocks * tme, tme)
    y_sorted = _experts(x_sorted, block_exp, n_used, wg, wu, wd, layer, tme)
    return _combine(y_sorted, pos_flat, gw_t.T, x2, shared, mod, final_g, seq, final_norm)


def kernel(x, c, positions, mod_w, mod_b, norm_mix, norm_ffn, hyb_w_in, hyb_w_out, lru_conv_w, lru_conv_b, lru_wa, lru_ba, lru_wx, lru_bx, lru_lambda, cfm_w1, cfm_b1, cfm_dw, cfm_dwb, cfm_ln_g, cfm_ln_b, cfm_w2, cfm_b2, moe_router, moe_bias, moe_wg, moe_wu, moe_wd, sh_wg, sh_wu, sh_wd, final_norm):
    batch, seq, d = x.shape
    t = batch * seq
    n_sel = min(INDEX_TOPK, seq // 4)
    tm = min(512, seq)
    tq = min(256, seq)
    tme = MOE_ROWS

    mod_all = _modulation(c, mod_w, mod_b).reshape(mod_w.shape[0], batch, 6, d)
    tab_a = _rope_tables(positions, ROT_ATTN, HEAD_DIM)
    tab_i = _rope_tables(positions, ROT_IDX, IDX_DIM)
    x2 = x.reshape(t, d)
    zero_bias = jnp.zeros((d,), F32)

    mod = mod_all[0]
    w_in = hyb_w_in[0]
    w_main = w_in[:, :MAIN_COLS].astype(BF16)
    w_idx = jnp.pad(w_in[:, MAIN_COLS:], ((0, 0), (0, IDX_PAD - IDX_COLS))).astype(BF16)
    proj = _nm_matmul(x2, norm_mix[0], mod, w_main, seq, tm, MAIN_COLS, BF16, "hyb_in_main")
    idx = _nm_matmul(x2, norm_mix[0], mod, w_idx, seq, tm, IDX_PAD, F32, "hyb_in_idx")
    y_lru = _rg_lru(proj, lru_conv_w[0], lru_conv_b[0], lru_wa[0], lru_ba[0], lru_wx[0], lru_bx[0],
                    lru_lambda[0], batch, seq)
    y_att = _dsa(proj, idx, tab_a, tab_i, batch, seq, n_sel, tq)
    x2 = _proj_residual([y_lru, y_att], hyb_w_out[0].astype(BF16), zero_bias, x2, mod, G_M, seq, tm,
                        "hyb_out")
    x2 = _moe(x2, norm_ffn[0], mod, moe_router[0], moe_bias[0], moe_wg, moe_wu, moe_wd, 0,
              sh_wg[0], sh_wu[0], sh_wd[0], final_norm, seq, tm, tme, False)

    mod = mod_all[1]
    u = _nm_glu(x2, norm_mix[1], mod, cfm_w1[0].astype(BF16), cfm_b1[0], seq, tm, d)
    z = _cfm_conv(u, cfm_dw[0], cfm_dwb[0], cfm_ln_g[0], cfm_ln_b[0], seq, tm)
    x2 = _proj_residual([z], cfm_w2[0].astype(BF16), cfm_b2[0], x2, mod, G_M, seq, tm, "cfm_out")
    out = _moe(x2, norm_ffn[1], mod, moe_router[1], moe_bias[1], moe_wg, moe_wu, moe_wd, 1,
               sh_wg[1], sh_wu[1], sh_wd[1], final_norm, seq, tm, tme, True)
    return out.reshape(batch, seq, d)
```
